```python
import math
import jax
import jax.numpy as jnp
from jax import lax
import numpy as np

D_MODEL = 2048
BATCH = 32
SEQ = 256
DEPTH = 2
DEC_BATCH = 4
DEC_SEQ = 2048
PAST_LEN = 512

GRID_W = 64
N_DIR = 2
EPS = 1e-6
RWKV_WIDTH = 1024
RWKV_HEAD = 64
RWKV_HEADS = RWKV_WIDTH // RWKV_HEAD
RWKV_DECAY_RANK = 64
RWKV_AICL_RANK = 64
RWKV_GATE_RANK = 128
RWKV_LN_EPS = 64e-5
RWKV_COLS = 3 * RWKV_WIDTH + RWKV_DECAY_RANK + RWKV_AICL_RANK + RWKV_GATE_RANK
SSD_WIDTH = 1024
SSD_HEAD = 64
SSD_HEADS = SSD_WIDTH // SSD_HEAD
SSD_GROUPS = 2
SSD_STATE = 128
SSD_CONV = 5
SSD_CHUNK = 128
SSD_XBC = SSD_WIDTH + 2 * SSD_GROUPS * SSD_STATE
SSD_COLS = SSD_WIDTH + SSD_XBC + SSD_HEADS
S5_WIDTH = 1024
S5_GROUP = 16
S5_GROUPS = S5_WIDTH // S5_GROUP
S5_STATE = 64
D_FF = 5504
N_MOD = 9
D_IN = RWKV_COLS + SSD_COLS + S5_WIDTH + 3 * D_MODEL

kernel_name = "hybrid_rwkv7_ssd_s5_diffusion_step"


def _split(x, sizes):
    return jnp.split(x, np.cumsum(sizes)[:-1].tolist(), axis=-1)


def _rmsnorm(x, g):
    xf = x.astype(jnp.float32)
    y = xf * lax.rsqrt(jnp.mean(xf * xf, axis=-1, keepdims=True) + EPS)
    return (y * g.astype(jnp.float32)).astype(x.dtype)


def _swiglu(h, w_in, w_out):
    gate, up = jnp.split(h @ w_in, 2, axis=-1)
    return (jax.nn.silu(gate) * up) @ w_out


def _centred_shift(x, grid):
    bsz, t, ch = x.shape
    if grid:
        rows = t // GRID_W
        pg = jnp.pad(x.reshape(bsz, rows, GRID_W, ch), ((0, 0), (1, 1), (1, 1), (0, 0)))
        nb = 0.25 * (pg[:, :-2, 1:-1] + pg[:, 2:, 1:-1] + pg[:, 1:-1, :-2] + pg[:, 1:-1, 2:])
        return nb.reshape(bsz, t, ch)
    ps = jnp.pad(x, ((0, 0), (1, 1), (0, 0)))
    return 0.5 * (ps[:, :-2] + ps[:, 2:])


def _depthwise_conv(x, w, b):
    k, ch = w.shape
    y = lax.conv_general_dilated(x, w[:, None, :].astype(x.dtype), window_strides=(1,),
                                 padding=[(k // 2, k // 2)], dimension_numbers=("NWC", "WIO", "NWC"),
                                 feature_group_count=ch)
    return y + b


def _rwkv7_scan(r, w, k, v, kk, a, s0, reverse):
    def step(s, inp):
        r_t, w_t, k_t, v_t, kk_t, a_t = inp
        s_kk = jnp.einsum("bhvk,bhk->bhv", s, kk_t)
        s = (s * w_t[:, :, None, :] - s_kk[..., None] * (kk_t * a_t)[:, :, None, :]
             + v_t[..., None] * k_t[:, :, None, :])
        return s, jnp.einsum("bhvk,bhk->bhv", s, r_t)
    xs = tuple(jnp.moveaxis(u, 1, 0) for u in (r, w, k, v, kk, a))
    s_fin, out = lax.scan(step, s0, xs, reverse=reverse)
    return jnp.moveaxis(out, 0, 1), s_fin


def _rwkv7_mixer(stream, grid, s0, p, l):
    bsz, t, _ = stream.shape
    f32 = jnp.float32
    xs = stream + p["rwkv_mu"][l] * (_centred_shift(stream, grid) - stream)
    r, k, v, wl, al, gl = _split(xs.astype(f32), [RWKV_WIDTH] * 3 + [RWKV_DECAY_RANK, RWKV_AICL_RANK, RWKV_GATE_RANK])

    def heads(u):
        return u.reshape(bsz, t, RWKV_HEADS, RWKV_HEAD)

    kk = heads(k * p["rwkv_k_k"][l])
    kk = kk * lax.rsqrt(jnp.sum(kk * kk, axis=-1, keepdims=True) + 1e-12)
    s0 = s0.astype(f32)
    outs, finals = [], []
    for d in range(N_DIR):
        w_log = -jax.nn.softplus(-(p["rwkv_w0"][l, d] + jnp.tanh(wl) @ p["rwkv_w2"][l, d])) - 0.5
        decay = jnp.exp(-jnp.exp(w_log))
        a_d = jax.nn.sigmoid(p["rwkv_a0"][l, d] + al @ p["rwkv_a2"][l, d])
        k_d = k * (1.0 + (a_d - 1.0) * p["rwkv_k_a"][l])
        o_d, s_d = _rwkv7_scan(heads(r), heads(decay), heads(k_d), heads(v), kk, heads(a_d), s0[:, d], d == 1)
        outs.append(o_d)
        finals.append(s_d)
    o = outs[0] + outs[1]
    mu = jnp.mean(o, axis=-1, keepdims=True)
    var = jnp.mean(jnp.square(o - mu), axis=-1, keepdims=True)
    o = ((o - mu) * lax.rsqrt(var + RWKV_LN_EPS)).reshape(bsz, t, RWKV_WIDTH) * p["rwkv_ln_g"][l] + p["rwkv_ln_b"][l]
    bonus = jnp.sum(heads(r) * heads(k) * p["rwkv_r_k"][l], axis=-1, keepdims=True) * heads(v)
    o = o + bonus.reshape(bsz, t, RWKV_WIDTH)
    g = jax.nn.sigmoid(gl) @ p["rwkv_g2"][l]
    return (o * g).astype(stream.dtype), jnp.stack(finals, axis=1)


def _ssd_chunked(x, dt, a, bm, cm, s0):
    bsz, t, h, pd = x.shape
    g, n = bm.shape[2], bm.shape[3]
    e = h // g
    nc, lc = t // SSD_CHUNK, SSD_CHUNK
    xc = (x * dt[..., None]).reshape(bsz, nc, lc, g, e, pd)
    da = jnp.moveaxis((dt * a).reshape(bsz, nc, lc, g, e), 2, -1)
    cum = jnp.cumsum(da, axis=-1)
    bc = bm.reshape(bsz, nc, lc, g, n)
    cc = cm.reshape(bsz, nc, lc, g, n)
    seg = cum[..., :, None] - cum[..., None, :]
    lower = jnp.tril(jnp.ones((lc, lc), dtype=bool))
    lmat = jnp.exp(jnp.where(lower, seg, -jnp.inf))
    y_diag = jnp.einsum("bclgn,bcsgn,bcgels,bcsgep->bclgep", cc, bc, lmat, xc)
    decay_states = jnp.exp(cum[..., -1:] - cum)
    states = jnp.einsum("bclgn,bcgel,bclgep->bcgepn", bc, decay_states, xc)
    chunk_decay = jnp.exp(cum[..., -1])

    def step(s, inp):
        st, dec = inp
        return s * dec[..., None, None] + st, s

    s_fin, s_in = lax.scan(step, s0.reshape(bsz, g, e, pd, n),
                           (jnp.moveaxis(states, 1, 0), jnp.moveaxis(chunk_decay, 1, 0)))
    s_in = jnp.moveaxis(s_in, 0, 1)
    y_off = jnp.einsum("bclgn,bcgel,bcgepn->bclgep", cc, jnp.exp(cum), s_in)
    return (y_diag + y_off).reshape(bsz, t, h, pd), s_fin.reshape(bsz, h, pd, n)


def _ssd_mixer(stream, s0, p, l):
    bsz, t, _ = stream.shape
    f32 = jnp.float32
    z, xbc, dt_raw = _split(stream, [SSD_WIDTH, SSD_XBC, SSD_HEADS])
    xbc = jax.nn.silu(_depthwise_conv(xbc, p["ssd_conv_w"][l], p["ssd_conv_b"][l])).astype(f32)
    xs, bm, cm = _split(xbc, [SSD_WIDTH, SSD_GROUPS * SSD_STATE, SSD_GROUPS * SSD_STATE])
    x = xs.reshape(bsz, t, SSD_HEADS, SSD_HEAD)
    bm = bm.reshape(bsz, t, SSD_GROUPS, SSD_STATE)
    cm = cm.reshape(bsz, t, SSD_GROUPS, SSD_STATE)
    s0 = s0.astype(f32)
    y = p["ssd_d"][l][:, None] * x
    finals = []
    for d in range(N_DIR):
        dt = jax.nn.softplus(dt_raw.astype(f32) + p["ssd_dt_bias"][l, d])
        a = -jnp.exp(p["ssd_a_log"][l, d].astype(f32))
        if d == 0:
            y_d, s_d = _ssd_chunked(x, dt, a, bm, cm, s0[:, d])
        else:
            y_d, s_d = _ssd_chunked(jnp.flip(x, 1), jnp.flip(dt, 1), a, jnp.flip(bm, 1), jnp.flip(cm, 1), s0[:, d])
            y_d = jnp.flip(y_d, 1)
        y = y + y_d
        finals.append(s_d)
    y = y.reshape(bsz, t, SSD_WIDTH) * jax.nn.silu(z.astype(f32))
    return _rmsnorm(y, p["ssd_norm_g"][l]).astype(stream.dtype), jnp.stack(finals, axis=1)


def _complex_affine_combine(e1, e2):
    a1r, a1i, b1r, b1i = e1
    a2r, a2i, b2r, b2i = e2
    return (a2r * a1r - a2i * a1i, a2r * a1i + a2i * a1r,
            a2r * b1r - a2i * b1i + b2r, a2r * b1i + a2i * b1r + b2i)


def _s5_mixer(u, s0_re, s0_im, p, l):
    bsz, t, _ = u.shape
    f32 = jnp.float32
    uf = u.astype(f32).reshape(bsz, t, S5_GROUPS, S5_GROUP)
    s0_re, s0_im = s0_re.astype(f32), s0_im.astype(f32)
    y = p["s5_d"][l].reshape(S5_GROUPS, S5_GROUP) * uf
    fin_re, fin_im = [], []
    for d in range(N_DIR):
        lam_re, lam_im = p["s5_lambda_re"][l, d], p["s5_lambda_im"][l, d]
        delta = jnp.exp(p["s5_log_dt"][l, d])[:, None]
        mag = jnp.exp(lam_re * delta)
        lb_re, lb_im = mag * jnp.cos(lam_im * delta), mag * jnp.sin(lam_im * delta)
        den = lam_re * lam_re + lam_im * lam_im
        q_re = ((lb_re - 1.0) * lam_re + lb_im * lam_im) / den
        q_im = (lb_im * lam_re - (lb_re - 1.0) * lam_im) / den
        b_re, b_im = p["s5_b_re"][l, d], p["s5_b_im"][l, d]
        bb_re = q_re[..., None] * b_re - q_im[..., None] * b_im
        bb_im = q_re[..., None] * b_im + q_im[..., None] * b_re
        bu_re = jnp.einsum("gpc,btgc->btgp", bb_re, uf)
        bu_im = jnp.einsum("gpc,btgc->btgp", bb_im, uf)
        first, last = (0, t - 1) if d == 0 else (t - 1, 0)
        s_re0, s_im0 = s0_re[:, d], s0_im[:, d]
        bu_re = bu_re.at[:, first].add(lb_re * s_re0 - lb_im * s_im0)
        bu_im = bu_im.at[:, first].add(lb_re * s_im0 + lb_im * s_re0)
        a_re = jnp.broadcast_to(lb_re[None, None], (1, t, S5_GROUPS, S5_STATE))
        a_im = jnp.broadcast_to(lb_im[None, None], (1, t, S5_GROUPS, S5_STATE))
        _, _, s_re, s_im = lax.associative_scan(_complex_affine_combine, (a_re, a_im, bu_re, bu_im),
                                                reverse=(d == 1), axis=1)
        y = y + (jnp.einsum("gcp,btgp->btgc", p["s5_c_re"][l, d], s_re)
                 - jnp.einsum("gcp,btgp->btgc", p["s5_c_im"][l, d], s_im))
        fin_re.append(s_re[:, last])
        fin_im.append(s_im[:, last])
    y = jax.nn.gelu(y.reshape(bsz, t, S5_WIDTH))
    return y.astype(u.dtype), jnp.stack(fin_re, axis=1), jnp.stack(fin_im, axis=1)


def _mixing(h, grid, s_rwkv, s_ssd, s_re, s_im, p, l):
    z = h @ p["w_in"][l]
    z_a, z_b, z_c, z_g = _split(z, [RWKV_COLS, SSD_COLS, S5_WIDTH, 3 * D_MODEL])
    y_a, st_a = _rwkv7_mixer(z_a, grid, s_rwkv, p, l)
    y_b, st_b = _ssd_mixer(z_b, s_ssd, p, l)
    y_c, st_re, st_im = _s5_mixer(z_c, s_re, s_im, p, l)
    g_a, g_b, g_c = jnp.split(jax.nn.sigmoid(z_g), 3, axis=-1)
    glu_val, glu_gate = jnp.split(y_c @ p["w_proj_c"][l], 2, axis=-1)
    merged = (g_a * (y_a @ p["w_proj_a"][l]) + g_b * (y_b @ p["w_proj_b"][l])
              + g_c * (glu_val * jax.nn.sigmoid(glu_gate)))
    return merged @ p["w_out"][l], st_a, st_b, st_re, st_im


def _trunk(x, cond, grid, st_rwkv, st_ssd, st_re, st_im, p):
    new_a, new_b, new_re, new_im = [], [], [], []
    for l in range(DEPTH):
        mod = (jax.nn.silu(cond) @ p["w_mod"][l] + p["b_mod"][l])[:, None, :]
        sh1, sc1, g1, sh2, sc2, g2, sh3, sc3, g3 = jnp.split(mod, N_MOD, axis=-1)
        h = _rmsnorm(x, p["norm_g"][l, 0]) * (1.0 + sc1) + sh1
        x = x + 0.5 * g1 * _swiglu(h, p["ffn_w_in"][l, 0], p["ffn_w_out"][l, 0])
        h = _rmsnorm(x, p["norm_g"][l, 1]) * (1.0 + sc2) + sh2
        m, sa, sb, sre, sim = _mixing(h, grid, st_rwkv[:, l], st_ssd[:, l], st_re[:, l], st_im[:, l], p, l)
        x = x + g2 * m
        h = _rmsnorm(x, p["norm_g"][l, 2]) * (1.0 + sc3) + sh3
        x = x + 0.5 * g3 * _swiglu(h, p["ffn_w_in"][l, 1], p["ffn_w_out"][l, 1])
        new_a.append(sa)
        new_b.append(sb)
        new_re.append(sre)
        new_im.append(sim)
    y = _rmsnorm(x, p["final_norm_g"])
    return y, jnp.stack(new_a, axis=1), jnp.stack(new_b, axis=1), jnp.stack(new_re, axis=1), jnp.stack(new_im, axis=1)


def setup_inputs(seed: int = 0) -> dict:
    key = jax.random.key(seed)
    ks = iter(jax.random.split(key, 64))

    def nrm(shape, scale):
        return scale * jax.random.normal(next(ks), shape, jnp.float32)

    def uni(shape, lo, hi):
        return jax.random.uniform(next(ks), shape, jnp.float32, lo, hi)

    ld = (DEPTH, N_DIR)
    dt_ssd = jnp.exp(uni(ld + (SSD_HEADS,), math.log(1e-3), math.log(1e-1)))
    lam_shape = ld + (S5_GROUPS, S5_STATE)
    return {
        "x_prompt": nrm((BATCH, SEQ, D_MODEL), 1.0),
        "x_sample": nrm((DEC_BATCH, DEC_SEQ, D_MODEL), 1.0),
        "state_rwkv": nrm((DEC_BATCH, DEPTH, N_DIR, RWKV_HEADS, RWKV_HEAD, RWKV_HEAD), 0.3),
        "state_ssd": nrm((DEC_BATCH, DEPTH, N_DIR, SSD_HEADS, SSD_HEAD, SSD_STATE), 0.1),
        "state_s5_re": nrm((DEC_BATCH, DEPTH, N_DIR, S5_GROUPS, S5_STATE), 0.3),
        "state_s5_im": nrm((DEC_BATCH, DEPTH, N_DIR, S5_GROUPS, S5_STATE), 0.3),
        "c": nrm((DEC_BATCH, D_MODEL), 1.0),
        "c_ctx": nrm((D_MODEL,), 1.0),
        "w_mod": nrm((DEPTH, D_MODEL, N_MOD * D_MODEL), 0.5 * D_MODEL ** -0.5),
        "b_mod": nrm((DEPTH, N_MOD * D_MODEL), 0.01),
        "norm_g": 1.0 + nrm((DEPTH, 3, D_MODEL), 0.02),
        "ffn_w_in": nrm((DEPTH, 2, D_MODEL, 2 * D_FF), D_MODEL ** -0.5),
        "ffn_w_out": nrm((DEPTH, 2, D_FF, D_MODEL), D_FF ** -0.5),
        "w_in": nrm((DEPTH, D_MODEL, D_IN), D_MODEL ** -0.5),
        "rwkv_mu": uni((DEPTH, RWKV_COLS), 0.0, 1.0),
        "rwkv_w0": uni(ld + (RWKV_WIDTH,), -6.0, -1.0),
        "rwkv_w2": nrm(ld + (RWKV_DECAY_RANK, RWKV_WIDTH), 0.1 * RWKV_DECAY_RANK ** -0.5),
        "rwkv_a0": nrm(ld + (RWKV_WIDTH,), 0.1),
        "rwkv_a2": nrm(ld + (RWKV_AICL_RANK, RWKV_WIDTH), 0.5 * RWKV_AICL_RANK ** -0.5),
        "rwkv_g2": nrm((DEPTH, RWKV_GATE_RANK, RWKV_WIDTH), RWKV_GATE_RANK ** -0.5),
        "rwkv_k_k": 0.85 + nrm((DEPTH, RWKV_WIDTH), 0.02),
        "rwkv_k_a": 1.0 + nrm((DEPTH, RWKV_WIDTH), 0.02),
        "rwkv_r_k": nrm((DEPTH, RWKV_HEADS, RWKV_HEAD), 0.1),
        "rwkv_ln_g": 1.0 + nrm((DEPTH, RWKV_WIDTH), 0.02),
        "rwkv_ln_b": nrm((DEPTH, RWKV_WIDTH), 0.01),
        "w_proj_a": nrm((DEPTH, RWKV_WIDTH, D_MODEL), RWKV_WIDTH ** -0.5),
        "ssd_conv_w": nrm((DEPTH, SSD_CONV, SSD_XBC), SSD_CONV ** -0.5),
        "ssd_conv_b": nrm((DEPTH, SSD_XBC), 0.01),
        "ssd_dt_bias": dt_ssd + jnp.log(-jnp.expm1(-dt_ssd)),
        "ssd_a_log": jnp.log(uni(ld + (SSD_HEADS,), 1.0, 16.0)),
        "ssd_d": 1.0 + nrm((DEPTH, SSD_HEADS), 0.1),
        "ssd_norm_g": 1.0 + nrm((DEPTH, SSD_WIDTH), 0.02),
        "w_proj_b": nrm((DEPTH, SSD_WIDTH, D_MODEL), SSD_WIDTH ** -0.5),
        "s5_lambda_re": -0.5 + nrm(lam_shape, 0.01),
        "s5_lambda_im": jnp.pi * jnp.arange(S5_STATE, dtype=jnp.float32) + nrm(lam_shape, 0.01),
        "s5_log_dt": uni(ld + (S5_GROUPS,), math.log(1e-3), math.log(1e-1)),
        "s5_b_re": nrm(ld + (S5_GROUPS, S5_STATE, S5_GROUP), (2 * S5_GROUP) ** -0.5),
        "s5_b_im": nrm(ld + (S5_GROUPS, S5_STATE, S5_GROUP), (2 * S5_GROUP) ** -0.5),
        "s5_c_re": nrm(ld + (S5_GROUPS, S5_GROUP, S5_STATE), (2 * S5_STATE) ** -0.5),
        "s5_c_im": nrm(ld + (S5_GROUPS, S5_GROUP, S5_STATE), (2 * S5_STATE) ** -0.5),
        "s5_d": nrm((DEPTH, S5_WIDTH), 1.0),
        "w_proj_c": nrm((DEPTH, S5_WIDTH, 2 * D_MODEL), S5_WIDTH ** -0.5),
        "w_out": nrm((DEPTH, D_MODEL, D_MODEL), D_MODEL ** -0.5),
        "final_norm_g": 1.0 + nrm((D_MODEL,), 0.02),
    }


def reference(x_prompt, x_sample, state_rwkv, state_ssd, state_s5_re, state_s5_im, c,
              c_ctx, w_mod, b_mod, norm_g, ffn_w_in, ffn_w_out, w_in,
              rwkv_mu, rwkv_w0, rwkv_w2, rwkv_a0, rwkv_a2, rwkv_g2, rwkv_k_k, rwkv_k_a, rwkv_r_k,
              rwkv_ln_g, rwkv_ln_b, w_proj_a,
              ssd_conv_w, ssd_conv_b, ssd_dt_bias, ssd_a_log, ssd_d, ssd_norm_g, w_proj_b,
              s5_lambda_re, s5_lambda_im, s5_log_dt, s5_b_re, s5_b_im, s5_c_re, s5_c_im, s5_d, w_proj_c,
              w_out, final_norm_g):
    p = {
        "w_mod": w_mod, "b_mod": b_mod, "norm_g": norm_g, "ffn_w_in": ffn_w_in, "ffn_w_out": ffn_w_out,
        "w_in": w_in, "rwkv_mu": rwkv_mu, "rwkv_w0": rwkv_w0, "rwkv_w2": rwkv_w2, "rwkv_a0": rwkv_a0,
        "rwkv_a2": rwkv_a2, "rwkv_g2": rwkv_g2, "rwkv_k_k": rwkv_k_k, "rwkv_k_a": rwkv_k_a,
        "rwkv_r_k": rwkv_r_k, "rwkv_ln_g": rwkv_ln_g, "rwkv_ln_b": rwkv_ln_b, "w_proj_a": w_proj_a,
        "ssd_conv_w": ssd_conv_w, "ssd_conv_b": ssd_conv_b, "ssd_dt_bias": ssd_dt_bias,
        "ssd_a_log": ssd_a_log, "ssd_d": ssd_d, "ssd_norm_g": ssd_norm_g, "w_proj_b": w_proj_b,
        "s5_lambda_re": s5_lambda_re, "s5_lambda_im": s5_lambda_im, "s5_log_dt": s5_log_dt,
        "s5_b_re": s5_b_re, "s5_b_im": s5_b_im, "s5_c_re": s5_c_re, "s5_c_im": s5_c_im, "s5_d": s5_d,
        "w_proj_c": w_proj_c, "w_out": w_out, "final_norm_g": final_norm_g,
    }
    bp = x_prompt.shape[0]
    f32 = jnp.float32
    y_prompt, new_rwkv, new_ssd, new_re, new_im = _trunk(
        x_prompt, c_ctx[None, :], False,
        jnp.zeros((bp, DEPTH, N_DIR, RWKV_HEADS, RWKV_HEAD, RWKV_HEAD), f32),
        jnp.zeros((bp, DEPTH, N_DIR, SSD_HEADS, SSD_HEAD, SSD_STATE), f32),
        jnp.zeros((bp, DEPTH, N_DIR, S5_GROUPS, S5_STATE), f32),
        jnp.zeros((bp, DEPTH, N_DIR, S5_GROUPS, S5_STATE), f32), p)
    y_sample, _, _, _, _ = _trunk(x_sample, c, True, state_rwkv, state_ssd, state_s5_re, state_s5_im, p)
    return (y_prompt, y_sample, new_rwkv, new_ssd, new_re, new_im)
```

```python
import functools
import math

import jax
import jax.numpy as jnp
import numpy as np
from jax import lax
from jax.experimental import pallas as pl
from jax.experimental.pallas import tpu as pltpu

F32 = jnp.float32
BF16 = jnp.bfloat16

LANES = 128
SUBLANES = 8
VMEM_LIMIT = 56 * 1024 * 1024

GRID_W = 64
SSD_CHUNK = 128
SSD_GROUPS = 2
N_DIR = 2
N_MOD = 9
EPS = 1e-6
RWKV_LN_EPS = 64e-5
RWKV_HEAD = 64
RWKV_SCAN_TB = 32
S5_TB = 64


def _cparams(sem):
    return pltpu.CompilerParams(dimension_semantics=sem, vmem_limit_bytes=VMEM_LIMIT)


def _pick(n, target, mult=LANES):
    best = None
    d = mult
    while d <= min(n, target):
        if n % d == 0:
            best = d
        d += mult
    return n if best is None else best


def _seq_rows(seq_len, nrows, row0, cap=2048):
    span = math.gcd(nrows, row0) if row0 else nrows
    assert span % seq_len == 0
    return _pick(span, max(cap, seq_len), seq_len)


def _round_up(n, m):
    return -(-n // m) * m


def _dot(a, b):
    return jnp.dot(a, b, preferred_element_type=F32)


def _split3(x):
    x1 = x.astype(BF16)
    r1 = x - x1.astype(F32)
    x2 = r1.astype(BF16)
    x3 = (r1 - x2.astype(F32)).astype(BF16)
    return x1, x2, x3


def _dot_exact_r(x, sel):
    return sum(_dot(p, sel) for p in _split3(x))


def _dot_exact_l(sel, x):
    return sum(_dot(sel, p) for p in _split3(x))


def _softplus(x):
    return jnp.maximum(x, 0.0) + jnp.log1p(jnp.exp(-jnp.abs(x)))


def _sigmoid(x):
    return 1.0 / (1.0 + jnp.exp(-x))


def _silu(x):
    return x * _sigmoid(x)


def _mod_kernel(c_ref, w_ref, b_ref, o_ref):
    c = c_ref[...]
    a = _silu(c).astype(BF16)
    o_ref[...] = _dot(a, w_ref[...].astype(BF16)) + b_ref[...]


def _mod_call(cond8, w, b):
    d, n = w.shape
    tn = _pick(n, 1024)
    return pl.pallas_call(
        _mod_kernel,
        out_shape=jax.ShapeDtypeStruct((cond8.shape[0], n), F32),
        grid=(n // tn,),
        in_specs=[pl.BlockSpec((cond8.shape[0], d), lambda j: (0, 0)),
                  pl.BlockSpec((d, tn), lambda j: (0, j)),
                  pl.BlockSpec((1, tn), lambda j: (0, j))],
        out_specs=pl.BlockSpec((cond8.shape[0], tn), lambda j: (0, j)),
        compiler_params=_cparams(("parallel",)),
        name="mod_proj",
    )(cond8, w, b.reshape(1, n))


def _norm_mod_kernel(x_ref, g_ref, sh_ref, sc_ref, o_ref):
    x = x_ref[...]
    y = x * lax.rsqrt(jnp.mean(x * x, axis=-1, keepdims=True) + EPS) * g_ref[...]
    o_ref[...] = (y * (1.0 + sc_ref[0]) + sh_ref[0]).astype(o_ref.dtype)


def _norm_mod_call(x, g, sh, sc, tm):
    m, d = x.shape
    return pl.pallas_call(
        _norm_mod_kernel,
        out_shape=jax.ShapeDtypeStruct((m, d), BF16),
        grid=(m // tm,),
        in_specs=[pl.BlockSpec((tm, d), lambda i: (i, 0)),
                  pl.BlockSpec((1, d), lambda i: (0, 0)),
                  pl.BlockSpec((1, 1, d), lambda i: (i, 0, 0)),
                  pl.BlockSpec((1, 1, d), lambda i: (i, 0, 0))],
        out_specs=pl.BlockSpec((tm, d), lambda i: (i, 0)),
        compiler_params=_cparams(("parallel",)),
        name="norm_mod",
    )(x, g.reshape(1, d), sh, sc)


def _final_norm_kernel(x_ref, g_ref, o_ref):
    x = x_ref[...]
    o_ref[...] = x * lax.rsqrt(jnp.mean(x * x, axis=-1, keepdims=True) + EPS) * g_ref[...]


def _final_norm_call(x, g, tm):
    m, d = x.shape
    return pl.pallas_call(
        _final_norm_kernel,
        out_shape=jax.ShapeDtypeStruct((m, d), F32),
        grid=(m // tm,),
        in_specs=[pl.BlockSpec((tm, d), lambda i: (i, 0)),
                  pl.BlockSpec((1, d), lambda i: (0, 0))],
        out_specs=pl.BlockSpec((tm, d), lambda i: (i, 0)),
        compiler_params=_cparams(("parallel",)),
        name="final_norm",
    )(x, g.reshape(1, d))


def _mm_kernel(a_ref, w_ref, o_ref):
    o_ref[...] = _dot(a_ref[...], w_ref[...]).astype(o_ref.dtype)


def _mm_call(a, w, tm, tn_target=1024, out_dtype=F32):
    m, k = a.shape
    n = w.shape[1]
    tn = _pick(n, tn_target)
    return pl.pallas_call(
        _mm_kernel,
        out_shape=jax.ShapeDtypeStruct((m, n), out_dtype),
        grid=(m // tm, n // tn),
        in_specs=[pl.BlockSpec((tm, k), lambda i, j: (i, 0)),
                  pl.BlockSpec((k, tn), lambda i, j: (0, j))],
        out_specs=pl.BlockSpec((tm, tn), lambda i, j: (i, j)),
        compiler_params=_cparams(("parallel", "parallel")),
        name="mm",
    )(a, w)


def _mm_swiglu_kernel(a_ref, w_ref, o_ref):
    a = a_ref[...]
    gate = _dot(a, w_ref[0])
    up = _dot(a, w_ref[1])
    o_ref[...] = (_silu(gate) * up).astype(o_ref.dtype)


def _mm_swiglu_call(a, w2, tm, tn_target=512):
    m, k = a.shape
    n = w2.shape[2]
    tn = _pick(n, tn_target)
    return pl.pallas_call(
        _mm_swiglu_kernel,
        out_shape=jax.ShapeDtypeStruct((m, n), BF16),
        grid=(m // tm, n // tn),
        in_specs=[pl.BlockSpec((tm, k), lambda i, j: (i, 0)),
                  pl.BlockSpec((2, k, tn), lambda i, j: (0, 0, j))],
        out_specs=pl.BlockSpec((tm, tn), lambda i, j: (i, j)),
        compiler_params=_cparams(("parallel", "parallel")),
        name="mm_swiglu",
    )(a, w2)


def _mm_res_kernel(a_ref, w_ref, x_ref, g_ref, o_ref, *, coef):
    o_ref[...] = x_ref[...] + (coef * g_ref[0]) * _dot(a_ref[...], w_ref[...])


def _mm_res_call(a, w, x, gate, coef, tm, tn_target=512):
    m, k = a.shape
    n = w.shape[1]
    tn = _pick(n, tn_target)
    return pl.pallas_call(
        functools.partial(_mm_res_kernel, coef=coef),
        out_shape=jax.ShapeDtypeStruct((m, n), F32),
        grid=(m // tm, n // tn),
        in_specs=[pl.BlockSpec((tm, k), lambda i, j: (i, 0)),
                  pl.BlockSpec((k, tn), lambda i, j: (0, j)),
                  pl.BlockSpec((tm, tn), lambda i, j: (i, j)),
                  pl.BlockSpec((1, 1, tn), lambda i, j: (i, 0, j))],
        out_specs=pl.BlockSpec((tm, tn), lambda i, j: (i, j)),
        compiler_params=_cparams(("parallel", "parallel")),
        name="mm_res",
    )(a, w, x, gate)


def _merge_kernel(ya_ref, yb_ref, yc_ref, ga_ref, gb_ref, gc_ref, wa_ref, wb_ref, wc_ref, o_ref):
    pa = _dot(ya_ref[...], wa_ref[...])
    pb = _dot(yb_ref[...], wb_ref[...])
    yc = yc_ref[...]
    val = _dot(yc, wc_ref[0])
    gate = _dot(yc, wc_ref[1])
    merged = (_sigmoid(ga_ref[...]) * pa + _sigmoid(gb_ref[...]) * pb
              + _sigmoid(gc_ref[...]) * (val * _sigmoid(gate)))
    o_ref[...] = merged.astype(o_ref.dtype)


def _merge_call(ya, yb, yc, zg, wa, wb, wc2, tm, tn_target=512):
    m, ka = ya.shape
    d = wa.shape[1]
    tn = _pick(d, tn_target)
    nj = d // tn
    return pl.pallas_call(
        _merge_kernel,
        out_shape=jax.ShapeDtypeStruct((m, d), BF16),
        grid=(m // tm, nj),
        in_specs=[pl.BlockSpec((tm, ka), lambda i, j: (i, 0)),
                  pl.BlockSpec((tm, yb.shape[1]), lambda i, j: (i, 0)),
                  pl.BlockSpec((tm, yc.shape[1]), lambda i, j: (i, 0)),
                  pl.BlockSpec((tm, tn), lambda i, j: (i, j)),
                  pl.BlockSpec((tm, tn), lambda i, j: (i, nj + j)),
                  pl.BlockSpec((tm, tn), lambda i, j: (i, 2 * nj + j)),
                  pl.BlockSpec((ka, tn), lambda i, j: (0, j)),
                  pl.BlockSpec((yb.shape[1], tn), lambda i, j: (0, j)),
                  pl.BlockSpec((2, yc.shape[1], tn), lambda i, j: (0, 0, j))],
        out_specs=pl.BlockSpec((tm, tn), lambda i, j: (i, j)),
        compiler_params=_cparams(("parallel", "parallel")),
        name="merge",
    )(ya, yb, yc, zg, zg, zg, wa, wb, wc2)


def _row_shift(x, off, t_idx, seq_len):
    rows = x.shape[0]
    rolled = pltpu.roll(x, (-off) % rows, axis=0)
    src = t_idx + off
    ok = jnp.logical_and(src >= 0, src < seq_len)
    return jnp.where(ok, rolled, 0.0)


def _centred_nb(x, t_idx, seq_len, grid_w):
    if grid_w is None:
        return 0.5 * (_row_shift(x, -1, t_idx, seq_len) + _row_shift(x, 1, t_idx, seq_len))
    col = t_idx % grid_w
    left = jnp.where(col >= 1, _row_shift(x, -1, t_idx, seq_len), 0.0)
    right = jnp.where(col < grid_w - 1, _row_shift(x, 1, t_idx, seq_len), 0.0)
    up = _row_shift(x, -grid_w, t_idx, seq_len)
    down = _row_shift(x, grid_w, t_idx, seq_len)
    return 0.25 * (up + down + left + right)


def _rwkv_prep_kernel(zr_ref, zk_ref, zv_ref, zl_ref,
                      mur_ref, muk_ref, muv_ref, mul_ref,
                      kk_w_ref, ka_w_ref, rk_w_ref, w0_ref, a0_ref,
                      w2_ref, a2_ref, g2_ref, sel_ref, selt_ref,
                      r_o, v_o, kk_o, w0_o, b0_o, kd0_o, w1_o, b1_o, kd1_o, bonus_o, g_o,
                      *, seq_len, grid_w, ranks):
    def shifted(ref, mu_ref):
        x = ref[...]
        t_idx = lax.broadcasted_iota(jnp.int32, x.shape, 0) % seq_len
        return x + mu_ref[...] * (_centred_nb(x, t_idx, seq_len, grid_w) - x)

    r = shifted(zr_ref, mur_ref)
    k = shifted(zk_ref, muk_ref)
    v = shifted(zv_ref, muv_ref)
    lo = shifted(zl_ref, mul_ref)
    rd, ra, rg = ranks
    wl = lo[:, :rd]
    al = lo[:, rd:rd + ra]
    gl = lo[:, rd + ra:rd + ra + rg]
    sel = sel_ref[...]
    selt = selt_ref[...]

    def head_sum(x):
        return _dot_exact_r(_dot_exact_r(x, sel), selt)

    kk = k * kk_w_ref[...]
    kk = kk * lax.rsqrt(head_sum(kk * kk) + 1e-12)
    r_o[...] = r
    v_o[...] = v
    kk_o[...] = kk
    tw = jnp.tanh(wl).astype(BF16)
    alb = al.astype(BF16)
    outs = ((w0_o, b0_o, kd0_o), (w1_o, b1_o, kd1_o))
    for d in range(N_DIR):
        w_log = -_softplus(-(w0_ref[d] + _dot(tw, w2_ref[d]))) - 0.5
        a_d = _sigmoid(a0_ref[d] + _dot(alb, a2_ref[d]))
        outs[d][0][...] = jnp.exp(-jnp.exp(w_log))
        outs[d][1][...] = kk * a_d
        outs[d][2][...] = k * (1.0 + (a_d - 1.0) * ka_w_ref[...])
    bonus_o[...] = head_sum(r * k * rk_w_ref[...]) * v
    g_o[...] = _dot(_sigmoid(gl).astype(BF16), g2_ref[...])


def _rwkv_prep_call(zr, zk, zv, zl, row0, nrows, seq_len, grid_w, pw):
    w = zr.shape[1]
    lw = zl.shape[1]
    cb = LANES
    rb = _seq_rows(seq_len, nrows, row0)
    assert w % cb == 0
    r0 = row0 // rb
    main = pl.BlockSpec((rb, cb), lambda i, j: (r0 + i, j))
    lspec = pl.BlockSpec((rb, lw), lambda i, j: (r0 + i, 0))
    colp = pl.BlockSpec((1, cb), lambda i, j: (0, j))
    dirp = pl.BlockSpec((N_DIR, 1, cb), lambda i, j: (0, 0, j))
    ospec = pl.BlockSpec((rb, cb), lambda i, j: (i, j))
    rd, ra, rg = pw["ranks"]
    nsel = pw["sel"].shape[1]
    return pl.pallas_call(
        functools.partial(_rwkv_prep_kernel, seq_len=seq_len, grid_w=grid_w, ranks=pw["ranks"]),
        out_shape=[jax.ShapeDtypeStruct((nrows, w), F32)] * 11,
        grid=(nrows // rb, w // cb),
        in_specs=[main, main, main, lspec,
                  colp, colp, colp, pl.BlockSpec((1, lw), lambda i, j: (0, 0)),
                  colp, colp, colp, dirp, dirp,
                  pl.BlockSpec((N_DIR, rd, cb), lambda i, j: (0, 0, j)),
                  pl.BlockSpec((N_DIR, ra, cb), lambda i, j: (0, 0, j)),
                  pl.BlockSpec((rg, cb), lambda i, j: (0, j)),
                  pl.BlockSpec((cb, nsel), lambda i, j: (0, 0)),
                  pl.BlockSpec((nsel, cb), lambda i, j: (0, 0))],
        out_specs=[ospec] * 11,
        compiler_params=_cparams(("parallel", "parallel")),
        name="rwkv_prep",
    )(zr, zk, zv, zl, pw["mu_r"], pw["mu_k"], pw["mu_v"], pw["mu_l"],
      pw["k_k"], pw["k_a"], pw["r_k"], pw["w0"], pw["a0"],
      pw["w2"], pw["a2"], pw["g2"], pw["sel"], pw["selt"])


def _rwkv_scan_kernel(kk_ref, w_ref, b_ref, kd_ref, r_ref, v_ref, s0_ref, o_ref, sf_ref, s_ref):
    tb = pl.program_id(1)
    n = s_ref.shape[0]
    nj = n // SUBLANES
    steps = kk_ref.shape[0]

    @pl.when(tb == 0)
    def _():
        s_ref[...] = s0_ref[...]

    def row(ref, t, k):
        return jnp.broadcast_to(ref[t, pl.ds(k, 1), :], (SUBLANES, LANES))

    def step(t, carry):
        acc = [None] * nj
        for k in range(n):
            kkb = row(kk_ref, t, k)
            for j in range(nj):
                p = s_ref[k, pl.ds(j * SUBLANES, SUBLANES), :] * kkb
                acc[j] = p if acc[j] is None else acc[j] + p
        vv = [v_ref[t, pl.ds(j * SUBLANES, SUBLANES), :] for j in range(nj)]
        out = [None] * nj
        for k in range(n):
            wb = row(w_ref, t, k)
            bb = row(b_ref, t, k)
            kdb = row(kd_ref, t, k)
            rb = row(r_ref, t, k)
            for j in range(nj):
                s_new = (s_ref[k, pl.ds(j * SUBLANES, SUBLANES), :] * wb
                         - acc[j] * bb + vv[j] * kdb)
                s_ref[k, pl.ds(j * SUBLANES, SUBLANES), :] = s_new
                q = s_new * rb
                out[j] = q if out[j] is None else out[j] + q
        for j in range(nj):
            o_ref[t, pl.ds(j * SUBLANES, SUBLANES), :] = out[j]
        return carry

    lax.fori_loop(0, steps, step, 0)

    @pl.when(tb == pl.num_programs(1) - 1)
    def _():
        sf_ref[...] = s_ref[...]


def _rwkv_scan_call(kk, w, b, kd, r, v, s0):
    t, n, nch = kk.shape
    tb = min(RWKV_SCAN_TB, t)
    assert t % tb == 0 and nch % LANES == 0
    op = pl.BlockSpec((tb, n, LANES), lambda g, i: (i, 0, g))
    st = pl.BlockSpec((n, n, LANES), lambda g, i: (0, 0, g))
    return pl.pallas_call(
        _rwkv_scan_kernel,
        out_shape=[jax.ShapeDtypeStruct((t, n, nch), F32),
                   jax.ShapeDtypeStruct((n, n, nch), F32)],
        grid=(nch // LANES, t // tb),
        in_specs=[op] * 6 + [st],
        out_specs=[op, st],
        scratch_shapes=[pltpu.VMEM((n, n, LANES), F32)],
        compiler_params=_cparams(("parallel", "arbitrary")),
        name="rwkv_scan",
    )(kk, w, b, kd, r, v, s0)


def _rwkv_post_kernel(of_ref, ob_ref, bonus_ref, g_ref, lng_ref, lnb_ref, sel_ref, selt_ref, o_ref,
                      *, head):
    sel = sel_ref[...]
    selt = selt_ref[...]

    def head_sum(x):
        return _dot_exact_r(_dot_exact_r(x, sel), selt)

    o = of_ref[...] + ob_ref[...]
    mu = head_sum(o) * (1.0 / head)
    dlt = o - mu
    var = head_sum(dlt * dlt) * (1.0 / head)
    on = dlt * lax.rsqrt(var + RWKV_LN_EPS) * lng_ref[...] + lnb_ref[...]
    o_ref[...] = ((on + bonus_ref[...]) * g_ref[...]).astype(o_ref.dtype)


def _rwkv_post_call(of, ob, bonus, g, pw, tm):
    m, w = of.shape
    cb = LANES
    nsel = pw["sel"].shape[1]
    main = pl.BlockSpec((tm, cb), lambda i, j: (i, j))
    colp = pl.BlockSpec((1, cb), lambda i, j: (0, j))
    return pl.pallas_call(
        functools.partial(_rwkv_post_kernel, head=RWKV_HEAD),
        out_shape=jax.ShapeDtypeStruct((m, w), BF16),
        grid=(m // tm, w // cb),
        in_specs=[main, main, main, main, colp, colp,
                  pl.BlockSpec((cb, nsel), lambda i, j: (0, 0)),
                  pl.BlockSpec((nsel, cb), lambda i, j: (0, 0))],
        out_specs=main,
        compiler_params=_cparams(("parallel", "parallel")),
        name="rwkv_post",
    )(of, ob, bonus, g, pw["ln_g"], pw["ln_b"], pw["sel"], pw["selt"])


def _conv_silu_kernel(x_ref, w_ref, b_ref, o_ref, *, seq_len):
    x = x_ref[...]
    rows = x.shape[0]
    kw = w_ref.shape[0]
    t_idx = lax.broadcasted_iota(jnp.int32, x.shape, 0) % seq_len
    y = b_ref[...] + jnp.zeros_like(x)
    for j in range(kw):
        off = j - kw // 2
        xs = x if off == 0 else _row_shift(x, off, t_idx, seq_len)
        y = y + w_ref[pl.ds(j, 1), :] * xs
    o_ref[...] = _silu(y)


def _conv_silu_call(x, row0, nrows, seq_len, w, b):
    kw, c = w.shape
    cb = _pick(c, 256)
    rb = _seq_rows(seq_len, nrows, row0)
    r0 = row0 // rb
    return pl.pallas_call(
        functools.partial(_conv_silu_kernel, seq_len=seq_len),
        out_shape=jax.ShapeDtypeStruct((nrows, c), F32),
        grid=(nrows // rb, c // cb),
        in_specs=[pl.BlockSpec((rb, cb), lambda i, j: (r0 + i, j)),
                  pl.BlockSpec((kw, cb), lambda i, j: (0, j)),
                  pl.BlockSpec((1, cb), lambda i, j: (0, j))],
        out_specs=pl.BlockSpec((rb, cb), lambda i, j: (i, j)),
        compiler_params=_cparams(("parallel", "parallel")),
        name="ssd_conv",
    )(x, w, b.reshape(1, c))


def _ssd_scan_kernel(x_ref, b_ref, c_ref, dt_ref, dtt_ref, bias_ref, alog_ref, biast_ref, alogt_ref,
                     tri_ref, trit_ref, e_ref, s0_ref, y_ref, sf_ref, st_ref,
                     *, heads, hdim, nstate, groups, has_init):
    d = pl.program_id(0)
    c = pl.program_id(2)
    nc = pl.num_programs(2)
    hpg = heads // groups
    gw = hpg * hdim

    @pl.when(c == 0)
    def _():
        if has_init:
            for g in range(groups):
                st_ref[pl.ds(g * nstate, nstate), :] = s0_ref[0, 0, pl.ds(g * gw, gw), :].T
        else:
            st_ref[...] = jnp.zeros_like(st_ref)

    tri = tri_ref[0]
    trit = trit_ref[0]
    lch = tri.shape[0]
    e = e_ref[...]
    dtp = _softplus(dt_ref[...] + bias_ref[0])
    a = -jnp.exp(alog_ref[0])
    da = dtp * a
    cum = _dot_exact_l(tri, da)
    dat = _softplus(dtt_ref[...] + biast_ref[0]) * (-jnp.exp(alogt_ref[0]))
    cumt = _dot_exact_r(dat, trit)
    total = jnp.sum(da, axis=0, keepdims=True)
    tot8 = jnp.broadcast_to(total, (SUBLANES, LANES))
    dt_full = _dot_exact_r(dtp, e)
    din_full = _dot_exact_r(jnp.exp(cum), e)
    dst_full = _dot_exact_r(jnp.exp(total - cum), e)
    tot_full = _dot_exact_r(jnp.exp(tot8), e)[0:1, :]
    x = x_ref[...]
    xdt = x * dt_full
    xdec = (xdt * dst_full).astype(BF16)
    xdt_b = xdt.astype(BF16)
    bm = b_ref[...]
    cm = c_ref[...]
    visible = tri > 0
    for g in range(groups):
        bg = bm[:, g * nstate:(g + 1) * nstate]
        cg = cm[:, g * nstate:(g + 1) * nstate].astype(BF16)
        bgb = bg.astype(BF16)
        cb = lax.dot_general(cg, bgb, (((1,), (1,)), ((), ())), preferred_element_type=F32)
        st_g = st_ref[pl.ds(g * nstate, nstate), :]
        y_off = _dot(cg, st_g.astype(BF16)) * din_full[:, g * gw:(g + 1) * gw]
        for hh in range(hpg):
            h = g * hpg + hh
            seg = cum[:, h:h + 1] - cumt[h:h + 1, :]
            lmat = jnp.exp(jnp.where(visible, seg, -jnp.inf))
            gmat = (cb * lmat).astype(BF16)
            yd = _dot(gmat, xdt_b[:, h * hdim:(h + 1) * hdim])
            y_ref[0, :, pl.ds(h * hdim, hdim)] = yd + y_off[:, hh * hdim:(hh + 1) * hdim]
        upd = _dot(bg.T.astype(BF16), xdec[:, g * gw:(g + 1) * gw])
        st_ref[pl.ds(g * nstate, nstate), :] = st_g * tot_full[:, g * gw:(g + 1) * gw] + upd

    @pl.when(c == nc - 1)
    def _():
        for g in range(groups):
            sf_ref[0, 0, pl.ds(g * gw, gw), :] = st_ref[pl.ds(g * nstate, nstate), :].T


def _ssd_scan_call(xs, bm, cm, zdt, dtt, row0, bsz, seq_len, pw, s0):
    hp = xs.shape[1]
    gn = bm.shape[1]
    heads = pw["heads"]
    hdim = hp // heads
    nstate = gn // SSD_GROUPS
    lch = min(SSD_CHUNK, seq_len)
    nc = seq_len // lch
    assert seq_len % lch == 0 and row0 % lch == 0
    c0 = row0 // lch
    has_init = s0 is not None
    if s0 is None:
        s0 = jnp.zeros((1, 1, hp, nstate), F32)

    def cidx(d, b, c):
        return b * nc + c + d * (nc - 1 - 2 * c)

    row = lambda d, b, c: (cidx(d, b, c), 0)
    s0_map = (lambda d, b, c: (b, d, 0, 0)) if has_init else (lambda d, b, c: (0, 0, 0, 0))
    dirp = pl.BlockSpec((1, 1, LANES), lambda d, b, c: (d, 0, 0))
    dirt = pl.BlockSpec((1, heads, 1), lambda d, b, c: (d, 0, 0))
    return pl.pallas_call(
        functools.partial(_ssd_scan_kernel, heads=heads, hdim=hdim, nstate=nstate,
                          groups=SSD_GROUPS, has_init=has_init),
        out_shape=[jax.ShapeDtypeStruct((N_DIR, bsz * seq_len, hp), F32),
                   jax.ShapeDtypeStruct((bsz, N_DIR, hp, nstate), F32)],
        grid=(N_DIR, bsz, nc),
        in_specs=[pl.BlockSpec((lch, hp), row),
                  pl.BlockSpec((lch, gn), row),
                  pl.BlockSpec((lch, gn), row),
                  pl.BlockSpec((lch, LANES), lambda d, b, c: (c0 + cidx(d, b, c), 0)),
                  pl.BlockSpec((heads, lch), lambda d, b, c: (0, c0 + cidx(d, b, c))),
                  dirp, dirp, dirt, dirt,
                  pl.BlockSpec((1, lch, lch), lambda d, b, c: (d, 0, 0)),
                  pl.BlockSpec((1, lch, lch), lambda d, b, c: (d, 0, 0)),
                  pl.BlockSpec((LANES, hp), lambda d, b, c: (0, 0)),
                  pl.BlockSpec((1, 1, hp, nstate), s0_map)],
        out_specs=[pl.BlockSpec((1, lch, hp), lambda d, b, c: (d, cidx(d, b, c), 0)),
                   pl.BlockSpec((1, 1, hp, nstate), lambda d, b, c: (b, d, 0, 0))],
        scratch_shapes=[pltpu.VMEM((gn, hp // SSD_GROUPS), F32)],
        compiler_params=_cparams(("arbitrary", "arbitrary", "arbitrary")),
        name="ssd_scan",
    )(xs, bm, cm, zdt, dtt, pw["bias"], pw["alog"], pw["bias_t"], pw["alog_t"],
      pw["tri"][:, :lch, :lch], pw["trit"][:, :lch, :lch], pw["expand"], s0)


def _ssd_post_kernel(x_ref, yf_ref, yb_ref, z_ref, d_ref, g_ref, o_ref):
    y = (d_ref[...] * x_ref[...] + yf_ref[0] + yb_ref[0]) * _silu(z_ref[...])
    y = y * lax.rsqrt(jnp.mean(y * y, axis=-1, keepdims=True) + EPS) * g_ref[...]
    o_ref[...] = y.astype(o_ref.dtype)


def _ssd_post_call(xs, ydir, zz, row0, d_full, g, tm):
    n, hp = xs.shape
    r0 = row0 // tm
    return pl.pallas_call(
        _ssd_post_kernel,
        out_shape=jax.ShapeDtypeStruct((n, hp), BF16),
        grid=(n // tm,),
        in_specs=[pl.BlockSpec((tm, hp), lambda i: (i, 0)),
                  pl.BlockSpec((1, tm, hp), lambda i: (0, i, 0)),
                  pl.BlockSpec((1, tm, hp), lambda i: (1, i, 0)),
                  pl.BlockSpec((tm, hp), lambda i: (r0 + i, 0)),
                  pl.BlockSpec((1, hp), lambda i: (0, 0)),
                  pl.BlockSpec((1, hp), lambda i: (0, 0))],
        out_specs=pl.BlockSpec((tm, hp), lambda i: (i, 0)),
        compiler_params=_cparams(("parallel",)),
        name="ssd_post",
    )(xs, ydir, ydir, zz, d_full, g)


def _s5_kernel(u_ref, bmat_ref, cmat_ref, lam_ref, s0_ref, y_ref, sf_ref, st_ref, buf_ref,
               *, slab_in, slab_state):
    d = pl.program_id(0)
    tb = pl.program_id(2)
    ntb = pl.num_programs(2)
    steps = u_ref.shape[0]
    nb = u_ref.shape[1]
    nslab = u_ref.shape[2] // slab_in

    @pl.when(tb == 0)
    def _():
        st_ref[...] = s0_ref[0]

    for s in range(nslab):
        u = u_ref[:, :, pl.ds(s * slab_in, slab_in)].reshape(steps * nb, slab_in).astype(BF16)
        buf_ref[...] = _dot(u, bmat_ref[0, s])
        lr = jnp.broadcast_to(lam_ref[0, 0, :, pl.ds(s * slab_state, slab_state)], (nb, slab_state))
        li = jnp.broadcast_to(lam_ref[0, 1, :, pl.ds(s * slab_state, slab_state)], (nb, slab_state))

        def step(i, carry):
            s_re, s_im = carry
            te = i + d * (steps - 1 - 2 * i)
            r0 = pl.multiple_of(te * nb, nb)
            n_re = lr * s_re - li * s_im + buf_ref[pl.ds(r0, nb), pl.ds(0, slab_state)]
            n_im = lr * s_im + li * s_re + buf_ref[pl.ds(r0, nb), pl.ds(slab_state, slab_state)]
            buf_ref[pl.ds(r0, nb), pl.ds(0, slab_state)] = n_re
            buf_ref[pl.ds(r0, nb), pl.ds(slab_state, slab_state)] = n_im
            return n_re, n_im

        s_re, s_im = lax.fori_loop(
            0, steps, step,
            (st_ref[0, :, pl.ds(s * slab_state, slab_state)], st_ref[1, :, pl.ds(s * slab_state, slab_state)]))
        st_ref[0, :, pl.ds(s * slab_state, slab_state)] = s_re
        st_ref[1, :, pl.ds(s * slab_state, slab_state)] = s_im
        y = _dot(buf_ref[...].astype(BF16), cmat_ref[0, s])
        y_ref[0, :, :, pl.ds(s * slab_in, slab_in)] = y.reshape(steps, nb, slab_in)

    @pl.when(tb == ntb - 1)
    def _():
        sf_ref[0] = st_ref[...]


def _s5_call(u_tm, pw, s0):
    t, bp, w = u_tm.shape
    nb = SUBLANES
    tb = min(S5_TB, t)
    assert t % tb == 0 and bp % nb == 0
    ntb = t // tb
    gp = pw["lam"].shape[-1]
    slab_in = pw["slab_in"]
    slab_state = pw["slab_state"]
    nslab = w // slab_in

    def tidx(d, i):
        return i + d * (ntb - 1 - 2 * i)

    return pl.pallas_call(
        functools.partial(_s5_kernel, slab_in=slab_in, slab_state=slab_state),
        out_shape=[jax.ShapeDtypeStruct((N_DIR, t, bp, w), F32),
                   jax.ShapeDtypeStruct((N_DIR, 2, bp, gp), F32)],
        grid=(N_DIR, bp // nb, ntb),
        in_specs=[pl.BlockSpec((tb, nb, w), lambda d, b, i: (tidx(d, i), b, 0)),
                  pl.BlockSpec((1, nslab, slab_in, 2 * slab_state), lambda d, b, i: (d, 0, 0, 0)),
                  pl.BlockSpec((1, nslab, 2 * slab_state, slab_in), lambda d, b, i: (d, 0, 0, 0)),
                  pl.BlockSpec((1, 2, 1, gp), lambda d, b, i: (d, 0, 0, 0)),
                  pl.BlockSpec((1, 2, nb, gp), lambda d, b, i: (d, 0, b, 0))],
        out_specs=[pl.BlockSpec((1, tb, nb, w), lambda d, b, i: (d, tidx(d, i), b, 0)),
                   pl.BlockSpec((1, 2, nb, gp), lambda d, b, i: (d, 0, b, 0))],
        scratch_shapes=[pltpu.VMEM((2, nb, gp), F32),
                        pltpu.VMEM((tb * nb, 2 * slab_state), F32)],
        compiler_params=_cparams(("arbitrary", "arbitrary", "arbitrary")),
        name="s5_scan",
    )(u_tm, pw["bmat"], pw["cmat"], pw["lam"], s0)


def _s5_post_kernel(u_ref, yf_ref, yb_ref, d_ref, o_ref):
    y = d_ref[...] * u_ref[...] + yf_ref[...] + yb_ref[...]
    o_ref[...] = jax.nn.gelu(y).astype(o_ref.dtype)


def _s5_post_call(u, yf, yb, d_full, tm):
    m, w = u.shape
    spec = pl.BlockSpec((tm, w), lambda i: (i, 0))
    return pl.pallas_call(
        _s5_post_kernel,
        out_shape=jax.ShapeDtypeStruct((m, w), BF16),
        grid=(m // tm,),
        in_specs=[spec, spec, spec, pl.BlockSpec((1, w), lambda i: (0, 0))],
        out_specs=spec,
        compiler_params=_cparams(("parallel",)),
        name="s5_post",
    )(u, yf, yb, d_full)


def _head_selectors(cb, head):
    nsel = LANES
    col = np.arange(cb)[:, None] // head
    sel = (col == np.arange(nsel)[None, :]).astype(np.float32)
    return jnp.asarray(sel, BF16), jnp.asarray(sel.T, BF16)


def _s5_weights(p, l, slab_groups):
    g, pst = p["s5_lambda_re"].shape[2:]
    cg = p["s5_b_re"].shape[-1]
    nslab = g // slab_groups
    eye = jnp.eye(slab_groups, dtype=F32)
    bmats, cmats, lams = [], [], []
    for d in range(N_DIR):
        lam_re, lam_im = p["s5_lambda_re"][l, d], p["s5_lambda_im"][l, d]
        delta = jnp.exp(p["s5_log_dt"][l, d])[:, None]
        mag = jnp.exp(lam_re * delta)
        lb_re, lb_im = mag * jnp.cos(lam_im * delta), mag * jnp.sin(lam_im * delta)
        den = lam_re * lam_re + lam_im * lam_im
        q_re = ((lb_re - 1.0) * lam_re + lb_im * lam_im) / den
        q_im = (lb_im * lam_re - (lb_re - 1.0) * lam_im) / den
        b_re, b_im = p["s5_b_re"][l, d], p["s5_b_im"][l, d]
        bb_re = q_re[..., None] * b_re - q_im[..., None] * b_im
        bb_im = q_re[..., None] * b_im + q_im[..., None] * b_re

        def in_blocks(bb):
            x = bb.reshape(nslab, slab_groups, pst, cg)
            return jnp.einsum("sgpc,gh->sgchp", x, eye).reshape(nslab, slab_groups * cg, slab_groups * pst)

        def out_blocks(cc):
            x = cc.reshape(nslab, slab_groups, cg, pst)
            return jnp.einsum("sgcp,gh->sgphc", x, eye).reshape(nslab, slab_groups * pst, slab_groups * cg)

        bmats.append(jnp.concatenate([in_blocks(bb_re), in_blocks(bb_im)], axis=-1))
        cmats.append(jnp.concatenate([out_blocks(p["s5_c_re"][l, d]), -out_blocks(p["s5_c_im"][l, d])], axis=-2))
        lams.append(jnp.stack([lb_re.reshape(1, g * pst), lb_im.reshape(1, g * pst)]))
    return dict(bmat=jnp.stack(bmats).astype(BF16), cmat=jnp.stack(cmats).astype(BF16),
                lam=jnp.stack(lams), slab_in=slab_groups * cg, slab_state=slab_groups * pst)


def _pad_cols(w, n):
    return jnp.pad(w, ((0, 0), (0, n - w.shape[1])))


def kernel(x_prompt, x_sample, state_rwkv, state_ssd, state_s5_re, state_s5_im, c, c_ctx, w_mod, b_mod, norm_g, ffn_w_in, ffn_w_out, w_in, rwkv_mu, rwkv_w0, rwkv_w2, rwkv_a0, rwkv_a2, rwkv_g2, rwkv_k_k, rwkv_k_a, rwkv_r_k, rwkv_ln_g, rwkv_ln_b, w_proj_a, ssd_conv_w, ssd_conv_b, ssd_dt_bias, ssd_a_log, ssd_d, ssd_norm_g, w_proj_b, s5_lambda_re, s5_lambda_im, s5_log_dt, s5_b_re, s5_b_im, s5_c_re, s5_c_im, s5_d, w_proj_c, w_out, final_norm_g):
    p = dict(s5_lambda_re=s5_lambda_re, s5_lambda_im=s5_lambda_im, s5_log_dt=s5_log_dt,
             s5_b_re=s5_b_re, s5_b_im=s5_b_im, s5_c_re=s5_c_re, s5_c_im=s5_c_im)
    bp, tp, dm = x_prompt.shape
    bs, ts, _ = x_sample.shape
    n_p, n_s = bp * tp, bs * ts
    m = n_p + n_s
    depth = w_mod.shape[0]
    d_ff = ffn_w_out.shape[2]
    ffp = _round_up(d_ff, 512)
    rw = rwkv_k_k.shape[1]
    rh = rw // RWKV_HEAD
    rd, ra, rg = rwkv_w2.shape[2], rwkv_a2.shape[2], rwkv_g2.shape[1]
    lora = rd + ra + rg
    sh = ssd_d.shape[1]
    sw = ssd_norm_g.shape[1]
    xbc_w = ssd_conv_w.shape[2]
    gn = (xbc_w - sw) // 2
    cw = s5_d.shape[1]
    s5_g, s5_p = s5_lambda_re.shape[2:]
    s5_cg = cw // s5_g
    slab_groups = max(1, min(s5_g, LANES // s5_cg))

    tm = _pick(math.gcd(n_p, ts), 1024, SUBLANES)
    n_tiles = m // tm
    tile_cond = np.array([0 if i * tm < n_p else 1 + (i * tm - n_p) // ts for i in range(n_tiles)])

    x = jnp.concatenate([x_prompt.reshape(n_p, dm), x_sample.reshape(n_s, dm)], axis=0)
    ncond = 1 + bs
    cond = jnp.concatenate([c_ctx[None, :], c], axis=0)
    cond8 = jnp.pad(cond, ((0, _round_up(ncond, SUBLANES) - ncond), (0, 0)))

    groups = [dict(b=bp, t=tp, row0=0, n=n_p, grid_w=None),
              dict(b=bs, t=ts, row0=n_p, n=n_s, grid_w=GRID_W)]

    sel, selt = _head_selectors(LANES, RWKV_HEAD)
    heads_per_blk = LANES // RWKV_HEAD
    tri_f = np.tril(np.ones((SSD_CHUNK, SSD_CHUNK), np.float32))
    tri = jnp.asarray(np.stack([tri_f, tri_f.T]), BF16)
    trit = jnp.asarray(np.stack([tri_f.T, tri_f]), BF16)
    expand = jnp.asarray((np.arange(LANES)[:, None] == (np.arange(sw)[None, :] // (sw // sh))).astype(np.float32), BF16)

    new_a, new_b, new_re, new_im = [], [], [], []
    for l in range(depth):
        mod = _mod_call(cond8, w_mod[l], b_mod[l])
        mods = mod.reshape(-1, N_MOD, dm)[tile_cond]
        sh1, sc1, g1, sh2, sc2, g2, sh3, sc3, g3 = [mods[:, i:i + 1, :] for i in range(N_MOD)]

        def ffn(x, idx, shv, scv, gv):
            wi = ffn_w_in[l, idx]
            w2 = jnp.stack([_pad_cols(wi[:, :d_ff], ffp), _pad_cols(wi[:, d_ff:], ffp)]).astype(BF16)
            wo = jnp.pad(ffn_w_out[l, idx], ((0, ffp - d_ff), (0, 0))).astype(BF16)
            h = _norm_mod_call(x, norm_g[l, idx * 2], shv, scv, tm)
            a = _mm_swiglu_call(h, w2, tm)
            return _mm_res_call(a, wo, x, gv, 0.5, tm)

        x = ffn(x, 0, sh1, sc1, g1)

        h = _norm_mod_call(x, norm_g[l, 1], sh2, sc2, tm)
        wl_in = w_in[l]
        offs = np.cumsum([0, rw, rw, rw, lora, sw, sw, gn, gn, sh, cw, 3 * dm])
        segs = [wl_in[:, offs[i]:offs[i + 1]].astype(BF16) for i in range(11)]
        segs[8] = _pad_cols(segs[8], LANES)
        zr, zk, zv, zl, zz, zxs, zbm, zcm, zdt, zc, zg = [_mm_call(h, wseg, tm) for wseg in segs]
        dtt = zdt[:, :sh].T

        mu = rwkv_mu[l]
        pw_r = dict(ranks=(rd, ra, rg), sel=sel, selt=selt,
                    mu_r=mu[None, :rw], mu_k=mu[None, rw:2 * rw], mu_v=mu[None, 2 * rw:3 * rw],
                    mu_l=mu[None, 3 * rw:],
                    k_k=rwkv_k_k[l][None], k_a=rwkv_k_a[l][None], r_k=rwkv_r_k[l].reshape(1, rw),
                    w0=rwkv_w0[l][:, None, :], a0=rwkv_a0[l][:, None, :],
                    w2=rwkv_w2[l].astype(BF16), a2=rwkv_a2[l].astype(BF16), g2=rwkv_g2[l].astype(BF16),
                    ln_g=rwkv_ln_g[l][None], ln_b=rwkv_ln_b[l][None])
        of_parts, ob_parts, bonus_parts, g_parts, fin_a = [], [], [], [], None
        for gi, gr in enumerate(groups):
            b_, t_ = gr["b"], gr["t"]
            r_, v_, kk_, w0_, b0_, kd0_, w1_, b1_, kd1_, bonus_, g_ = _rwkv_prep_call(
                zr, zk, zv, zl, gr["row0"], gr["n"], t_, gr["grid_w"], pw_r)
            nchain = N_DIR * b_ * rh
            nch_pad = _round_up(nchain, LANES)

            def to_chain(fwd, bwd):
                def tr(a):
                    return a.reshape(b_, t_, rh, RWKV_HEAD).transpose(1, 3, 0, 2).reshape(t_, RWKV_HEAD, b_ * rh)
                a = jnp.concatenate([tr(fwd), jnp.flip(tr(bwd), axis=0)], axis=-1)
                return jnp.pad(a, ((0, 0), (0, 0), (0, nch_pad - nchain)))

            if gi == 0:
                s0c = jnp.zeros((RWKV_HEAD, RWKV_HEAD, nch_pad), F32)
            else:
                s0c = state_rwkv[:, l].transpose(4, 3, 1, 0, 2).reshape(RWKV_HEAD, RWKV_HEAD, nchain)
                s0c = jnp.pad(s0c, ((0, 0), (0, 0), (0, nch_pad - nchain)))
            o_c, sf_c = _rwkv_scan_call(to_chain(kk_, kk_), to_chain(w0_, w1_), to_chain(b0_, b1_),
                                        to_chain(kd0_, kd1_), to_chain(r_, r_), to_chain(v_, v_), s0c)
            o_c = o_c[:, :, :nchain].reshape(t_, RWKV_HEAD, N_DIR, b_, rh)
            o_f = o_c[:, :, 0].transpose(2, 0, 3, 1).reshape(gr["n"], rw)
            o_b = jnp.flip(o_c[:, :, 1], axis=0).transpose(2, 0, 3, 1).reshape(gr["n"], rw)
            of_parts.append(o_f)
            ob_parts.append(o_b)
            bonus_parts.append(bonus_)
            g_parts.append(g_)
            if gi == 0:
                fin_a = sf_c[:, :, :nchain].reshape(RWKV_HEAD, RWKV_HEAD, N_DIR, b_, rh).transpose(3, 2, 4, 1, 0)
        y_a = _rwkv_post_call(jnp.concatenate(of_parts), jnp.concatenate(ob_parts),
                              jnp.concatenate(bonus_parts), jnp.concatenate(g_parts), pw_r, tm)
        new_a.append(fin_a)

        cw_l, cb_l = ssd_conv_w[l], ssd_conv_b[l]
        pw_s = dict(heads=sh, tri=tri, trit=trit, expand=expand,
                    bias=jnp.pad(ssd_dt_bias[l], ((0, 0), (0, LANES - sh)))[:, None, :],
                    alog=jnp.pad(ssd_a_log[l], ((0, 0), (0, LANES - sh)))[:, None, :],
                    bias_t=ssd_dt_bias[l][:, :, None], alog_t=ssd_a_log[l][:, :, None])
        yb_parts, fin_b = [], None
        for gi, gr in enumerate(groups):
            xs_ = _conv_silu_call(zxs, gr["row0"], gr["n"], gr["t"], cw_l[:, :sw], cb_l[:sw])
            bm_ = _conv_silu_call(zbm, gr["row0"], gr["n"], gr["t"], cw_l[:, sw:sw + gn], cb_l[sw:sw + gn])
            cm_ = _conv_silu_call(zcm, gr["row0"], gr["n"], gr["t"], cw_l[:, sw + gn:], cb_l[sw + gn:])
            s0s = None if gi == 0 else state_ssd[:, l].reshape(gr["b"], N_DIR, sw, gn // SSD_GROUPS)
            ydir, sf_s = _ssd_scan_call(xs_, bm_, cm_, zdt, dtt, gr["row0"], gr["b"], gr["t"], pw_s, s0s)
            yb_parts.append(_ssd_post_call(xs_, ydir, zz, gr["row0"],
                                           jnp.repeat(ssd_d[l], sw // sh)[None, :], ssd_norm_g[l][None, :], tm))
            if gi == 0:
                fin_b = sf_s.reshape(gr["b"], N_DIR, sh, sw // sh, gn // SSD_GROUPS)
        y_b = jnp.concatenate(yb_parts)
        new_b.append(fin_b)

        pw_c = _s5_weights(p, l, slab_groups)
        yf_parts, ybw_parts, fin_re, fin_im = [], [], None, None
        for gi, gr in enumerate(groups):
            b_, t_ = gr["b"], gr["t"]
            bpad = _round_up(b_, SUBLANES)
            u_tm = zc[gr["row0"]:gr["row0"] + gr["n"]].reshape(b_, t_, cw).transpose(1, 0, 2)
            u_tm = jnp.pad(u_tm, ((0, 0), (0, bpad - b_), (0, 0)))
            if gi == 0:
                s0 = jnp.zeros((N_DIR, 2, bpad, s5_g * s5_p), F32)
            else:
                s0 = jnp.stack([state_s5_re[:, l], state_s5_im[:, l]])
                s0 = s0.reshape(2, b_, N_DIR, s5_g * s5_p).transpose(2, 0, 1, 3)
                s0 = jnp.pad(s0, ((0, 0), (0, 0), (0, bpad - b_), (0, 0)))
            y_tm, sf = _s5_call(u_tm, pw_c, s0)
            y_tok = y_tm[:, :, :b_].transpose(0, 2, 1, 3).reshape(N_DIR, gr["n"], cw)
            yf_parts.append(y_tok[0])
            ybw_parts.append(y_tok[1])
            if gi == 0:
                fin = sf[:, :, :b_].reshape(N_DIR, 2, b_, s5_g, s5_p).transpose(1, 2, 0, 3, 4)
                fin_re, fin_im = fin[0], fin[1]
        y_c = _s5_post_call(zc, jnp.concatenate(yf_parts), jnp.concatenate(ybw_parts), s5_d[l][None, :], tm)
        new_re.append(fin_re)
        new_im.append(fin_im)

        wc = w_proj_c[l]
        merged = _merge_call(y_a, y_b, y_c, zg, w_proj_a[l].astype(BF16), w_proj_b[l].astype(BF16),
                             jnp.stack([wc[:, :dm], wc[:, dm:]]).astype(BF16), tm)
        x = _mm_res_call(merged, w_out[l].astype(BF16), x, g2, 1.0, tm)

        x = ffn(x, 1, sh3, sc3, g3)

    y = _final_norm_call(x, final_norm_g, tm)
    return (y[:n_p].reshape(bp, tp, dm), y[n_p:].reshape(bs, ts, dm),
            jnp.stack(new_a, axis=1), jnp.stack(new_b, axis=1),
            jnp.stack(new_re, axis=1), jnp.stack(new_im, axis=1))
```

```python
import functools
import math

import jax
import jax.numpy as jnp
import numpy as np
from jax import lax
from jax.experimental import pallas as pl
from jax.experimental.pallas import tpu as pltpu

F32 = jnp.float32
BF16 = jnp.bfloat16

LANES = 128
SUBLANES = 8
VMEM_LIMIT = 56 * 1024 * 1024

GRID_W = 64
SSD_CHUNK = 128
SSD_GROUPS = 2
N_DIR = 2
N_MOD = 9
EPS = 1e-6
RWKV_LN_EPS = 64e-5
RWKV_HEAD = 64
RWKV_SCAN_TB = 32
S5_TB = 64


def _cparams(sem):
    return pltpu.CompilerParams(dimension_semantics=sem, vmem_limit_bytes=VMEM_LIMIT)


def _pick(n, target, mult=LANES):
    best = None
    d = mult
    while d <= min(n, target):
        if n % d == 0:
            best = d
        d += mult
    return n if best is None else best


def _seq_rows(seq_len, nrows, row0, cap=2048):
    span = math.gcd(nrows, row0) if row0 else nrows
    assert span % seq_len == 0
    return _pick(span, max(cap, seq_len), seq_len)


def _round_up(n, m):
    return -(-n // m) * m


def _dot(a, b):
    return jnp.dot(a, b, preferred_element_type=F32)


def _split3(x):
    x1 = x.astype(BF16)
    r1 = x - x1.astype(F32)
    x2 = r1.astype(BF16)
    x3 = (r1 - x2.astype(F32)).astype(BF16)
    return x1, x2, x3


def _dot_exact_r(x, sel):
    return sum(_dot(p, sel) for p in _split3(x))


def _dot_exact_l(sel, x):
    return sum(_dot(sel, p) for p in _split3(x))


def _softplus(x):
    return jnp.maximum(x, 0.0) + jnp.log1p(jnp.exp(-jnp.abs(x)))


def _sigmoid(x):
    return 1.0 / (1.0 + jnp.exp(-x))


def _silu(x):
    return x * _sigmoid(x)


def _mod_kernel(c_ref, w_ref, b_ref, o_ref):
    c = c_ref[...]
    a = _silu(c).astype(BF16)
    o_ref[...] = _dot(a, w_ref[...].astype(BF16)) + b_ref[...]


def _mod_call(cond8, w, b):
    d, n = w.shape
    tn = _pick(n, 1024)
    return pl.pallas_call(
        _mod_kernel,
        out_shape=jax.ShapeDtypeStruct((cond8.shape[0], n), F32),
        grid=(n // tn,),
        in_specs=[pl.BlockSpec((cond8.shape[0], d), lambda j: (0, 0)),
                  pl.BlockSpec((d, tn), lambda j: (0, j)),
                  pl.BlockSpec((1, tn), lambda j: (0, j))],
        out_specs=pl.BlockSpec((cond8.shape[0], tn), lambda j: (0, j)),
        compiler_params=_cparams(("parallel",)),
        name="mod_proj",
    )(cond8, w, b.reshape(1, n))


def _norm_mod_kernel(x_ref, g_ref, sh_ref, sc_ref, o_ref):
    x = x_ref[...]
    y = x * lax.rsqrt(jnp.mean(x * x, axis=-1, keepdims=True) + EPS) * g_ref[...]
    o_ref[...] = (y * (1.0 + sc_ref[0]) + sh_ref[0]).astype(o_ref.dtype)


def _norm_mod_call(x, g, sh, sc, tm):
    m, d = x.shape
    return pl.pallas_call(
        _norm_mod_kernel,
        out_shape=jax.ShapeDtypeStruct((m, d), BF16),
        grid=(m // tm,),
        in_specs=[pl.BlockSpec((tm, d), lambda i: (i, 0)),
                  pl.BlockSpec((1, d), lambda i: (0, 0)),
                  pl.BlockSpec((1, 1, d), lambda i: (i, 0, 0)),
                  pl.BlockSpec((1, 1, d), lambda i: (i, 0, 0))],
        out_specs=pl.BlockSpec((tm, d), lambda i: (i, 0)),
        compiler_params=_cparams(("parallel",)),
        name="norm_mod",
    )(x, g.reshape(1, d), sh, sc)


def _final_norm_kernel(x_ref, g_ref, o_ref):
    x = x_ref[...]
    o_ref[...] = x * lax.rsqrt(jnp.mean(x * x, axis=-1, keepdims=True) + EPS) * g_ref[...]


def _final_norm_call(x, g, tm):
    m, d = x.shape
    return pl.pallas_call(
        _final_norm_kernel,
        out_shape=jax.ShapeDtypeStruct((m, d), F32),
        grid=(m // tm,),
        in_specs=[pl.BlockSpec((tm, d), lambda i: (i, 0)),
                  pl.BlockSpec((1, d), lambda i: (0, 0))],
        out_specs=pl.BlockSpec((tm, d), lambda i: (i, 0)),
        compiler_params=_cparams(("parallel",)),
        name="final_norm",
    )(x, g.reshape(1, d))


def _mm_kernel(a_ref, w_ref, o_ref):
    o_ref[...] = _dot(a_ref[...], w_ref[...]).astype(o_ref.dtype)


def _mm_call(a, w, tm, tn_target=1024, out_dtype=F32):
    m, k = a.shape
    n = w.shape[1]
    tn = _pick(n, tn_target)
    return pl.pallas_call(
        _mm_kernel,
        out_shape=jax.ShapeDtypeStruct((m, n), out_dtype),
        grid=(m // tm, n // tn),
        in_specs=[pl.BlockSpec((tm, k), lambda i, j: (i, 0)),
                  pl.BlockSpec((k, tn), lambda i, j: (0, j))],
        out_specs=pl.BlockSpec((tm, tn), lambda i, j: (i, j)),
        compiler_params=_cparams(("parallel", "parallel")),
        name="mm",
    )(a, w)


def _mm_swiglu_kernel(a_ref, w_ref, o_ref):
    a = a_ref[...]
    gate = _dot(a, w_ref[0])
    up = _dot(a, w_ref[1])
    o_ref[...] = (_silu(gate) * up).astype(o_ref.dtype)


def _mm_swiglu_call(a, w2, tm, tn_target=512):
    m, k = a.shape
    n = w2.shape[2]
    tn = _pick(n, tn_target)
    return pl.pallas_call(
        _mm_swiglu_kernel,
        out_shape=jax.ShapeDtypeStruct((m, n), BF16),
        grid=(m // tm, n // tn),
        in_specs=[pl.BlockSpec((tm, k), lambda i, j: (i, 0)),
                  pl.BlockSpec((2, k, tn), lambda i, j: (0, 0, j))],
        out_specs=pl.BlockSpec((tm, tn), lambda i, j: (i, j)),
        compiler_params=_cparams(("parallel", "parallel")),
        name="mm_swiglu",
    )(a, w2)


def _mm_res_kernel(a_ref, w_ref, x_ref, g_ref, o_ref, *, coef):
    o_ref[...] = x_ref[...] + (coef * g_ref[0]) * _dot(a_ref[...], w_ref[...])


def _mm_res_call(a, w, x, gate, coef, tm, tn_target=512):
    m, k = a.shape
    n = w.shape[1]
    tn = _pick(n, tn_target)
    return pl.pallas_call(
        functools.partial(_mm_res_kernel, coef=coef),
        out_shape=jax.ShapeDtypeStruct((m, n), F32),
        grid=(m // tm, n // tn),
        in_specs=[pl.BlockSpec((tm, k), lambda i, j: (i, 0)),
                  pl.BlockSpec((k, tn), lambda i, j: (0, j)),
                  pl.BlockSpec((tm, tn), lambda i, j: (i, j)),
                  pl.BlockSpec((1, 1, tn), lambda i, j: (i, 0, j))],
        out_specs=pl.BlockSpec((tm, tn), lambda i, j: (i, j)),
        compiler_params=_cparams(("parallel", "parallel")),
        name="mm_res",
    )(a, w, x, gate)


def _merge_kernel(ya_ref, yb_ref, yc_ref, ga_ref, gb_ref, gc_ref, wa_ref, wb_ref, wc_ref, o_ref):
    pa = _dot(ya_ref[...], wa_ref[...])
    pb = _dot(yb_ref[...], wb_ref[...])
    yc = yc_ref[...]
    val = _dot(yc, wc_ref[0])
    gate = _dot(yc, wc_ref[1])
    merged = (_sigmoid(ga_ref[...]) * pa + _sigmoid(gb_ref[...]) * pb
              + _sigmoid(gc_ref[...]) * (val * _sigmoid(gate)))
    o_ref[...] = merged.astype(o_ref.dtype)


def _merge_call(ya, yb, yc, zg, wa, wb, wc2, tm, tn_target=512):
    m, ka = ya.shape
    d = wa.shape[1]
    tn = _pick(d, tn_target)
    nj = d // tn
    return pl.pallas_call(
        _merge_kernel,
        out_shape=jax.ShapeDtypeStruct((m, d), BF16),
        grid=(m // tm, nj),
        in_specs=[pl.BlockSpec((tm, ka), lambda i, j: (i, 0)),
                  pl.BlockSpec((tm, yb.shape[1]), lambda i, j: (i, 0)),
                  pl.BlockSpec((tm, yc.shape[1]), lambda i, j: (i, 0)),
                  pl.BlockSpec((tm, tn), lambda i, j: (i, j)),
                  pl.BlockSpec((tm, tn), lambda i, j: (i, nj + j)),
                  pl.BlockSpec((tm, tn), lambda i, j: (i, 2 * nj + j)),
                  pl.BlockSpec((ka, tn), lambda i, j: (0, j)),
                  pl.BlockSpec((yb.shape[1], tn), lambda i, j: (0, j)),
                  pl.BlockSpec((2, yc.shape[1], tn), lambda i, j: (0, 0, j))],
        out_specs=pl.BlockSpec((tm, tn), lambda i, j: (i, j)),
        compiler_params=_cparams(("parallel", "parallel")),
        name="merge",
    )(ya, yb, yc, zg, zg, zg, wa, wb, wc2)


def _row_shift(x, off, t_idx, seq_len):
    rows = x.shape[0]
    rolled = pltpu.roll(x, (-off) % rows, axis=0)
    src = t_idx + off
    ok = jnp.logical_and(src >= 0, src < seq_len)
    return jnp.where(ok, rolled, 0.0)


def _centred_nb(x, t_idx, seq_len, grid_w):
    if grid_w is None:
        return 0.5 * (_row_shift(x, -1, t_idx, seq_len) + _row_shift(x, 1, t_idx, seq_len))
    col = t_idx % grid_w
    left = jnp.where(col >= 1, _row_shift(x, -1, t_idx, seq_len), 0.0)
    right = jnp.where(col < grid_w - 1, _row_shift(x, 1, t_idx, seq_len), 0.0)
    up = _row_shift(x, -grid_w, t_idx, seq_len)
    down = _row_shift(x, grid_w, t_idx, seq_len)
    return 0.25 * (up + down + left + right)


def _rwkv_prep_kernel(zr_ref, zk_ref, zv_ref, zl_ref,
                      mur_ref, muk_ref, muv_ref, mul_ref,
                      kk_w_ref, ka_w_ref, rk_w_ref, w0_ref, a0_ref,
                      w2_ref, a2_ref, g2_ref, sel_ref, selt_ref,
                      rv_o, wb_o, kk_o, bonus_o, g_o,
                      *, seq_len, grid_w, ranks):
    nseq = rv_o.shape[0]

    def emit(write, a, b):
        lane = lax.broadcasted_iota(jnp.int32, a.shape, 1)
        first = lane < RWKV_HEAD
        head0 = jnp.where(first, a, pltpu.roll(b, RWKV_HEAD, axis=1))
        head1 = jnp.where(first, pltpu.roll(a, RWKV_HEAD, axis=1), b)
        for q in range(nseq):
            write(q, 0, head0[q * seq_len:(q + 1) * seq_len])
            write(q, 1, head1[q * seq_len:(q + 1) * seq_len])

    def shifted(ref, mu_ref):
        x = ref[...]
        t_idx = lax.broadcasted_iota(jnp.int32, x.shape, 0) % seq_len
        return x + mu_ref[...] * (_centred_nb(x, t_idx, seq_len, grid_w) - x)

    r = shifted(zr_ref, mur_ref)
    k = shifted(zk_ref, muk_ref)
    v = shifted(zv_ref, muv_ref)
    lo = shifted(zl_ref, mul_ref)
    rd, ra, rg = ranks
    wl = lo[:, :rd]
    al = lo[:, rd:rd + ra]
    gl = lo[:, rd + ra:rd + ra + rg]
    sel = sel_ref[...]
    selt = selt_ref[...]

    def head_sum(x):
        return _dot_exact_r(_dot_exact_r(x, sel), selt)

    kk = k * kk_w_ref[...]
    kk = kk * lax.rsqrt(head_sum(kk * kk) + 1e-12)
    def write_rv(q, hh, val):
        rv_o[q, hh] = val

    emit(write_rv, r, v)
    tw = jnp.tanh(wl).astype(BF16)
    alb = al.astype(BF16)
    for d in range(N_DIR):
        w_log = -_softplus(-(w0_ref[d] + _dot(tw, w2_ref[d]))) - 0.5
        a_d = _sigmoid(a0_ref[d] + _dot(alb, a2_ref[d]))

        def write_wb(q, hh, val, d=d):
            wb_o[d, q, hh] = val

        def write_kk(q, hh, val, d=d):
            kk_o[d, q, hh] = val

        emit(write_wb, jnp.exp(-jnp.exp(w_log)), kk * a_d)
        emit(write_kk, k * (1.0 + (a_d - 1.0) * ka_w_ref[...]), kk)
    bonus_o[...] = head_sum(r * k * rk_w_ref[...]) * v
    g_o[...] = _dot(_sigmoid(gl).astype(BF16), g2_ref[...])


def _rwkv_prep_call(zr, zk, zv, zl, row0, nrows, seq_len, grid_w, pw):
    w = zr.shape[1]
    lw = zl.shape[1]
    cb = LANES
    rb = _seq_rows(seq_len, nrows, row0)
    assert w % cb == 0
    r0 = row0 // rb
    main = pl.BlockSpec((rb, cb), lambda i, j: (r0 + i, j))
    lspec = pl.BlockSpec((rb, lw), lambda i, j: (r0 + i, 0))
    colp = pl.BlockSpec((1, cb), lambda i, j: (0, j))
    dirp = pl.BlockSpec((N_DIR, 1, cb), lambda i, j: (0, 0, j))
    ospec = pl.BlockSpec((rb, cb), lambda i, j: (i, j))
    rd, ra, rg = pw["ranks"]
    nsel = pw["sel"].shape[1]
    assert cb == 2 * RWKV_HEAD
    nseq = rb // seq_len
    bsz = nrows // seq_len
    heads = w // RWKV_HEAD
    pk = jax.ShapeDtypeStruct((bsz, heads, seq_len, cb), F32)
    pkd = jax.ShapeDtypeStruct((N_DIR, bsz, heads, seq_len, cb), F32)
    pspec = pl.BlockSpec((nseq, 2, seq_len, cb), lambda i, j: (i, j, 0, 0))
    pdspec = pl.BlockSpec((N_DIR, nseq, 2, seq_len, cb), lambda i, j: (0, i, j, 0, 0))
    return pl.pallas_call(
        functools.partial(_rwkv_prep_kernel, seq_len=seq_len, grid_w=grid_w, ranks=pw["ranks"]),
        out_shape=[pk, pkd, pkd, jax.ShapeDtypeStruct((nrows, w), F32), jax.ShapeDtypeStruct((nrows, w), F32)],
        grid=(nrows // rb, w // cb),
        in_specs=[main, main, main, lspec,
                  colp, colp, colp, pl.BlockSpec((1, lw), lambda i, j: (0, 0)),
                  colp, colp, colp, dirp, dirp,
                  pl.BlockSpec((N_DIR, rd, cb), lambda i, j: (0, 0, j)),
                  pl.BlockSpec((N_DIR, ra, cb), lambda i, j: (0, 0, j)),
                  pl.BlockSpec((rg, cb), lambda i, j: (0, j)),
                  pl.BlockSpec((cb, nsel), lambda i, j: (0, 0)),
                  pl.BlockSpec((nsel, cb), lambda i, j: (0, 0))],
        out_specs=[pspec, pdspec, pdspec, ospec, ospec],
        compiler_params=_cparams(("parallel", "parallel")),
        name="rwkv_prep",
    )(zr, zk, zv, zl, pw["mu_r"], pw["mu_k"], pw["mu_v"], pw["mu_l"],
      pw["k_k"], pw["k_a"], pw["r_k"], pw["w0"], pw["a0"],
      pw["w2"], pw["a2"], pw["g2"], pw["sel"], pw["selt"])


def _rwkv_scan_kernel(rva_ref, rvb_ref, wba_ref, wbb_ref, kka_ref, kkb_ref, s0_ref,
                      oa_ref, ob_ref, sf_ref, s_ref, ops_ref, out_ref):
    i = pl.program_id(1)
    n = s_ref.shape[0]
    nj = n // SUBLANES
    steps = out_ref.shape[0]
    half = LANES // 2
    r_off, v_off, w_off, b_off, kd_off, kk_off = (q * n for q in range(6))

    @pl.when(i == 0)
    def _():
        s_ref[...] = s0_ref[...]

    def relayout(s, carry):
        sb = steps - 1 - s
        tiles = ((rva_ref[:, :, s, :], rvb_ref[:, :, sb, :]),
                 (wba_ref[0, :, :, s, :], wbb_ref[0, :, :, sb, :]),
                 (kka_ref[0, :, :, s, :], kkb_ref[0, :, :, sb, :]))
        for p, (xa, xb) in enumerate(tiles):
            x = jnp.concatenate([xa.reshape(half, LANES), xb.reshape(half, LANES)], axis=0)
            ops_ref[s, pl.ds(p * LANES, LANES), :] = x.T
        return carry

    lax.fori_loop(0, steps, relayout, 0)

    def row(s, r):
        return jnp.broadcast_to(ops_ref[s, pl.ds(r, 1), :], (SUBLANES, LANES))

    def state(k, j):
        return s_ref[k, pl.ds(j * SUBLANES, SUBLANES), :]

    acc0 = [None] * nj
    for k in range(n):
        kkb = row(0, kk_off + k)
        for j in range(nj):
            p = state(k, j) * kkb
            acc0[j] = p if acc0[j] is None else acc0[j] + p

    def step(s, acc):
        s_next = jnp.minimum(s + 1, steps - 1)
        vv = [ops_ref[s, pl.ds(v_off + j * SUBLANES, SUBLANES), :] for j in range(nj)]
        out = [None] * nj
        acc_next = [None] * nj
        for k in range(n):
            wb = row(s, w_off + k)
            bb = row(s, b_off + k)
            kdb = row(s, kd_off + k)
            rb = row(s, r_off + k)
            kkn = row(s_next, kk_off + k)
            for j in range(nj):
                s_new = state(k, j) * wb - acc[j] * bb + vv[j] * kdb
                s_ref[k, pl.ds(j * SUBLANES, SUBLANES), :] = s_new
                q = s_new * rb
                out[j] = q if out[j] is None else out[j] + q
                p = s_new * kkn
                acc_next[j] = p if acc_next[j] is None else acc_next[j] + p
        for j in range(nj):
            out_ref[s, pl.ds(j * SUBLANES, SUBLANES), :] = out[j]
        return tuple(acc_next)

    lax.fori_loop(0, steps, step, tuple(acc0))

    def unlay(s, carry):
        sb = steps - 1 - s
        ot = out_ref[s].T
        oa_ref[:, :, s, :] = ot[:half].reshape(oa_ref.shape[0], oa_ref.shape[1], n)
        ob_ref[:, :, sb, :] = ot[half:].reshape(ob_ref.shape[0], ob_ref.shape[1], n)
        return carry

    lax.fori_loop(0, steps, unlay, 0)

    @pl.when(i == pl.num_programs(1) - 1)
    def _():
        sf_ref[...] = s_ref[...]


def _rwkv_scan_call(rv, wb, kk, s0):
    bsz, heads, t, _ = rv.shape
    n = RWKV_HEAD
    bpb = (LANES // 2) // heads
    assert bpb * heads * 2 == LANES and bsz % bpb == 0
    nblk = bsz // bpb
    tb = min(RWKV_SCAN_TB, t)
    assert t % tb == 0
    nt = t // tb
    fwd4 = pl.BlockSpec((bpb, heads, tb, LANES), lambda g, i: (g, 0, i, 0))
    bwd4 = pl.BlockSpec((bpb, heads, tb, LANES), lambda g, i: (g, 0, nt - 1 - i, 0))
    fwd5 = pl.BlockSpec((1, bpb, heads, tb, LANES), lambda g, i: (0, g, 0, i, 0))
    bwd5 = pl.BlockSpec((1, bpb, heads, tb, LANES), lambda g, i: (1, g, 0, nt - 1 - i, 0))
    st = pl.BlockSpec((n, n, LANES), lambda g, i: (0, 0, g))
    o_sds = jax.ShapeDtypeStruct((bsz, heads, t, n), F32)
    return pl.pallas_call(
        _rwkv_scan_kernel,
        out_shape=[o_sds, o_sds, jax.ShapeDtypeStruct((n, n, nblk * LANES), F32)],
        grid=(nblk, nt),
        in_specs=[fwd4, bwd4, fwd5, bwd5, fwd5, bwd5, st],
        out_specs=[pl.BlockSpec((bpb, heads, tb, n), lambda g, i: (g, 0, i, 0)),
                   pl.BlockSpec((bpb, heads, tb, n), lambda g, i: (g, 0, nt - 1 - i, 0)),
                   st],
        scratch_shapes=[pltpu.VMEM((n, n, LANES), F32),
                        pltpu.VMEM((tb, 6 * n, LANES), F32),
                        pltpu.VMEM((tb, n, LANES), F32)],
        compiler_params=_cparams(("parallel", "arbitrary")),
        name="rwkv_scan",
    )(rv, rv, wb, wb, kk, kk, s0)


def _rwkv_post_kernel(of_ref, ob_ref, bonus_ref, g_ref, lng_ref, lnb_ref, sel_ref, selt_ref, o_ref,
                      *, head):
    sel = sel_ref[...]
    selt = selt_ref[...]

    def head_sum(x):
        return _dot_exact_r(_dot_exact_r(x, sel), selt)

    def rows(ref):
        parts = [jnp.concatenate([ref[q, 0], ref[q, 1]], axis=1) for q in range(ref.shape[0])]
        return parts[0] if len(parts) == 1 else jnp.concatenate(parts, axis=0)

    o = rows(of_ref) + rows(ob_ref)
    mu = head_sum(o) * (1.0 / head)
    dlt = o - mu
    var = head_sum(dlt * dlt) * (1.0 / head)
    on = dlt * lax.rsqrt(var + RWKV_LN_EPS) * lng_ref[...] + lnb_ref[...]
    o_ref[...] = ((on + bonus_ref[...]) * g_ref[...]).astype(o_ref.dtype)


def _rwkv_post_call(of, ob, bonus, g, pw):
    bsz, heads, t, n = of.shape
    m, w = bonus.shape
    cb = LANES
    nsel = pw["sel"].shape[1]
    tm = _seq_rows(t, m, 0)
    nseq = tm // t
    main = pl.BlockSpec((tm, cb), lambda i, j: (i, j))
    colp = pl.BlockSpec((1, cb), lambda i, j: (0, j))
    ospec = pl.BlockSpec((nseq, 2, t, n), lambda i, j: (i, j, 0, 0))
    return pl.pallas_call(
        functools.partial(_rwkv_post_kernel, head=RWKV_HEAD),
        out_shape=jax.ShapeDtypeStruct((m, w), BF16),
        grid=(m // tm, w // cb),
        in_specs=[ospec, ospec, main, main, colp, colp,
                  pl.BlockSpec((cb, nsel), lambda i, j: (0, 0)),
                  pl.BlockSpec((nsel, cb), lambda i, j: (0, 0))],
        out_specs=main,
        compiler_params=_cparams(("parallel", "parallel")),
        name="rwkv_post",
    )(of, ob, bonus, g, pw["ln_g"], pw["ln_b"], pw["sel"], pw["selt"])


def _conv_silu_kernel(x_ref, w_ref, b_ref, o_ref, *, seq_len):
    x = x_ref[...]
    rows = x.shape[0]
    kw = w_ref.shape[0]
    t_idx = lax.broadcasted_iota(jnp.int32, x.shape, 0) % seq_len
    y = b_ref[...] + jnp.zeros_like(x)
    for j in range(kw):
        off = j - kw // 2
        xs = x if off == 0 else _row_shift(x, off, t_idx, seq_len)
        y = y + w_ref[pl.ds(j, 1), :] * xs
    o_ref[...] = _silu(y)


def _conv_silu_call(x, row0, nrows, seq_len, w, b):
    kw, c = w.shape
    cb = _pick(c, 256)
    rb = _seq_rows(seq_len, nrows, row0)
    r0 = row0 // rb
    return pl.pallas_call(
        functools.partial(_conv_silu_kernel, seq_len=seq_len),
        out_shape=jax.ShapeDtypeStruct((nrows, c), F32),
        grid=(nrows // rb, c // cb),
        in_specs=[pl.BlockSpec((rb, cb), lambda i, j: (r0 + i, j)),
                  pl.BlockSpec((kw, cb), lambda i, j: (0, j)),
                  pl.BlockSpec((1, cb), lambda i, j: (0, j))],
        out_specs=pl.BlockSpec((rb, cb), lambda i, j: (i, j)),
        compiler_params=_cparams(("parallel", "parallel")),
        name="ssd_conv",
    )(x, w, b.reshape(1, c))


def _ssd_scan_kernel(x_ref, b_ref, c_ref, dt_ref, dtt_ref, bias_ref, alog_ref, biast_ref, alogt_ref,
                     tri_ref, trit_ref, e_ref, s0_ref, y_ref, sf_ref, st_ref,
                     *, heads, hdim, nstate, groups, has_init):
    d = pl.program_id(0)
    c = pl.program_id(2)
    nc = pl.num_programs(2)
    hpg = heads // groups
    gw = hpg * hdim

    @pl.when(c == 0)
    def _():
        if has_init:
            for g in range(groups):
                st_ref[pl.ds(g * nstate, nstate), :] = s0_ref[0, 0, pl.ds(g * gw, gw), :].T
        else:
            st_ref[...] = jnp.zeros_like(st_ref)

    tri = tri_ref[0]
    trit = trit_ref[0]
    lch = tri.shape[0]
    e = e_ref[...]
    dtp = _softplus(dt_ref[...] + bias_ref[0])
    a = -jnp.exp(alog_ref[0])
    da = dtp * a
    cum = _dot_exact_l(tri, da)
    dat = _softplus(dtt_ref[...] + biast_ref[0]) * (-jnp.exp(alogt_ref[0]))
    cumt = _dot_exact_r(dat, trit)
    total = jnp.sum(da, axis=0, keepdims=True)
    tot8 = jnp.broadcast_to(total, (SUBLANES, LANES))
    dt_full = _dot_exact_r(dtp, e)
    din_full = _dot_exact_r(jnp.exp(cum), e)
    dst_full = _dot_exact_r(jnp.exp(total - cum), e)
    tot_full = _dot_exact_r(jnp.exp(tot8), e)[0:1, :]
    x = x_ref[...]
    xdt = x * dt_full
    xdec = (xdt * dst_full).astype(BF16)
    xdt_b = xdt.astype(BF16)
    bm = b_ref[...]
    cm = c_ref[...]
    visible = tri > 0
    for g in range(groups):
        bg = bm[:, g * nstate:(g + 1) * nstate]
        cg = cm[:, g * nstate:(g + 1) * nstate].astype(BF16)
        bgb = bg.astype(BF16)
        cb = lax.dot_general(cg, bgb, (((1,), (1,)), ((), ())), preferred_element_type=F32)
        st_g = st_ref[pl.ds(g * nstate, nstate), :]
        y_off = _dot(cg, st_g.astype(BF16)) * din_full[:, g * gw:(g + 1) * gw]
        for hh in range(hpg):
            h = g * hpg + hh
            seg = cum[:, h:h + 1] - cumt[h:h + 1, :]
            lmat = jnp.exp(jnp.where(visible, seg, -jnp.inf))
            gmat = (cb * lmat).astype(BF16)
            yd = _dot(gmat, xdt_b[:, h * hdim:(h + 1) * hdim])
            y_ref[0, :, pl.ds(h * hdim, hdim)] = yd + y_off[:, hh * hdim:(hh + 1) * hdim]
        upd = _dot(bg.T.astype(BF16), xdec[:, g * gw:(g + 1) * gw])
        st_ref[pl.ds(g * nstate, nstate), :] = st_g * tot_full[:, g * gw:(g + 1) * gw] + upd

    @pl.when(c == nc - 1)
    def _():
        for g in range(groups):
            sf_ref[0, 0, pl.ds(g * gw, gw), :] = st_ref[pl.ds(g * nstate, nstate), :].T


def _ssd_scan_call(xs, bm, cm, zdt, dtt, row0, bsz, seq_len, pw, s0):
    hp = xs.shape[1]
    gn = bm.shape[1]
    heads = pw["heads"]
    hdim = hp // heads
    nstate = gn // SSD_GROUPS
    lch = min(SSD_CHUNK, seq_len)
    nc = seq_len // lch
    assert seq_len % lch == 0 and row0 % lch == 0
    c0 = row0 // lch
    has_init = s0 is not None
    if s0 is None:
        s0 = jnp.zeros((1, 1, hp, nstate), F32)

    def cidx(d, b, c):
        return b * nc + c + d * (nc - 1 - 2 * c)

    row = lambda d, b, c: (cidx(d, b, c), 0)
    s0_map = (lambda d, b, c: (b, d, 0, 0)) if has_init else (lambda d, b, c: (0, 0, 0, 0))
    dirp = pl.BlockSpec((1, 1, LANES), lambda d, b, c: (d, 0, 0))
    dirt = pl.BlockSpec((1, heads, 1), lambda d, b, c: (d, 0, 0))
    return pl.pallas_call(
        functools.partial(_ssd_scan_kernel, heads=heads, hdim=hdim, nstate=nstate,
                          groups=SSD_GROUPS, has_init=has_init),
        out_shape=[jax.ShapeDtypeStruct((N_DIR, bsz * seq_len, hp), F32),
                   jax.ShapeDtypeStruct((bsz, N_DIR, hp, nstate), F32)],
        grid=(N_DIR, bsz, nc),
        in_specs=[pl.BlockSpec((lch, hp), row),
                  pl.BlockSpec((lch, gn), row),
                  pl.BlockSpec((lch, gn), row),
                  pl.BlockSpec((lch, LANES), lambda d, b, c: (c0 + cidx(d, b, c), 0)),
                  pl.BlockSpec((heads, lch), lambda d, b, c: (0, c0 + cidx(d, b, c))),
                  dirp, dirp, dirt, dirt,
                  pl.BlockSpec((1, lch, lch), lambda d, b, c: (d, 0, 0)),
                  pl.BlockSpec((1, lch, lch), lambda d, b, c: (d, 0, 0)),
                  pl.BlockSpec((LANES, hp), lambda d, b, c: (0, 0)),
                  pl.BlockSpec((1, 1, hp, nstate), s0_map)],
        out_specs=[pl.BlockSpec((1, lch, hp), lambda d, b, c: (d, cidx(d, b, c), 0)),
                   pl.BlockSpec((1, 1, hp, nstate), lambda d, b, c: (b, d, 0, 0))],
        scratch_shapes=[pltpu.VMEM((gn, hp // SSD_GROUPS), F32)],
        compiler_params=_cparams(("arbitrary", "arbitrary", "arbitrary")),
        name="ssd_scan",
    )(xs, bm, cm, zdt, dtt, pw["bias"], pw["alog"], pw["bias_t"], pw["alog_t"],
      pw["tri"][:, :lch, :lch], pw["trit"][:, :lch, :lch], pw["expand"], s0)


def _ssd_post_kernel(x_ref, yf_ref, yb_ref, z_ref, d_ref, g_ref, o_ref):
    y = (d_ref[...] * x_ref[...] + yf_ref[0] + yb_ref[0]) * _silu(z_ref[...])
    y = y * lax.rsqrt(jnp.mean(y * y, axis=-1, keepdims=True) + EPS) * g_ref[...]
    o_ref[...] = y.astype(o_ref.dtype)


def _ssd_post_call(xs, ydir, zz, row0, d_full, g, tm):
    n, hp = xs.shape
    r0 = row0 // tm
    return pl.pallas_call(
        _ssd_post_kernel,
        out_shape=jax.ShapeDtypeStruct((n, hp), BF16),
        grid=(n // tm,),
        in_specs=[pl.BlockSpec((tm, hp), lambda i: (i, 0)),
                  pl.BlockSpec((1, tm, hp), lambda i: (0, i, 0)),
                  pl.BlockSpec((1, tm, hp), lambda i: (1, i, 0)),
                  pl.BlockSpec((tm, hp), lambda i: (r0 + i, 0)),
                  pl.BlockSpec((1, hp), lambda i: (0, 0)),
                  pl.BlockSpec((1, hp), lambda i: (0, 0))],
        out_specs=pl.BlockSpec((tm, hp), lambda i: (i, 0)),
        compiler_params=_cparams(("parallel",)),
        name="ssd_post",
    )(xs, ydir, ydir, zz, d_full, g)


def _s5_kernel(u_ref, bmat_ref, cmat_ref, lam_ref, s0_ref, y_ref, sf_ref, st_ref, buf_ref,
               *, slab_in, slab_state):
    d = pl.program_id(0)
    tb = pl.program_id(2)
    ntb = pl.num_programs(2)
    nb, steps = u_ref.shape[0], u_ref.shape[1]
    nslab = u_ref.shape[2] // slab_in
    spl = buf_ref.shape[0]
    ncol = slab_state // LANES

    @pl.when(tb == 0)
    def _():
        st_ref[...] = s0_ref[0]

    for s_base in range(0, nslab, spl):
        lam = []
        init = []
        for q in range(spl):
            s = s_base + q
            u = u_ref[:, :, pl.ds(s * slab_in, slab_in)].reshape(nb * steps, slab_in).astype(BF16)
            bu = _dot(u, bmat_ref[0, s])
            for c in range(2 * ncol):
                buf_ref[q, c] = bu[:, c * LANES:(c + 1) * LANES]
            for c in range(ncol):
                lanes = pl.ds(s * slab_state + c * LANES, LANES)
                lam.append((jnp.broadcast_to(lam_ref[0, 0, :, lanes], (nb, LANES)),
                            jnp.broadcast_to(lam_ref[0, 1, :, lanes], (nb, LANES))))
                init.append(st_ref[0, :, lanes])
                init.append(st_ref[1, :, lanes])

        def step(i, carry):
            te = i + d * (steps - 1 - 2 * i)
            rows = pl.ds(te, nb, stride=steps)
            new = []
            for q in range(spl):
                for c in range(ncol):
                    lr, li = lam[q * ncol + c]
                    s_re = carry[2 * (q * ncol + c)]
                    s_im = carry[2 * (q * ncol + c) + 1]
                    n_re = lr * s_re - li * s_im + buf_ref[q, c, rows, :]
                    n_im = lr * s_im + li * s_re + buf_ref[q, ncol + c, rows, :]
                    buf_ref[q, c, rows, :] = n_re
                    buf_ref[q, ncol + c, rows, :] = n_im
                    new += [n_re, n_im]
            return tuple(new)

        fin = lax.fori_loop(0, steps, step, tuple(init), unroll=4)
        for q in range(spl):
            s = s_base + q
            for c in range(ncol):
                lanes = pl.ds(s * slab_state + c * LANES, LANES)
                st_ref[0, :, lanes] = fin[2 * (q * ncol + c)]
                st_ref[1, :, lanes] = fin[2 * (q * ncol + c) + 1]
            states = jnp.concatenate([buf_ref[q, c] for c in range(2 * ncol)], axis=1)
            y = _dot(states.astype(BF16), cmat_ref[0, s])
            y_ref[0, :, :, pl.ds(s * slab_in, slab_in)] = y.reshape(nb, steps, slab_in)

    @pl.when(tb == ntb - 1)
    def _():
        sf_ref[0] = st_ref[...]


def _s5_call(u3, seq0, bsz, pw, s0):
    _, t, w = u3.shape
    nb = min(SUBLANES, bsz)
    tb = min(S5_TB, t)
    assert t % tb == 0 and bsz % nb == 0 and seq0 % nb == 0
    ntb = t // tb
    b0 = seq0 // nb
    gp = pw["lam"].shape[-1]
    slab_in = pw["slab_in"]
    slab_state = pw["slab_state"]
    nslab = w // slab_in
    spl = 2 if nslab % 2 == 0 else 1

    def tidx(d, i):
        return i + d * (ntb - 1 - 2 * i)

    return pl.pallas_call(
        functools.partial(_s5_kernel, slab_in=slab_in, slab_state=slab_state),
        out_shape=[jax.ShapeDtypeStruct((N_DIR, bsz, t, w), F32),
                   jax.ShapeDtypeStruct((N_DIR, 2, bsz, gp), F32)],
        grid=(N_DIR, bsz // nb, ntb),
        in_specs=[pl.BlockSpec((nb, tb, w), lambda d, b, i: (b0 + b, tidx(d, i), 0)),
                  pl.BlockSpec((1, nslab, slab_in, 2 * slab_state), lambda d, b, i: (d, 0, 0, 0)),
                  pl.BlockSpec((1, nslab, 2 * slab_state, slab_in), lambda d, b, i: (d, 0, 0, 0)),
                  pl.BlockSpec((1, 2, 1, gp), lambda d, b, i: (d, 0, 0, 0)),
                  pl.BlockSpec((1, 2, nb, gp), lambda d, b, i: (d, 0, b, 0))],
        out_specs=[pl.BlockSpec((1, nb, tb, w), lambda d, b, i: (d, b, tidx(d, i), 0)),
                   pl.BlockSpec((1, 2, nb, gp), lambda d, b, i: (d, 0, b, 0))],
        scratch_shapes=[pltpu.VMEM((2, nb, gp), F32),
                        pltpu.VMEM((spl, 2 * slab_state // LANES, nb * tb, LANES), F32)],
        compiler_params=_cparams(("arbitrary", "arbitrary", "arbitrary")),
        name="s5_scan",
    )(u3, pw["bmat"], pw["cmat"], pw["lam"], s0)


def _s5_post_kernel(u_ref, yf_ref, yb_ref, d_ref, o_ref):
    y = d_ref[...] * u_ref[...] + yf_ref[0] + yb_ref[0]
    o_ref[...] = jax.nn.gelu(y).astype(o_ref.dtype)


def _s5_post_call(u, row0, ydir, d_full, tm):
    _, m, w = ydir.shape
    r0 = row0 // tm
    return pl.pallas_call(
        _s5_post_kernel,
        out_shape=jax.ShapeDtypeStruct((m, w), BF16),
        grid=(m // tm,),
        in_specs=[pl.BlockSpec((tm, w), lambda i: (r0 + i, 0)),
                  pl.BlockSpec((1, tm, w), lambda i: (0, i, 0)),
                  pl.BlockSpec((1, tm, w), lambda i: (1, i, 0)),
                  pl.BlockSpec((1, w), lambda i: (0, 0))],
        out_specs=pl.BlockSpec((tm, w), lambda i: (i, 0)),
        compiler_params=_cparams(("parallel",)),
        name="s5_post",
    )(u, ydir, ydir, d_full)


def _head_selectors(cb, head):
    nsel = LANES
    col = np.arange(cb)[:, None] // head
    sel = (col == np.arange(nsel)[None, :]).astype(np.float32)
    return jnp.asarray(sel, BF16), jnp.asarray(sel.T, BF16)


def _s5_weights(p, l, slab_groups):
    g, pst = p["s5_lambda_re"].shape[2:]
    cg = p["s5_b_re"].shape[-1]
    nslab = g // slab_groups
    eye = jnp.eye(slab_groups, dtype=F32)
    bmats, cmats, lams = [], [], []
    for d in range(N_DIR):
        lam_re, lam_im = p["s5_lambda_re"][l, d], p["s5_lambda_im"][l, d]
        delta = jnp.exp(p["s5_log_dt"][l, d])[:, None]
        mag = jnp.exp(lam_re * delta)
        lb_re, lb_im = mag * jnp.cos(lam_im * delta), mag * jnp.sin(lam_im * delta)
        den = lam_re * lam_re + lam_im * lam_im
        q_re = ((lb_re - 1.0) * lam_re + lb_im * lam_im) / den
        q_im = (lb_im * lam_re - (lb_re - 1.0) * lam_im) / den
        b_re, b_im = p["s5_b_re"][l, d], p["s5_b_im"][l, d]
        bb_re = q_re[..., None] * b_re - q_im[..., None] * b_im
        bb_im = q_re[..., None] * b_im + q_im[..., None] * b_re

        def in_blocks(bb):
            x = bb.reshape(nslab, slab_groups, pst, cg)
            return jnp.einsum("sgpc,gh->sgchp", x, eye).reshape(nslab, slab_groups * cg, slab_groups * pst)

        def out_blocks(cc):
            x = cc.reshape(nslab, slab_groups, cg, pst)
            return jnp.einsum("sgcp,gh->sgphc", x, eye).reshape(nslab, slab_groups * pst, slab_groups * cg)

        bmats.append(jnp.concatenate([in_blocks(bb_re), in_blocks(bb_im)], axis=-1))
        cmats.append(jnp.concatenate([out_blocks(p["s5_c_re"][l, d]), -out_blocks(p["s5_c_im"][l, d])], axis=-2))
        lams.append(jnp.stack([lb_re.reshape(1, g * pst), lb_im.reshape(1, g * pst)]))
    return dict(bmat=jnp.stack(bmats).astype(BF16), cmat=jnp.stack(cmats).astype(BF16),
                lam=jnp.stack(lams), slab_in=slab_groups * cg, slab_state=slab_groups * pst)


def _pad_cols(w, n):
    return jnp.pad(w, ((0, 0), (0, n - w.shape[1])))


def kernel(x_prompt, x_sample, state_rwkv, state_ssd, state_s5_re, state_s5_im, c, c_ctx, w_mod, b_mod, norm_g, ffn_w_in, ffn_w_out, w_in, rwkv_mu, rwkv_w0, rwkv_w2, rwkv_a0, rwkv_a2, rwkv_g2, rwkv_k_k, rwkv_k_a, rwkv_r_k, rwkv_ln_g, rwkv_ln_b, w_proj_a, ssd_conv_w, ssd_conv_b, ssd_dt_bias, ssd_a_log, ssd_d, ssd_norm_g, w_proj_b, s5_lambda_re, s5_lambda_im, s5_log_dt, s5_b_re, s5_b_im, s5_c_re, s5_c_im, s5_d, w_proj_c, w_out, final_norm_g):
    p = dict(s5_lambda_re=s5_lambda_re, s5_lambda_im=s5_lambda_im, s5_log_dt=s5_log_dt,
             s5_b_re=s5_b_re, s5_b_im=s5_b_im, s5_c_re=s5_c_re, s5_c_im=s5_c_im)
    bp, tp, dm = x_prompt.shape
    bs, ts, _ = x_sample.shape
    n_p, n_s = bp * tp, bs * ts
    m = n_p + n_s
    depth = w_mod.shape[0]
    d_ff = ffn_w_out.shape[2]
    ffp = _round_up(d_ff, 512)
    rw = rwkv_k_k.shape[1]
    rh = rw // RWKV_HEAD
    rd, ra, rg = rwkv_w2.shape[2], rwkv_a2.shape[2], rwkv_g2.shape[1]
    lora = rd + ra + rg
    sh = ssd_d.shape[1]
    sw = ssd_norm_g.shape[1]
    xbc_w = ssd_conv_w.shape[2]
    gn = (xbc_w - sw) // 2
    cw = s5_d.shape[1]
    s5_g, s5_p = s5_lambda_re.shape[2:]
    s5_cg = cw // s5_g
    slab_groups = max(1, min(s5_g, LANES // s5_cg))

    tm = _pick(math.gcd(n_p, ts), 1024, SUBLANES)
    n_tiles = m // tm
    tile_cond = np.array([0 if i * tm < n_p else 1 + (i * tm - n_p) // ts for i in range(n_tiles)])

    x = jnp.concatenate([x_prompt.reshape(n_p, dm), x_sample.reshape(n_s, dm)], axis=0)
    ncond = 1 + bs
    cond = jnp.concatenate([c_ctx[None, :], c], axis=0)
    cond8 = jnp.pad(cond, ((0, _round_up(ncond, SUBLANES) - ncond), (0, 0)))

    groups = [dict(b=bp, t=tp, row0=0, n=n_p, grid_w=None),
              dict(b=bs, t=ts, row0=n_p, n=n_s, grid_w=GRID_W)]

    sel, selt = _head_selectors(LANES, RWKV_HEAD)
    heads_per_blk = LANES // RWKV_HEAD
    tri_f = np.tril(np.ones((SSD_CHUNK, SSD_CHUNK), np.float32))
    tri = jnp.asarray(np.stack([tri_f, tri_f.T]), BF16)
    trit = jnp.asarray(np.stack([tri_f.T, tri_f]), BF16)
    expand = jnp.asarray((np.arange(LANES)[:, None] == (np.arange(sw)[None, :] // (sw // sh))).astype(np.float32), BF16)

    new_a, new_b, new_re, new_im = [], [], [], []
    for l in range(depth):
        mod = _mod_call(cond8, w_mod[l], b_mod[l])
        mods = mod.reshape(-1, N_MOD, dm)[tile_cond]
        sh1, sc1, g1, sh2, sc2, g2, sh3, sc3, g3 = [mods[:, i:i + 1, :] for i in range(N_MOD)]

        def ffn(x, idx, shv, scv, gv):
            wi = ffn_w_in[l, idx]
            w2 = jnp.stack([_pad_cols(wi[:, :d_ff], ffp), _pad_cols(wi[:, d_ff:], ffp)]).astype(BF16)
            wo = jnp.pad(ffn_w_out[l, idx], ((0, ffp - d_ff), (0, 0))).astype(BF16)
            h = _norm_mod_call(x, norm_g[l, idx * 2], shv, scv, tm)
            a = _mm_swiglu_call(h, w2, tm)
            return _mm_res_call(a, wo, x, gv, 0.5, tm)

        x = ffn(x, 0, sh1, sc1, g1)

        h = _norm_mod_call(x, norm_g[l, 1], sh2, sc2, tm)
        wl_in = w_in[l]
        offs = np.cumsum([0, rw, rw, rw, lora, sw, sw, gn, gn, sh, cw, 3 * dm])
        segs = [wl_in[:, offs[i]:offs[i + 1]].astype(BF16) for i in range(11)]
        segs[8] = _pad_cols(segs[8], LANES)
        zr, zk, zv, zl, zz, zxs, zbm, zcm, zdt, zc, zg = [_mm_call(h, wseg, tm) for wseg in segs]
        dtt = zdt[:, :sh].T

        mu = rwkv_mu[l]
        pw_r = dict(ranks=(rd, ra, rg), sel=sel, selt=selt,
                    mu_r=mu[None, :rw], mu_k=mu[None, rw:2 * rw], mu_v=mu[None, 2 * rw:3 * rw],
                    mu_l=mu[None, 3 * rw:],
                    k_k=rwkv_k_k[l][None], k_a=rwkv_k_a[l][None], r_k=rwkv_r_k[l].reshape(1, rw),
                    w0=rwkv_w0[l][:, None, :], a0=rwkv_a0[l][:, None, :],
                    w2=rwkv_w2[l].astype(BF16), a2=rwkv_a2[l].astype(BF16), g2=rwkv_g2[l].astype(BF16),
                    ln_g=rwkv_ln_g[l][None], ln_b=rwkv_ln_b[l][None])
        ya_parts, fin_a = [], None
        bpb = (LANES // 2) // rh
        for gi, gr in enumerate(groups):
            b_, t_ = gr["b"], gr["t"]
            nblk = b_ // bpb
            rv_, wb_, kk_, bonus_, g_ = _rwkv_prep_call(
                zr, zk, zv, zl, gr["row0"], gr["n"], t_, gr["grid_w"], pw_r)
            if gi == 0:
                s0c = jnp.zeros((RWKV_HEAD, RWKV_HEAD, nblk * LANES), F32)
            else:
                s0c = state_rwkv[:, l].reshape(nblk, bpb, N_DIR, rh, RWKV_HEAD, RWKV_HEAD)
                s0c = s0c.transpose(5, 4, 0, 2, 1, 3).reshape(RWKV_HEAD, RWKV_HEAD, nblk * LANES)
            o_f, o_b, sf_c = _rwkv_scan_call(rv_, wb_, kk_, s0c)
            ya_parts.append(_rwkv_post_call(o_f, o_b, bonus_, g_, pw_r))
            if gi == 0:
                fin_a = sf_c.reshape(RWKV_HEAD, RWKV_HEAD, nblk, N_DIR, bpb, rh)
                fin_a = fin_a.transpose(2, 4, 3, 5, 1, 0).reshape(b_, N_DIR, rh, RWKV_HEAD, RWKV_HEAD)
        y_a = jnp.concatenate(ya_parts)
        new_a.append(fin_a)

        cw_l, cb_l = ssd_conv_w[l], ssd_conv_b[l]
        pw_s = dict(heads=sh, tri=tri, trit=trit, expand=expand,
                    bias=jnp.pad(ssd_dt_bias[l], ((0, 0), (0, LANES - sh)))[:, None, :],
                    alog=jnp.pad(ssd_a_log[l], ((0, 0), (0, LANES - sh)))[:, None, :],
                    bias_t=ssd_dt_bias[l][:, :, None], alog_t=ssd_a_log[l][:, :, None])
        yb_parts, fin_b = [], None
        for gi, gr in enumerate(groups):
            xs_ = _conv_silu_call(zxs, gr["row0"], gr["n"], gr["t"], cw_l[:, :sw], cb_l[:sw])
            bm_ = _conv_silu_call(zbm, gr["row0"], gr["n"], gr["t"], cw_l[:, sw:sw + gn], cb_l[sw:sw + gn])
            cm_ = _conv_silu_call(zcm, gr["row0"], gr["n"], gr["t"], cw_l[:, sw + gn:], cb_l[sw + gn:])
            s0s = None if gi == 0 else state_ssd[:, l].reshape(gr["b"], N_DIR, sw, gn // SSD_GROUPS)
            ydir, sf_s = _ssd_scan_call(xs_, bm_, cm_, zdt, dtt, gr["row0"], gr["b"], gr["t"], pw_s, s0s)
            yb_parts.append(_ssd_post_call(xs_, ydir, zz, gr["row0"],
                                           jnp.repeat(ssd_d[l], sw // sh)[None, :], ssd_norm_g[l][None, :], tm))
            if gi == 0:
                fin_b = sf_s.reshape(gr["b"], N_DIR, sh, sw // sh, gn // SSD_GROUPS)
        y_b = jnp.concatenate(yb_parts)
        new_b.append(fin_b)

        pw_c = _s5_weights(p, l, slab_groups)
        yc_parts, fin_re, fin_im = [], None, None
        for gi, gr in enumerate(groups):
            b_, t_ = gr["b"], gr["t"]
            assert gr["row0"] % t_ == 0
            if gi == 0:
                s0 = jnp.zeros((N_DIR, 2, b_, s5_g * s5_p), F32)
            else:
                s0 = jnp.stack([state_s5_re[:, l], state_s5_im[:, l]])
                s0 = s0.reshape(2, b_, N_DIR, s5_g * s5_p).transpose(2, 0, 1, 3)
            y_dir, sf = _s5_call(zc.reshape(m // t_, t_, cw), gr["row0"] // t_, b_, pw_c, s0)
            y_dir = y_dir.reshape(N_DIR, gr["n"], cw)
            yc_parts.append(_s5_post_call(zc, gr["row0"], y_dir, s5_d[l][None, :], tm))
            if gi == 0:
                fin = sf.reshape(N_DIR, 2, b_, s5_g, s5_p).transpose(1, 2, 0, 3, 4)
                fin_re, fin_im = fin[0], fin[1]
        y_c = jnp.concatenate(yc_parts)
        new_re.append(fin_re)
        new_im.append(fin_im)

        wc = w_proj_c[l]
        merged = _merge_call(y_a, y_b, y_c, zg, w_proj_a[l].astype(BF16), w_proj_b[l].astype(BF16),
                             jnp.stack([wc[:, :dm], wc[:, dm:]]).astype(BF16), tm)
        x = _mm_res_call(merged, w_out[l].astype(BF16), x, g2, 1.0, tm)

        x = ffn(x, 1, sh3, sc3, g3)

    y = _final_norm_call(x, final_norm_g, tm)
    return (y[:n_p].reshape(bp, tp, dm), y[n_p:].reshape(bs, ts, dm),
            jnp.stack(new_a, axis=1), jnp.stack(new_b, axis=1),
            jnp.stack(new_re, axis=1), jnp.stack(new_im, axis=1))
```

```python
import functools
import math

import jax
import jax.numpy as jnp
import numpy as np
from jax import lax
from jax.experimental import pallas as pl
from jax.experimental.pallas import tpu as pltpu

F32 = jnp.float32
BF16 = jnp.bfloat16

LANES = 128
SUBLANES = 8
VMEM_LIMIT = 56 * 1024 * 1024

GRID_W = 64
SSD_CHUNK = 128
SSD_GROUPS = 2
N_DIR = 2
N_MOD = 9
EPS = 1e-6
RWKV_LN_EPS = 64e-5
RWKV_HEAD = 64
RWKV_SCAN_TB = 32
RWKV_SCAN_PARTS = 1
S5_TB = 64


def _cparams(sem):
    return pltpu.CompilerParams(dimension_semantics=sem, vmem_limit_bytes=VMEM_LIMIT)


def _pick(n, target, mult=LANES):
    best = None
    d = mult
    while d <= min(n, target):
        if n % d == 0:
            best = d
        d += mult
    return n if best is None else best


def _seq_rows(seq_len, nrows, row0, cap=2048):
    span = math.gcd(nrows, row0) if row0 else nrows
    assert span % seq_len == 0
    return _pick(span, max(cap, seq_len), seq_len)


def _round_up(n, m):
    return -(-n // m) * m


def _dot(a, b):
    return jnp.dot(a, b, preferred_element_type=F32)


def _split3(x):
    x1 = x.astype(BF16)
    r1 = x - x1.astype(F32)
    x2 = r1.astype(BF16)
    x3 = (r1 - x2.astype(F32)).astype(BF16)
    return x1, x2, x3


def _dot_exact_r(x, sel):
    return sum(_dot(p, sel) for p in _split3(x))


def _dot_exact_l(sel, x):
    return sum(_dot(sel, p) for p in _split3(x))


def _softplus(x):
    return jnp.maximum(x, 0.0) + jnp.log1p(jnp.exp(-jnp.abs(x)))


def _sigmoid(x):
    return 1.0 / (1.0 + jnp.exp(-x))


def _silu(x):
    return x * _sigmoid(x)


def _mod_kernel(c_ref, w_ref, b_ref, o_ref):
    c = c_ref[...]
    a = _silu(c).astype(BF16)
    o_ref[...] = _dot(a, w_ref[...].astype(BF16)) + b_ref[...]


def _mod_call(cond8, w, b):
    d, n = w.shape
    tn = _pick(n, 1024)
    return pl.pallas_call(
        _mod_kernel,
        out_shape=jax.ShapeDtypeStruct((cond8.shape[0], n), F32),
        grid=(n // tn,),
        in_specs=[pl.BlockSpec((cond8.shape[0], d), lambda j: (0, 0)),
                  pl.BlockSpec((d, tn), lambda j: (0, j)),
                  pl.BlockSpec((1, tn), lambda j: (0, j))],
        out_specs=pl.BlockSpec((cond8.shape[0], tn), lambda j: (0, j)),
        compiler_params=_cparams(("parallel",)),
        name="mod_proj",
    )(cond8, w, b.reshape(1, n))


def _norm_mod_kernel(x_ref, g_ref, sh_ref, sc_ref, o_ref):
    x = x_ref[...]
    y = x * lax.rsqrt(jnp.mean(x * x, axis=-1, keepdims=True) + EPS) * g_ref[...]
    o_ref[...] = (y * (1.0 + sc_ref[0]) + sh_ref[0]).astype(o_ref.dtype)


def _norm_mod_call(x, g, sh, sc, tm):
    m, d = x.shape
    return pl.pallas_call(
        _norm_mod_kernel,
        out_shape=jax.ShapeDtypeStruct((m, d), BF16),
        grid=(m // tm,),
        in_specs=[pl.BlockSpec((tm, d), lambda i: (i, 0)),
                  pl.BlockSpec((1, d), lambda i: (0, 0)),
                  pl.BlockSpec((1, 1, d), lambda i: (i, 0, 0)),
                  pl.BlockSpec((1, 1, d), lambda i: (i, 0, 0))],
        out_specs=pl.BlockSpec((tm, d), lambda i: (i, 0)),
        compiler_params=_cparams(("parallel",)),
        name="norm_mod",
    )(x, g.reshape(1, d), sh, sc)


def _final_norm_kernel(x_ref, g_ref, o_ref):
    x = x_ref[...]
    o_ref[...] = x * lax.rsqrt(jnp.mean(x * x, axis=-1, keepdims=True) + EPS) * g_ref[...]


def _final_norm_call(x, g, tm):
    m, d = x.shape
    return pl.pallas_call(
        _final_norm_kernel,
        out_shape=jax.ShapeDtypeStruct((m, d), F32),
        grid=(m // tm,),
        in_specs=[pl.BlockSpec((tm, d), lambda i: (i, 0)),
                  pl.BlockSpec((1, d), lambda i: (0, 0))],
        out_specs=pl.BlockSpec((tm, d), lambda i: (i, 0)),
        compiler_params=_cparams(("parallel",)),
        name="final_norm",
    )(x, g.reshape(1, d))


def _mm_kernel(a_ref, w_ref, o_ref):
    o_ref[...] = _dot(a_ref[...], w_ref[...]).astype(o_ref.dtype)


def _mm_call(a, w, tm, tn_target=1024, out_dtype=F32):
    m, k = a.shape
    n = w.shape[1]
    tn = _pick(n, tn_target)
    return pl.pallas_call(
        _mm_kernel,
        out_shape=jax.ShapeDtypeStruct((m, n), out_dtype),
        grid=(m // tm, n // tn),
        in_specs=[pl.BlockSpec((tm, k), lambda i, j: (i, 0)),
                  pl.BlockSpec((k, tn), lambda i, j: (0, j))],
        out_specs=pl.BlockSpec((tm, tn), lambda i, j: (i, j)),
        compiler_params=_cparams(("parallel", "parallel")),
        name="mm",
    )(a, w)


def _mm_swiglu_kernel(a_ref, w_ref, o_ref):
    a = a_ref[...]
    gate = _dot(a, w_ref[0])
    up = _dot(a, w_ref[1])
    o_ref[...] = (_silu(gate) * up).astype(o_ref.dtype)


def _mm_swiglu_call(a, w2, tm, tn_target=512):
    m, k = a.shape
    n = w2.shape[2]
    tn = _pick(n, tn_target)
    return pl.pallas_call(
        _mm_swiglu_kernel,
        out_shape=jax.ShapeDtypeStruct((m, n), BF16),
        grid=(m // tm, n // tn),
        in_specs=[pl.BlockSpec((tm, k), lambda i, j: (i, 0)),
                  pl.BlockSpec((2, k, tn), lambda i, j: (0, 0, j))],
        out_specs=pl.BlockSpec((tm, tn), lambda i, j: (i, j)),
        compiler_params=_cparams(("parallel", "parallel")),
        name="mm_swiglu",
    )(a, w2)


def _mm_res_kernel(a_ref, w_ref, x_ref, g_ref, o_ref, *, coef):
    o_ref[...] = x_ref[...] + (coef * g_ref[0]) * _dot(a_ref[...], w_ref[...])


def _mm_res_call(a, w, x, gate, coef, tm, tn_target=512):
    m, k = a.shape
    n = w.shape[1]
    tn = _pick(n, tn_target)
    return pl.pallas_call(
        functools.partial(_mm_res_kernel, coef=coef),
        out_shape=jax.ShapeDtypeStruct((m, n), F32),
        grid=(m // tm, n // tn),
        in_specs=[pl.BlockSpec((tm, k), lambda i, j: (i, 0)),
                  pl.BlockSpec((k, tn), lambda i, j: (0, j)),
                  pl.BlockSpec((tm, tn), lambda i, j: (i, j)),
                  pl.BlockSpec((1, 1, tn), lambda i, j: (i, 0, j))],
        out_specs=pl.BlockSpec((tm, tn), lambda i, j: (i, j)),
        compiler_params=_cparams(("parallel", "parallel")),
        name="mm_res",
    )(a, w, x, gate)


def _merge_kernel(ya_ref, yb_ref, yc_ref, ga_ref, gb_ref, gc_ref, wa_ref, wb_ref, wc_ref, o_ref):
    pa = _dot(ya_ref[...], wa_ref[...])
    pb = _dot(yb_ref[...], wb_ref[...])
    yc = yc_ref[...]
    val = _dot(yc, wc_ref[0])
    gate = _dot(yc, wc_ref[1])
    merged = (_sigmoid(ga_ref[...]) * pa + _sigmoid(gb_ref[...]) * pb
              + _sigmoid(gc_ref[...]) * (val * _sigmoid(gate)))
    o_ref[...] = merged.astype(o_ref.dtype)


def _merge_call(ya, yb, yc, zg, wa, wb, wc2, tm, tn_target=512):
    m, ka = ya.shape
    d = wa.shape[1]
    tn = _pick(d, tn_target)
    nj = d // tn
    return pl.pallas_call(
        _merge_kernel,
        out_shape=jax.ShapeDtypeStruct((m, d), BF16),
        grid=(m // tm, nj),
        in_specs=[pl.BlockSpec((tm, ka), lambda i, j: (i, 0)),
                  pl.BlockSpec((tm, yb.shape[1]), lambda i, j: (i, 0)),
                  pl.BlockSpec((tm, yc.shape[1]), lambda i, j: (i, 0)),
                  pl.BlockSpec((tm, tn), lambda i, j: (i, j)),
                  pl.BlockSpec((tm, tn), lambda i, j: (i, nj + j)),
                  pl.BlockSpec((tm, tn), lambda i, j: (i, 2 * nj + j)),
                  pl.BlockSpec((ka, tn), lambda i, j: (0, j)),
                  pl.BlockSpec((yb.shape[1], tn), lambda i, j: (0, j)),
                  pl.BlockSpec((2, yc.shape[1], tn), lambda i, j: (0, 0, j))],
        out_specs=pl.BlockSpec((tm, tn), lambda i, j: (i, j)),
        compiler_params=_cparams(("parallel", "parallel")),
        name="merge",
    )(ya, yb, yc, zg, zg, zg, wa, wb, wc2)


def _row_shift(x, off, t_idx, seq_len):
    rows = x.shape[0]
    rolled = pltpu.roll(x, (-off) % rows, axis=0)
    src = t_idx + off
    ok = jnp.logical_and(src >= 0, src < seq_len)
    return jnp.where(ok, rolled, 0.0)


def _centred_nb(x, t_idx, seq_len, grid_w):
    if grid_w is None:
        return 0.5 * (_row_shift(x, -1, t_idx, seq_len) + _row_shift(x, 1, t_idx, seq_len))
    col = t_idx % grid_w
    left = jnp.where(col >= 1, _row_shift(x, -1, t_idx, seq_len), 0.0)
    right = jnp.where(col < grid_w - 1, _row_shift(x, 1, t_idx, seq_len), 0.0)
    up = _row_shift(x, -grid_w, t_idx, seq_len)
    down = _row_shift(x, grid_w, t_idx, seq_len)
    return 0.25 * (up + down + left + right)


def _rwkv_prep_kernel(zr_ref, zk_ref, zv_ref, zl_ref,
                      mur_ref, muk_ref, muv_ref, mul_ref,
                      kk_w_ref, ka_w_ref, rk_w_ref, w0_ref, a0_ref,
                      w2_ref, a2_ref, g2_ref, sel_ref, selt_ref,
                      rv_o, wb_o, kk_o, bonus_o, g_o,
                      *, seq_len, grid_w, ranks):
    nseq = rv_o.shape[0]

    def emit(write, a, b):
        lane = lax.broadcasted_iota(jnp.int32, a.shape, 1)
        first = lane < RWKV_HEAD
        head0 = jnp.where(first, a, pltpu.roll(b, RWKV_HEAD, axis=1))
        head1 = jnp.where(first, pltpu.roll(a, RWKV_HEAD, axis=1), b)
        for q in range(nseq):
            write(q, 0, head0[q * seq_len:(q + 1) * seq_len])
            write(q, 1, head1[q * seq_len:(q + 1) * seq_len])

    def shifted(ref, mu_ref):
        x = ref[...]
        t_idx = lax.broadcasted_iota(jnp.int32, x.shape, 0) % seq_len
        return x + mu_ref[...] * (_centred_nb(x, t_idx, seq_len, grid_w) - x)

    r = shifted(zr_ref, mur_ref)
    k = shifted(zk_ref, muk_ref)
    v = shifted(zv_ref, muv_ref)
    lo = shifted(zl_ref, mul_ref)
    rd, ra, rg = ranks
    wl = lo[:, :rd]
    al = lo[:, rd:rd + ra]
    gl = lo[:, rd + ra:rd + ra + rg]
    sel = sel_ref[...]
    selt = selt_ref[...]

    def head_sum(x):
        return _dot_exact_r(_dot_exact_r(x, sel), selt)

    kk = k * kk_w_ref[...]
    kk = kk * lax.rsqrt(head_sum(kk * kk) + 1e-12)
    def write_rv(q, hh, val):
        rv_o[q, hh] = val

    emit(write_rv, r, v)
    tw = jnp.tanh(wl).astype(BF16)
    alb = al.astype(BF16)
    for d in range(N_DIR):
        w_log = -_softplus(-(w0_ref[d] + _dot(tw, w2_ref[d]))) - 0.5
        a_d = _sigmoid(a0_ref[d] + _dot(alb, a2_ref[d]))

        def write_wb(q, hh, val, d=d):
            wb_o[d, q, hh] = val

        def write_kk(q, hh, val, d=d):
            kk_o[d, q, hh] = val

        emit(write_wb, jnp.exp(-jnp.exp(w_log)), kk * a_d)
        emit(write_kk, k * (1.0 + (a_d - 1.0) * ka_w_ref[...]), kk)
    bonus_o[...] = head_sum(r * k * rk_w_ref[...]) * v
    g_o[...] = _dot(_sigmoid(gl).astype(BF16), g2_ref[...])


def _rwkv_prep_call(zr, zk, zv, zl, row0, nrows, seq_len, grid_w, pw):
    w = zr.shape[1]
    lw = zl.shape[1]
    cb = LANES
    rb = _seq_rows(seq_len, nrows, row0)
    assert w % cb == 0
    r0 = row0 // rb
    main = pl.BlockSpec((rb, cb), lambda i, j: (r0 + i, j))
    lspec = pl.BlockSpec((rb, lw), lambda i, j: (r0 + i, 0))
    colp = pl.BlockSpec((1, cb), lambda i, j: (0, j))
    dirp = pl.BlockSpec((N_DIR, 1, cb), lambda i, j: (0, 0, j))
    ospec = pl.BlockSpec((rb, cb), lambda i, j: (i, j))
    rd, ra, rg = pw["ranks"]
    nsel = pw["sel"].shape[1]
    assert cb == 2 * RWKV_HEAD
    nseq = rb // seq_len
    bsz = nrows // seq_len
    heads = w // RWKV_HEAD
    pk = jax.ShapeDtypeStruct((bsz, heads, seq_len, cb), F32)
    pkd = jax.ShapeDtypeStruct((N_DIR, bsz, heads, seq_len, cb), F32)
    pspec = pl.BlockSpec((nseq, 2, seq_len, cb), lambda i, j: (i, j, 0, 0))
    pdspec = pl.BlockSpec((N_DIR, nseq, 2, seq_len, cb), lambda i, j: (0, i, j, 0, 0))
    return pl.pallas_call(
        functools.partial(_rwkv_prep_kernel, seq_len=seq_len, grid_w=grid_w, ranks=pw["ranks"]),
        out_shape=[pk, pkd, pkd, jax.ShapeDtypeStruct((nrows, w), F32), jax.ShapeDtypeStruct((nrows, w), F32)],
        grid=(nrows // rb, w // cb),
        in_specs=[main, main, main, lspec,
                  colp, colp, colp, pl.BlockSpec((1, lw), lambda i, j: (0, 0)),
                  colp, colp, colp, dirp, dirp,
                  pl.BlockSpec((N_DIR, rd, cb), lambda i, j: (0, 0, j)),
                  pl.BlockSpec((N_DIR, ra, cb), lambda i, j: (0, 0, j)),
                  pl.BlockSpec((rg, cb), lambda i, j: (0, j)),
                  pl.BlockSpec((cb, nsel), lambda i, j: (0, 0)),
                  pl.BlockSpec((nsel, cb), lambda i, j: (0, 0))],
        out_specs=[pspec, pdspec, pdspec, ospec, ospec],
        compiler_params=_cparams(("parallel", "parallel")),
        name="rwkv_prep",
    )(zr, zk, zv, zl, pw["mu_r"], pw["mu_k"], pw["mu_v"], pw["mu_l"],
      pw["k_k"], pw["k_a"], pw["r_k"], pw["w0"], pw["a0"],
      pw["w2"], pw["a2"], pw["g2"], pw["sel"], pw["selt"])


def _load_time_rows(ref, s):
    steps, width = ref.shape[-2], ref.shape[-1]
    chains = math.prod(ref.shape[:-2])
    return ref.reshape(chains * steps, width)[pl.ds(s, chains, stride=steps), :]


def _store_time_rows(ref, s, val):
    steps, width = ref.shape[-2], ref.shape[-1]
    chains = math.prod(ref.shape[:-2])
    ref.reshape(chains * steps, width)[pl.ds(s, chains, stride=steps), :] = val


def _rwkv_scan_kernel(rva_ref, rvb_ref, wba_ref, wbb_ref, kka_ref, kkb_ref, s0_ref,
                      oa_ref, ob_ref, sf_ref, s_ref, out_ref, *ops_refs):
    i = pl.program_id(1)
    n = s_ref.shape[0]
    nj = n // SUBLANES
    steps = out_ref.shape[0]
    half = LANES // 2
    r_off, v_off, w_off, b_off, kd_off, kk_off = (q * n for q in range(6))

    @pl.when(i == 0)
    def _():
        s_ref[...] = s0_ref[...]

    srcs = ((rva_ref, rvb_ref), (wba_ref, wbb_ref), (kka_ref, kkb_ref))

    nparts = len(ops_refs)
    plen = steps // nparts

    def relayout(part, s):
        t = part * plen + s
        tb_ = steps - 1 - t
        for p, (a_ref, b_ref) in enumerate(srcs):
            x = jnp.concatenate([_load_time_rows(a_ref, t), _load_time_rows(b_ref, tb_)], axis=0)
            ops_refs[part][s, pl.ds(p * LANES, LANES), :] = x.T

    def unlay(t):
        tb_ = steps - 1 - t
        ot = out_ref[t].T
        _store_time_rows(oa_ref, t, ot[:half])
        _store_time_rows(ob_ref, tb_, ot[half:])

    out_ref[0] = jnp.zeros(out_ref.shape[1:], F32)

    def relayout_step(s, carry):
        relayout(0, s)
        return carry

    lax.fori_loop(0, plen, relayout_step, 0)

    def state(k, j):
        return s_ref[k, pl.ds(j * SUBLANES, SUBLANES), :]

    for part in range(nparts):
        ops_ref = ops_refs[part]

        def row(s, r, ops_ref=ops_ref):
            return jnp.broadcast_to(ops_ref[s, pl.ds(r, 1), :], (SUBLANES, LANES))

        acc0 = [None] * nj
        for k in range(n):
            kkb = row(0, kk_off + k)
            for j in range(nj):
                p = state(k, j) * kkb
                acc0[j] = p if acc0[j] is None else acc0[j] + p

        def step(s, acc, part=part, ops_ref=ops_ref, row=row):
            t = part * plen + s
            unlay(jnp.maximum(t - 1, 0))
            if part + 1 < nparts:
                relayout(part + 1, s)
            s_next = jnp.minimum(s + 1, plen - 1)
            vv = [ops_ref[s, pl.ds(v_off + j * SUBLANES, SUBLANES), :] for j in range(nj)]
            out = [None] * nj
            acc_next = [None] * nj
            for k in range(n):
                wb = row(s, w_off + k)
                bb = row(s, b_off + k)
                kdb = row(s, kd_off + k)
                rb = row(s, r_off + k)
                kkn = row(s_next, kk_off + k)
                for j in range(nj):
                    s_new = state(k, j) * wb - acc[j] * bb + vv[j] * kdb
                    s_ref[k, pl.ds(j * SUBLANES, SUBLANES), :] = s_new
                    q = s_new * rb
                    out[j] = q if out[j] is None else out[j] + q
                    p = s_new * kkn
                    acc_next[j] = p if acc_next[j] is None else acc_next[j] + p
            for j in range(nj):
                out_ref[t, pl.ds(j * SUBLANES, SUBLANES), :] = out[j]
            return tuple(acc_next)

        lax.fori_loop(0, plen, step, tuple(acc0))
    unlay(steps - 1)

    @pl.when(i == pl.num_programs(1) - 1)
    def _():
        sf_ref[...] = s_ref[...]


def _rwkv_scan_call(rv, wb, kk, s0):
    bsz, heads, t, _ = rv.shape
    n = RWKV_HEAD
    bpb = (LANES // 2) // heads
    assert bpb * heads * 2 == LANES and bsz % bpb == 0
    nblk = bsz // bpb
    tb = min(RWKV_SCAN_TB, t)
    nparts = RWKV_SCAN_PARTS
    assert t % tb == 0 and tb % nparts == 0
    nt = t // tb
    fwd4 =pl.BlockSpec((bpb, heads, tb, LANES), lambda g, i: (g, 0, i, 0))
    bwd4 = pl.BlockSpec((bpb, heads, tb, LANES), lambda g, i: (g, 0, nt - 1 - i, 0))
    fwd5 = pl.BlockSpec((1, bpb, heads, tb, LANES), lambda g, i: (0, g, 0, i, 0))
    bwd5 = pl.BlockSpec((1, bpb, heads, tb, LANES), lambda g, i: (1, g, 0, nt - 1 - i, 0))
    st = pl.BlockSpec((n, n, LANES), lambda g, i: (0, 0, g))
    o_sds = jax.ShapeDtypeStruct((bsz, heads, t, n), F32)
    return pl.pallas_call(
        _rwkv_scan_kernel,
        out_shape=[o_sds, o_sds, jax.ShapeDtypeStruct((n, n, nblk * LANES), F32)],
        grid=(nblk, nt),
        in_specs=[fwd4, bwd4, fwd5, bwd5, fwd5, bwd5, st],
        out_specs=[pl.BlockSpec((bpb, heads, tb, n), lambda g, i: (g, 0, i, 0)),
                   pl.BlockSpec((bpb, heads, tb, n), lambda g, i: (g, 0, nt - 1 - i, 0)),
                   st],
        scratch_shapes=[pltpu.VMEM((n, n, LANES), F32),
                        pltpu.VMEM((tb, n, LANES), F32)]
        + [pltpu.VMEM((tb // nparts, 6 * n, LANES), F32)] * nparts,
        compiler_params=_cparams(("parallel", "arbitrary")),
        name="rwkv_scan",
    )(rv, rv, wb, wb, kk, kk, s0)


def _rwkv_post_kernel(of_ref, ob_ref, bonus_ref, g_ref, lng_ref, lnb_ref, sel_ref, selt_ref, o_ref,
                      *, head):
    sel = sel_ref[...]
    selt = selt_ref[...]

    def head_sum(x):
        return _dot_exact_r(_dot_exact_r(x, sel), selt)

    def rows(ref):
        parts = [jnp.concatenate([ref[q, 0], ref[q, 1]], axis=1) for q in range(ref.shape[0])]
        return parts[0] if len(parts) == 1 else jnp.concatenate(parts, axis=0)

    o = rows(of_ref) + rows(ob_ref)
    mu = head_sum(o) * (1.0 / head)
    dlt = o - mu
    var = head_sum(dlt * dlt) * (1.0 / head)
    on = dlt * lax.rsqrt(var + RWKV_LN_EPS) * lng_ref[...] + lnb_ref[...]
    o_ref[...] = ((on + bonus_ref[...]) * g_ref[...]).astype(o_ref.dtype)


def _rwkv_post_call(of, ob, bonus, g, pw):
    bsz, heads, t, n = of.shape
    m, w = bonus.shape
    cb = LANES
    nsel = pw["sel"].shape[1]
    tm = _seq_rows(t, m, 0)
    nseq = tm // t
    main = pl.BlockSpec((tm, cb), lambda i, j: (i, j))
    colp = pl.BlockSpec((1, cb), lambda i, j: (0, j))
    ospec = pl.BlockSpec((nseq, 2, t, n), lambda i, j: (i, j, 0, 0))
    return pl.pallas_call(
        functools.partial(_rwkv_post_kernel, head=RWKV_HEAD),
        out_shape=jax.ShapeDtypeStruct((m, w), BF16),
        grid=(m // tm, w // cb),
        in_specs=[ospec, ospec, main, main, colp, colp,
                  pl.BlockSpec((cb, nsel), lambda i, j: (0, 0)),
                  pl.BlockSpec((nsel, cb), lambda i, j: (0, 0))],
        out_specs=main,
        compiler_params=_cparams(("parallel", "parallel")),
        name="rwkv_post",
    )(of, ob, bonus, g, pw["ln_g"], pw["ln_b"], pw["sel"], pw["selt"])


def _conv_silu_kernel(x_ref, w_ref, b_ref, o_ref, *, seq_len):
    x = x_ref[...]
    rows = x.shape[0]
    kw = w_ref.shape[0]
    t_idx = lax.broadcasted_iota(jnp.int32, x.shape, 0) % seq_len
    y = b_ref[...] + jnp.zeros_like(x)
    for j in range(kw):
        off = j - kw // 2
        xs = x if off == 0 else _row_shift(x, off, t_idx, seq_len)
        y = y + w_ref[pl.ds(j, 1), :] * xs
    o_ref[...] = _silu(y)


def _conv_silu_call(x, row0, nrows, seq_len, w, b):
    kw, c = w.shape
    cb = _pick(c, 256)
    rb = _seq_rows(seq_len, nrows, row0)
    r0 = row0 // rb
    return pl.pallas_call(
        functools.partial(_conv_silu_kernel, seq_len=seq_len),
        out_shape=jax.ShapeDtypeStruct((nrows, c), F32),
        grid=(nrows // rb, c // cb),
        in_specs=[pl.BlockSpec((rb, cb), lambda i, j: (r0 + i, j)),
                  pl.BlockSpec((kw, cb), lambda i, j: (0, j)),
                  pl.BlockSpec((1, cb), lambda i, j: (0, j))],
        out_specs=pl.BlockSpec((rb, cb), lambda i, j: (i, j)),
        compiler_params=_cparams(("parallel", "parallel")),
        name="ssd_conv",
    )(x, w, b.reshape(1, c))


def _ssd_scan_kernel(x_ref, b_ref, c_ref, dt_ref, dtt_ref, bias_ref, alog_ref, biast_ref, alogt_ref,
                     tri_ref, trit_ref, e_ref, s0_ref, y_ref, sf_ref, st_ref,
                     *, heads, hdim, nstate, groups, has_init):
    d = pl.program_id(0)
    c = pl.program_id(2)
    nc = pl.num_programs(2)
    hpg = heads // groups
    gw = hpg * hdim

    @pl.when(c == 0)
    def _():
        if has_init:
            for g in range(groups):
                st_ref[pl.ds(g * nstate, nstate), :] = s0_ref[0, 0, pl.ds(g * gw, gw), :].T
        else:
            st_ref[...] = jnp.zeros_like(st_ref)

    tri = tri_ref[0]
    trit = trit_ref[0]
    lch = tri.shape[0]
    e = e_ref[...]
    dtp = _softplus(dt_ref[...] + bias_ref[0])
    a = -jnp.exp(alog_ref[0])
    da = dtp * a
    cum = _dot_exact_l(tri, da)
    dat = _softplus(dtt_ref[...] + biast_ref[0]) * (-jnp.exp(alogt_ref[0]))
    cumt = _dot_exact_r(dat, trit)
    total = jnp.sum(da, axis=0, keepdims=True)
    tot8 = jnp.broadcast_to(total, (SUBLANES, LANES))
    dt_full = _dot_exact_r(dtp, e)
    din_full = _dot_exact_r(jnp.exp(cum), e)
    dst_full = _dot_exact_r(jnp.exp(total - cum), e)
    tot_full = _dot_exact_r(jnp.exp(tot8), e)[0:1, :]
    x = x_ref[...]
    xdt = x * dt_full
    xdec = (xdt * dst_full).astype(BF16)
    xdt_b = xdt.astype(BF16)
    bm = b_ref[...]
    cm = c_ref[...]
    visible = tri > 0
    for g in range(groups):
        bg = bm[:, g * nstate:(g + 1) * nstate]
        cg = cm[:, g * nstate:(g + 1) * nstate].astype(BF16)
        bgb = bg.astype(BF16)
        cb = lax.dot_general(cg, bgb, (((1,), (1,)), ((), ())), preferred_element_type=F32)
        st_g = st_ref[pl.ds(g * nstate, nstate), :]
        y_off = _dot(cg, st_g.astype(BF16)) * din_full[:, g * gw:(g + 1) * gw]
        for hh in range(hpg):
            h = g * hpg + hh
            seg = cum[:, h:h + 1] - cumt[h:h + 1, :]
            lmat = jnp.exp(jnp.where(visible, seg, -jnp.inf))
            gmat = (cb * lmat).astype(BF16)
            yd = _dot(gmat, xdt_b[:, h * hdim:(h + 1) * hdim])
            y_ref[0, :, pl.ds(h * hdim, hdim)] = yd + y_off[:, hh * hdim:(hh + 1) * hdim]
        upd = _dot(bg.T.astype(BF16), xdec[:, g * gw:(g + 1) * gw])
        st_ref[pl.ds(g * nstate, nstate), :] = st_g * tot_full[:, g * gw:(g + 1) * gw] + upd

    @pl.when(c == nc - 1)
    def _():
        for g in range(groups):
            sf_ref[0, 0, pl.ds(g * gw, gw), :] = st_ref[pl.ds(g * nstate, nstate), :].T


def _ssd_scan_call(xs, bm, cm, zdt, dtt, row0, bsz, seq_len, pw, s0):
    hp = xs.shape[1]
    gn = bm.shape[1]
    heads = pw["heads"]
    hdim = hp // heads
    nstate = gn // SSD_GROUPS
    lch = min(SSD_CHUNK, seq_len)
    nc = seq_len // lch
    assert seq_len % lch == 0 and row0 % lch == 0
    c0 = row0 // lch
    has_init = s0 is not None
    if s0 is None:
        s0 = jnp.zeros((1, 1, hp, nstate), F32)

    def cidx(d, b, c):
        return b * nc + c + d * (nc - 1 - 2 * c)

    row = lambda d, b, c: (cidx(d, b, c), 0)
    s0_map = (lambda d, b, c: (b, d, 0, 0)) if has_init else (lambda d, b, c: (0, 0, 0, 0))
    dirp = pl.BlockSpec((1, 1, LANES), lambda d, b, c: (d, 0, 0))
    dirt = pl.BlockSpec((1, heads, 1), lambda d, b, c: (d, 0, 0))
    return pl.pallas_call(
        functools.partial(_ssd_scan_kernel, heads=heads, hdim=hdim, nstate=nstate,
                          groups=SSD_GROUPS, has_init=has_init),
        out_shape=[jax.ShapeDtypeStruct((N_DIR, bsz * seq_len, hp), F32),
                   jax.ShapeDtypeStruct((bsz, N_DIR, hp, nstate), F32)],
        grid=(N_DIR, bsz, nc),
        in_specs=[pl.BlockSpec((lch, hp), row),
                  pl.BlockSpec((lch, gn), row),
                  pl.BlockSpec((lch, gn), row),
                  pl.BlockSpec((lch, LANES), lambda d, b, c: (c0 + cidx(d, b, c), 0)),
                  pl.BlockSpec((heads, lch), lambda d, b, c: (0, c0 + cidx(d, b, c))),
                  dirp, dirp, dirt, dirt,
                  pl.BlockSpec((1, lch, lch), lambda d, b, c: (d, 0, 0)),
                  pl.BlockSpec((1, lch, lch), lambda d, b, c: (d, 0, 0)),
                  pl.BlockSpec((LANES, hp), lambda d, b, c: (0, 0)),
                  pl.BlockSpec((1, 1, hp, nstate), s0_map)],
        out_specs=[pl.BlockSpec((1, lch, hp), lambda d, b, c: (d, cidx(d, b, c), 0)),
                   pl.BlockSpec((1, 1, hp, nstate), lambda d, b, c: (b, d, 0, 0))],
        scratch_shapes=[pltpu.VMEM((gn, hp // SSD_GROUPS), F32)],
        compiler_params=_cparams(("arbitrary", "arbitrary", "arbitrary")),
        name="ssd_scan",
    )(xs, bm, cm, zdt, dtt, pw["bias"], pw["alog"], pw["bias_t"], pw["alog_t"],
      pw["tri"][:, :lch, :lch], pw["trit"][:, :lch, :lch], pw["expand"], s0)


def _ssd_post_kernel(x_ref, yf_ref, yb_ref, z_ref, d_ref, g_ref, o_ref):
    y = (d_ref[...] * x_ref[...] + yf_ref[0] + yb_ref[0]) * _silu(z_ref[...])
    y = y * lax.rsqrt(jnp.mean(y * y, axis=-1, keepdims=True) + EPS) * g_ref[...]
    o_ref[...] = y.astype(o_ref.dtype)


def _ssd_post_call(xs, ydir, zz, row0, d_full, g, tm):
    n, hp = xs.shape
    r0 = row0 // tm
    return pl.pallas_call(
        _ssd_post_kernel,
        out_shape=jax.ShapeDtypeStruct((n, hp), BF16),
        grid=(n // tm,),
        in_specs=[pl.BlockSpec((tm, hp), lambda i: (i, 0)),
                  pl.BlockSpec((1, tm, hp), lambda i: (0, i, 0)),
                  pl.BlockSpec((1, tm, hp), lambda i: (1, i, 0)),
                  pl.BlockSpec((tm, hp), lambda i: (r0 + i, 0)),
                  pl.BlockSpec((1, hp), lambda i: (0, 0)),
                  pl.BlockSpec((1, hp), lambda i: (0, 0))],
        out_specs=pl.BlockSpec((tm, hp), lambda i: (i, 0)),
        compiler_params=_cparams(("parallel",)),
        name="ssd_post",
    )(xs, ydir, ydir, zz, d_full, g)


def _s5_kernel(u_ref, bmat_ref, cmat_ref, lam_ref, s0_ref, y_ref, sf_ref,
               st_ref, ubt_ref, utb_ref, buf_ref, ybuf_ref, *, slab_in, slab_state):
    d = pl.program_id(0)
    tb = pl.program_id(2)
    ntb = pl.num_programs(2)
    nb, steps = u_ref.shape[0], u_ref.shape[1]
    nslab = u_ref.shape[2] // slab_in
    spl = buf_ref.shape[0]
    ncol = slab_state // LANES
    assert slab_in == LANES

    @pl.when(tb == 0)
    def _():
        st_ref[...] = s0_ref[0]

    for s in range(nslab):
        ubt_ref[s] = u_ref[:, :, pl.ds(s * slab_in, slab_in)].reshape(nb * steps, slab_in)

    def to_time_major(t, carry):
        dst = pl.ds(pl.multiple_of(t * nb, nb), nb)
        for s in range(nslab):
            utb_ref[s, dst, :] = ubt_ref[s, pl.ds(t, nb, stride=steps), :]
        return carry

    lax.fori_loop(0, steps, to_time_major, 0, unroll=4)

    for s_base in range(0, nslab, spl):
        lam = []
        init = []
        for q in range(spl):
            s = s_base + q
            bu = _dot(utb_ref[s].astype(BF16), bmat_ref[0, s])
            for c in range(2 * ncol):
                buf_ref[q, c] = bu[:, c * LANES:(c + 1) * LANES]
            for c in range(ncol):
                lanes = pl.ds(s * slab_state + c * LANES, LANES)
                lam.append((jnp.broadcast_to(lam_ref[0, 0, :, lanes], (nb, LANES)),
                            jnp.broadcast_to(lam_ref[0, 1, :, lanes], (nb, LANES))))
                init.append(st_ref[0, :, lanes])
                init.append(st_ref[1, :, lanes])

        def step(i, carry):
            te = i + d * (steps - 1 - 2 * i)
            rows = pl.ds(pl.multiple_of(te * nb, nb), nb)
            new = []
            for q in range(spl):
                for c in range(ncol):
                    lr, li = lam[q * ncol + c]
                    s_re = carry[2 * (q * ncol + c)]
                    s_im = carry[2 * (q * ncol + c) + 1]
                    n_re = lr * s_re - li * s_im + buf_ref[q, c, rows, :]
                    n_im = lr * s_im + li * s_re + buf_ref[q, ncol + c, rows, :]
                    buf_ref[q, c, rows, :] = n_re
                    buf_ref[q, ncol + c, rows, :] = n_im
                    new += [n_re, n_im]
            return tuple(new)

        fin = lax.fori_loop(0, steps, step, tuple(init), unroll=4)
        for q in range(spl):
            s = s_base + q
            for c in range(ncol):
                lanes = pl.ds(s * slab_state + c * LANES, LANES)
                st_ref[0, :, lanes] = fin[2 * (q * ncol + c)]
                st_ref[1, :, lanes] = fin[2 * (q * ncol + c) + 1]
            states = jnp.concatenate([buf_ref[q, c] for c in range(2 * ncol)], axis=1)
            ybuf_ref[...] = _dot(states.astype(BF16), cmat_ref[0, s])
            for b in range(nb):
                y_ref[0, b, :, pl.ds(s * slab_in, slab_in)] = ybuf_ref[pl.ds(b, steps, stride=nb), :]

    @pl.when(tb == ntb - 1)
    def _():
        sf_ref[0] = st_ref[...]


def _s5_call(u3, seq0, bsz, pw, s0):
    _, t, w = u3.shape
    nb = min(SUBLANES, bsz)
    tb = min(S5_TB, t)
    assert t % tb == 0 and bsz % nb == 0 and seq0 % nb == 0
    ntb = t // tb
    b0 = seq0 // nb
    gp = pw["lam"].shape[-1]
    slab_in = pw["slab_in"]
    slab_state = pw["slab_state"]
    nslab = w // slab_in
    spl = 2 if nslab % 2 == 0 else 1

    def tidx(d, i):
        return i + d * (ntb - 1 - 2 * i)

    return pl.pallas_call(
        functools.partial(_s5_kernel, slab_in=slab_in, slab_state=slab_state),
        out_shape=[jax.ShapeDtypeStruct((N_DIR, bsz, t, w), F32),
                   jax.ShapeDtypeStruct((N_DIR, 2, bsz, gp), F32)],
        grid=(N_DIR, bsz // nb, ntb),
        in_specs=[pl.BlockSpec((nb, tb, w), lambda d, b, i: (b0 + b, tidx(d, i), 0)),
                  pl.BlockSpec((1, nslab, slab_in, 2 * slab_state), lambda d, b, i: (d, 0, 0, 0)),
                  pl.BlockSpec((1, nslab, 2 * slab_state, slab_in), lambda d, b, i: (d, 0, 0, 0)),
                  pl.BlockSpec((1, 2, 1, gp), lambda d, b, i: (d, 0, 0, 0)),
                  pl.BlockSpec((1, 2, nb, gp), lambda d, b, i: (d, 0, b, 0))],
        out_specs=[pl.BlockSpec((1, nb, tb, w), lambda d, b, i: (d, b, tidx(d, i), 0)),
                   pl.BlockSpec((1, 2, nb, gp), lambda d, b, i: (d, 0, b, 0))],
        scratch_shapes=[pltpu.VMEM((2, nb, gp), F32),
                        pltpu.VMEM((nslab, nb * tb, slab_in), F32),
                        pltpu.VMEM((nslab, nb * tb, slab_in), F32),
                        pltpu.VMEM((spl, 2 * slab_state // LANES, nb * tb, LANES), F32),
                        pltpu.VMEM((nb * tb, slab_in), F32)],
        compiler_params=_cparams(("arbitrary", "arbitrary", "arbitrary")),
        name="s5_scan",
    )(u3, pw["bmat"], pw["cmat"], pw["lam"], s0)


def _s5_post_kernel(u_ref, yf_ref, yb_ref, d_ref, o_ref):
    y = d_ref[...] * u_ref[...] + yf_ref[0] + yb_ref[0]
    o_ref[...] = jax.nn.gelu(y).astype(o_ref.dtype)


def _s5_post_call(u, row0, ydir, d_full, tm):
    _, m, w = ydir.shape
    r0 = row0 // tm
    return pl.pallas_call(
        _s5_post_kernel,
        out_shape=jax.ShapeDtypeStruct((m, w), BF16),
        grid=(m // tm,),
        in_specs=[pl.BlockSpec((tm, w), lambda i: (r0 + i, 0)),
                  pl.BlockSpec((1, tm, w), lambda i: (0, i, 0)),
                  pl.BlockSpec((1, tm, w), lambda i: (1, i, 0)),
                  pl.BlockSpec((1, w), lambda i: (0, 0))],
        out_specs=pl.BlockSpec((tm, w), lambda i: (i, 0)),
        compiler_params=_cparams(("parallel",)),
        name="s5_post",
    )(u, ydir, ydir, d_full)


def _head_selectors(cb, head):
    nsel = LANES
    col = np.arange(cb)[:, None] // head
    sel = (col == np.arange(nsel)[None, :]).astype(np.float32)
    return jnp.asarray(sel, BF16), jnp.asarray(sel.T, BF16)


def _s5_weights(p, l, slab_groups):
    g, pst = p["s5_lambda_re"].shape[2:]
    cg = p["s5_b_re"].shape[-1]
    nslab = g // slab_groups
    eye = jnp.eye(slab_groups, dtype=F32)
    bmats, cmats, lams = [], [], []
    for d in range(N_DIR):
        lam_re, lam_im = p["s5_lambda_re"][l, d], p["s5_lambda_im"][l, d]
        delta = jnp.exp(p["s5_log_dt"][l, d])[:, None]
        mag = jnp.exp(lam_re * delta)
        lb_re, lb_im = mag * jnp.cos(lam_im * delta), mag * jnp.sin(lam_im * delta)
        den = lam_re * lam_re + lam_im * lam_im
        q_re = ((lb_re - 1.0) * lam_re + lb_im * lam_im) / den
        q_im = (lb_im * lam_re - (lb_re - 1.0) * lam_im) / den
        b_re, b_im = p["s5_b_re"][l, d], p["s5_b_im"][l, d]
        bb_re = q_re[..., None] * b_re - q_im[..., None] * b_im
        bb_im = q_re[..., None] * b_im + q_im[..., None] * b_re

        def in_blocks(bb):
            x = bb.reshape(nslab, slab_groups, pst, cg)
            return jnp.einsum("sgpc,gh->sgchp", x, eye).reshape(nslab, slab_groups * cg, slab_groups * pst)

        def out_blocks(cc):
            x = cc.reshape(nslab, slab_groups, cg, pst)
            return jnp.einsum("sgcp,gh->sgphc", x, eye).reshape(nslab, slab_groups * pst, slab_groups * cg)

        bmats.append(jnp.concatenate([in_blocks(bb_re), in_blocks(bb_im)], axis=-1))
        cmats.append(jnp.concatenate([out_blocks(p["s5_c_re"][l, d]), -out_blocks(p["s5_c_im"][l, d])], axis=-2))
        lams.append(jnp.stack([lb_re.reshape(1, g * pst), lb_im.reshape(1, g * pst)]))
    return dict(bmat=jnp.stack(bmats).astype(BF16), cmat=jnp.stack(cmats).astype(BF16),
                lam=jnp.stack(lams), slab_in=slab_groups * cg, slab_state=slab_groups * pst)


def _pad_cols(w, n):
    return jnp.pad(w, ((0, 0), (0, n - w.shape[1])))


def kernel(x_prompt, x_sample, state_rwkv, state_ssd, state_s5_re, state_s5_im, c, c_ctx, w_mod, b_mod, norm_g, ffn_w_in, ffn_w_out, w_in, rwkv_mu, rwkv_w0, rwkv_w2, rwkv_a0, rwkv_a2, rwkv_g2, rwkv_k_k, rwkv_k_a, rwkv_r_k, rwkv_ln_g, rwkv_ln_b, w_proj_a, ssd_conv_w, ssd_conv_b, ssd_dt_bias, ssd_a_log, ssd_d, ssd_norm_g, w_proj_b, s5_lambda_re, s5_lambda_im, s5_log_dt, s5_b_re, s5_b_im, s5_c_re, s5_c_im, s5_d, w_proj_c, w_out, final_norm_g):
    p = dict(s5_lambda_re=s5_lambda_re, s5_lambda_im=s5_lambda_im, s5_log_dt=s5_log_dt,
             s5_b_re=s5_b_re, s5_b_im=s5_b_im, s5_c_re=s5_c_re, s5_c_im=s5_c_im)
    bp, tp, dm = x_prompt.shape
    bs, ts, _ = x_sample.shape
    n_p, n_s = bp * tp, bs * ts
    m = n_p + n_s
    depth = w_mod.shape[0]
    d_ff = ffn_w_out.shape[2]
    ffp = _round_up(d_ff, 512)
    rw = rwkv_k_k.shape[1]
    rh = rw // RWKV_HEAD
    rd, ra, rg = rwkv_w2.shape[2], rwkv_a2.shape[2], rwkv_g2.shape[1]
    lora = rd + ra + rg
    sh = ssd_d.shape[1]
    sw = ssd_norm_g.shape[1]
    xbc_w = ssd_conv_w.shape[2]
    gn = (xbc_w - sw) // 2
    cw = s5_d.shape[1]
    s5_g, s5_p = s5_lambda_re.shape[2:]
    s5_cg = cw // s5_g
    slab_groups = max(1, min(s5_g, LANES // s5_cg))

    tm = _pick(math.gcd(n_p, ts), 1024, SUBLANES)
    n_tiles = m // tm
    tile_cond = np.array([0 if i * tm < n_p else 1 + (i * tm - n_p) // ts for i in range(n_tiles)])

    x = jnp.concatenate([x_prompt.reshape(n_p, dm), x_sample.reshape(n_s, dm)], axis=0)
    ncond = 1 + bs
    cond = jnp.concatenate([c_ctx[None, :], c], axis=0)
    cond8 = jnp.pad(cond, ((0, _round_up(ncond, SUBLANES) - ncond), (0, 0)))

    groups = [dict(b=bp, t=tp, row0=0, n=n_p, grid_w=None),
              dict(b=bs, t=ts, row0=n_p, n=n_s, grid_w=GRID_W)]

    sel, selt = _head_selectors(LANES, RWKV_HEAD)
    heads_per_blk = LANES // RWKV_HEAD
    tri_f = np.tril(np.ones((SSD_CHUNK, SSD_CHUNK), np.float32))
    tri = jnp.asarray(np.stack([tri_f, tri_f.T]), BF16)
    trit = jnp.asarray(np.stack([tri_f.T, tri_f]), BF16)
    expand = jnp.asarray((np.arange(LANES)[:, None] == (np.arange(sw)[None, :] // (sw // sh))).astype(np.float32), BF16)

    new_a, new_b, new_re, new_im = [], [], [], []
    for l in range(depth):
        mod = _mod_call(cond8, w_mod[l], b_mod[l])
        mods = mod.reshape(-1, N_MOD, dm)[tile_cond]
        sh1, sc1, g1, sh2, sc2, g2, sh3, sc3, g3 = [mods[:, i:i + 1, :] for i in range(N_MOD)]

        def ffn(x, idx, shv, scv, gv):
            wi = ffn_w_in[l, idx]
            w2 = jnp.stack([_pad_cols(wi[:, :d_ff], ffp), _pad_cols(wi[:, d_ff:], ffp)]).astype(BF16)
            wo = jnp.pad(ffn_w_out[l, idx], ((0, ffp - d_ff), (0, 0))).astype(BF16)
            h = _norm_mod_call(x, norm_g[l, idx * 2], shv, scv, tm)
            a = _mm_swiglu_call(h, w2, tm)
            return _mm_res_call(a, wo, x, gv, 0.5, tm)

        x = ffn(x, 0, sh1, sc1, g1)

        h = _norm_mod_call(x, norm_g[l, 1], sh2, sc2, tm)
        wl_in = w_in[l]
        offs = np.cumsum([0, rw, rw, rw, lora, sw, sw, gn, gn, sh, cw, 3 * dm])
        segs = [wl_in[:, offs[i]:offs[i + 1]].astype(BF16) for i in range(11)]
        segs[8] = _pad_cols(segs[8], LANES)
        zr, zk, zv, zl, zz, zxs, zbm, zcm, zdt, zc, zg = [_mm_call(h, wseg, tm) for wseg in segs]
        dtt = zdt[:, :sh].T

        mu = rwkv_mu[l]
        pw_r = dict(ranks=(rd, ra, rg), sel=sel, selt=selt,
                    mu_r=mu[None, :rw], mu_k=mu[None, rw:2 * rw], mu_v=mu[None, 2 * rw:3 * rw],
                    mu_l=mu[None, 3 * rw:],
                    k_k=rwkv_k_k[l][None], k_a=rwkv_k_a[l][None], r_k=rwkv_r_k[l].reshape(1, rw),
                    w0=rwkv_w0[l][:, None, :], a0=rwkv_a0[l][:, None, :],
                    w2=rwkv_w2[l].astype(BF16), a2=rwkv_a2[l].astype(BF16), g2=rwkv_g2[l].astype(BF16),
                    ln_g=rwkv_ln_g[l][None], ln_b=rwkv_ln_b[l][None])
        ya_parts, fin_a = [], None
        bpb = (LANES // 2) // rh
        for gi, gr in enumerate(groups):
            b_, t_ = gr["b"], gr["t"]
            nblk = b_ // bpb
            rv_, wb_, kk_, bonus_, g_ = _rwkv_prep_call(
                zr, zk, zv, zl, gr["row0"], gr["n"], t_, gr["grid_w"], pw_r)
            if gi == 0:
                s0c = jnp.zeros((RWKV_HEAD, RWKV_HEAD, nblk * LANES), F32)
            else:
                s0c = state_rwkv[:, l].reshape(nblk, bpb, N_DIR, rh, RWKV_HEAD, RWKV_HEAD)
                s0c = s0c.transpose(5, 4, 0, 2, 1, 3).reshape(RWKV_HEAD, RWKV_HEAD, nblk * LANES)
            o_f, o_b, sf_c = _rwkv_scan_call(rv_, wb_, kk_, s0c)
            ya_parts.append(_rwkv_post_call(o_f, o_b, bonus_, g_, pw_r))
            if gi == 0:
                fin_a = sf_c.reshape(RWKV_HEAD, RWKV_HEAD, nblk, N_DIR, bpb, rh)
                fin_a = fin_a.transpose(2, 4, 3, 5, 1, 0).reshape(b_, N_DIR, rh, RWKV_HEAD, RWKV_HEAD)
        y_a = jnp.concatenate(ya_parts)
        new_a.append(fin_a)

        cw_l, cb_l = ssd_conv_w[l], ssd_conv_b[l]
        pw_s = dict(heads=sh, tri=tri, trit=trit, expand=expand,
                    bias=jnp.pad(ssd_dt_bias[l], ((0, 0), (0, LANES - sh)))[:, None, :],
                    alog=jnp.pad(ssd_a_log[l], ((0, 0), (0, LANES - sh)))[:, None, :],
                    bias_t=ssd_dt_bias[l][:, :, None], alog_t=ssd_a_log[l][:, :, None])
        yb_parts, fin_b = [], None
        for gi, gr in enumerate(groups):
            xs_ = _conv_silu_call(zxs, gr["row0"], gr["n"], gr["t"], cw_l[:, :sw], cb_l[:sw])
            bm_ = _conv_silu_call(zbm, gr["row0"], gr["n"], gr["t"], cw_l[:, sw:sw + gn], cb_l[sw:sw + gn])
            cm_ = _conv_silu_call(zcm, gr["row0"], gr["n"], gr["t"], cw_l[:, sw + gn:], cb_l[sw + gn:])
            s0s = None if gi == 0 else state_ssd[:, l].reshape(gr["b"], N_DIR, sw, gn // SSD_GROUPS)
            ydir, sf_s = _ssd_scan_call(xs_, bm_, cm_, zdt, dtt, gr["row0"], gr["b"], gr["t"], pw_s, s0s)
            yb_parts.append(_ssd_post_call(xs_, ydir, zz, gr["row0"],
                                           jnp.repeat(ssd_d[l], sw // sh)[None, :], ssd_norm_g[l][None, :], tm))
            if gi == 0:
                fin_b = sf_s.reshape(gr["b"], N_DIR, sh, sw // sh, gn // SSD_GROUPS)
        y_b = jnp.concatenate(yb_parts)
        new_b.append(fin_b)

        pw_c = _s5_weights(p, l, slab_groups)
        yc_parts, fin_re, fin_im = [], None, None
        for gi, gr in enumerate(groups):
            b_, t_ = gr["b"], gr["t"]
            assert gr["row0"] % t_ == 0
            if gi == 0:
                s0 = jnp.zeros((N_DIR, 2, b_, s5_g * s5_p), F32)
            else:
                s0 = jnp.stack([state_s5_re[:, l], state_s5_im[:, l]])
                s0 = s0.reshape(2, b_, N_DIR, s5_g * s5_p).transpose(2, 0, 1, 3)
            y_dir, sf = _s5_call(zc.reshape(m // t_, t_, cw), gr["row0"] // t_, b_, pw_c, s0)
            y_dir = y_dir.reshape(N_DIR, gr["n"], cw)
            yc_parts.append(_s5_post_call(zc, gr["row0"], y_dir, s5_d[l][None, :], tm))
            if gi == 0:
                fin = sf.reshape(N_DIR, 2, b_, s5_g, s5_p).transpose(1, 2, 0, 3, 4)
                fin_re, fin_im = fin[0], fin[1]
        y_c = jnp.concatenate(yc_parts)
        new_re.append(fin_re)
        new_im.append(fin_im)

        wc = w_proj_c[l]
        merged = _merge_call(y_a, y_b, y_c, zg, w_proj_a[l].astype(BF16), w_proj_b[l].astype(BF16),
                             jnp.stack([wc[:, :dm], wc[:, dm:]]).astype(BF16), tm)
        x = _mm_res_call(merged, w_out[l].astype(BF16), x, g2, 1.0, tm)

        x = ffn(x, 1, sh3, sc3, g3)

    y = _final_norm_call(x, final_norm_g, tm)
    return (y[:n_p].reshape(bp, tp, dm), y[n_p:].reshape(bs, ts, dm),
            jnp.stack(new_a, axis=1), jnp.stack(new_b, axis=1),
            jnp.stack(new_re, axis=1), jnp.stack(new_im, axis=1))
```

```python
import functools
import math

import jax
import jax.numpy as jnp
import numpy as np
from jax import lax
from jax.experimental import pallas as pl
from jax.experimental.pallas import tpu as pltpu

F32 = jnp.float32
BF16 = jnp.bfloat16

LANES = 128
SUBLANES = 8
VMEM_LIMIT = 56 * 1024 * 1024

GRID_W = 64
SSD_CHUNK = 128
SSD_GROUPS = 2
N_DIR = 2
N_MOD = 9
EPS = 1e-6
RWKV_LN_EPS = 64e-5
RWKV_HEAD = 64
RWKV_SCAN_TB = 32
RWKV_SCAN_PARTS = 1
S5_TB = 64


def _cparams(sem):
    return pltpu.CompilerParams(dimension_semantics=sem, vmem_limit_bytes=VMEM_LIMIT)


def _pick(n, target, mult=LANES):
    best = None
    d = mult
    while d <= min(n, target):
        if n % d == 0:
            best = d
        d += mult
    return n if best is None else best


def _seq_rows(seq_len, nrows, row0, cap=2048):
    span = math.gcd(nrows, row0) if row0 else nrows
    assert span % seq_len == 0
    return _pick(span, max(cap, seq_len), seq_len)


def _round_up(n, m):
    return -(-n // m) * m


def _dot(a, b):
    return jnp.dot(a, b, preferred_element_type=F32)


def _split3(x):
    x1 = x.astype(BF16)
    r1 = x - x1.astype(F32)
    x2 = r1.astype(BF16)
    x3 = (r1 - x2.astype(F32)).astype(BF16)
    return x1, x2, x3


def _dot_exact_r(x, sel):
    return sum(_dot(p, sel) for p in _split3(x))


def _dot_exact_l(sel, x):
    return sum(_dot(sel, p) for p in _split3(x))


def _softplus(x):
    return jnp.maximum(x, 0.0) + jnp.log1p(jnp.exp(-jnp.abs(x)))


def _sigmoid(x):
    return 1.0 / (1.0 + jnp.exp(-x))


def _silu(x):
    return x * _sigmoid(x)


def _mod_kernel(c_ref, w_ref, b_ref, o_ref):
    c = c_ref[...]
    a = _silu(c).astype(BF16)
    o_ref[...] = _dot(a, w_ref[...].astype(BF16)) + b_ref[...]


def _mod_call(cond8, w, b):
    d, n = w.shape
    tn = _pick(n, 1024)
    return pl.pallas_call(
        _mod_kernel,
        out_shape=jax.ShapeDtypeStruct((cond8.shape[0], n), F32),
        grid=(n // tn,),
        in_specs=[pl.BlockSpec((cond8.shape[0], d), lambda j: (0, 0)),
                  pl.BlockSpec((d, tn), lambda j: (0, j)),
                  pl.BlockSpec((1, tn), lambda j: (0, j))],
        out_specs=pl.BlockSpec((cond8.shape[0], tn), lambda j: (0, j)),
        compiler_params=_cparams(("parallel",)),
        name="mod_proj",
    )(cond8, w, b.reshape(1, n))


def _norm_mod_kernel(x_ref, g_ref, sh_ref, sc_ref, o_ref):
    x = x_ref[...]
    y = x * lax.rsqrt(jnp.mean(x * x, axis=-1, keepdims=True) + EPS) * g_ref[...]
    o_ref[...] = (y * (1.0 + sc_ref[0]) + sh_ref[0]).astype(o_ref.dtype)


def _norm_mod_call(x, g, sh, sc, tm):
    m, d = x.shape
    return pl.pallas_call(
        _norm_mod_kernel,
        out_shape=jax.ShapeDtypeStruct((m, d), BF16),
        grid=(m // tm,),
        in_specs=[pl.BlockSpec((tm, d), lambda i: (i, 0)),
                  pl.BlockSpec((1, d), lambda i: (0, 0)),
                  pl.BlockSpec((1, 1, d), lambda i: (i, 0, 0)),
                  pl.BlockSpec((1, 1, d), lambda i: (i, 0, 0))],
        out_specs=pl.BlockSpec((tm, d), lambda i: (i, 0)),
        compiler_params=_cparams(("parallel",)),
        name="norm_mod",
    )(x, g.reshape(1, d), sh, sc)


def _final_norm_kernel(x_ref, g_ref, o_ref):
    x = x_ref[...]
    o_ref[...] = x * lax.rsqrt(jnp.mean(x * x, axis=-1, keepdims=True) + EPS) * g_ref[...]


def _final_norm_call(x, g, tm):
    m, d = x.shape
    return pl.pallas_call(
        _final_norm_kernel,
        out_shape=jax.ShapeDtypeStruct((m, d), F32),
        grid=(m // tm,),
        in_specs=[pl.BlockSpec((tm, d), lambda i: (i, 0)),
                  pl.BlockSpec((1, d), lambda i: (0, 0))],
        out_specs=pl.BlockSpec((tm, d), lambda i: (i, 0)),
        compiler_params=_cparams(("parallel",)),
        name="final_norm",
    )(x, g.reshape(1, d))


def _mm_kernel(a_ref, w_ref, o_ref):
    o_ref[...] = _dot(a_ref[...], w_ref[...]).astype(o_ref.dtype)


def _mm_call(a, w, tm, tn_target=1024, out_dtype=F32):
    m, k = a.shape
    n = w.shape[1]
    tn = _pick(n, tn_target)
    return pl.pallas_call(
        _mm_kernel,
        out_shape=jax.ShapeDtypeStruct((m, n), out_dtype),
        grid=(m // tm, n // tn),
        in_specs=[pl.BlockSpec((tm, k), lambda i, j: (i, 0)),
                  pl.BlockSpec((k, tn), lambda i, j: (0, j))],
        out_specs=pl.BlockSpec((tm, tn), lambda i, j: (i, j)),
        compiler_params=_cparams(("parallel", "parallel")),
        name="mm",
    )(a, w)


def _mm_swiglu_kernel(a_ref, w_ref, o_ref):
    a = a_ref[...]
    gate = _dot(a, w_ref[0])
    up = _dot(a, w_ref[1])
    o_ref[...] = (_silu(gate) * up).astype(o_ref.dtype)


def _mm_swiglu_call(a, w2, tm, tn_target=512):
    m, k = a.shape
    n = w2.shape[2]
    tn = _pick(n, tn_target)
    return pl.pallas_call(
        _mm_swiglu_kernel,
        out_shape=jax.ShapeDtypeStruct((m, n), BF16),
        grid=(m // tm, n // tn),
        in_specs=[pl.BlockSpec((tm, k), lambda i, j: (i, 0)),
                  pl.BlockSpec((2, k, tn), lambda i, j: (0, 0, j))],
        out_specs=pl.BlockSpec((tm, tn), lambda i, j: (i, j)),
        compiler_params=_cparams(("parallel", "parallel")),
        name="mm_swiglu",
    )(a, w2)


def _mm_res_kernel(a_ref, w_ref, x_ref, g_ref, o_ref, *, coef):
    o_ref[...] = x_ref[...] + (coef * g_ref[0]) * _dot(a_ref[...], w_ref[...])


def _mm_res_call(a, w, x, gate, coef, tm, tn_target=512):
    m, k = a.shape
    n = w.shape[1]
    tn = _pick(n, tn_target)
    return pl.pallas_call(
        functools.partial(_mm_res_kernel, coef=coef),
        out_shape=jax.ShapeDtypeStruct((m, n), F32),
        grid=(m // tm, n // tn),
        in_specs=[pl.BlockSpec((tm, k), lambda i, j: (i, 0)),
                  pl.BlockSpec((k, tn), lambda i, j: (0, j)),
                  pl.BlockSpec((tm, tn), lambda i, j: (i, j)),
                  pl.BlockSpec((1, 1, tn), lambda i, j: (i, 0, j))],
        out_specs=pl.BlockSpec((tm, tn), lambda i, j: (i, j)),
        compiler_params=_cparams(("parallel", "parallel")),
        name="mm_res",
    )(a, w, x, gate)


def _merge_kernel(ya_ref, yb_ref, yc_ref, ga_ref, gb_ref, gc_ref, wa_ref, wb_ref, wc_ref, o_ref):
    pa = _dot(ya_ref[...], wa_ref[...])
    pb = _dot(yb_ref[...], wb_ref[...])
    yc = yc_ref[...]
    val = _dot(yc, wc_ref[0])
    gate = _dot(yc, wc_ref[1])
    merged = (_sigmoid(ga_ref[...]) * pa + _sigmoid(gb_ref[...]) * pb
              + _sigmoid(gc_ref[...]) * (val * _sigmoid(gate)))
    o_ref[...] = merged.astype(o_ref.dtype)


def _merge_call(ya, yb, yc, zg, wa, wb, wc2, tm, tn_target=512):
    m, ka = ya.shape
    d = wa.shape[1]
    tn = _pick(d, tn_target)
    nj = d // tn
    return pl.pallas_call(
        _merge_kernel,
        out_shape=jax.ShapeDtypeStruct((m, d), BF16),
        grid=(m // tm, nj),
        in_specs=[pl.BlockSpec((tm, ka), lambda i, j: (i, 0)),
                  pl.BlockSpec((tm, yb.shape[1]), lambda i, j: (i, 0)),
                  pl.BlockSpec((tm, yc.shape[1]), lambda i, j: (i, 0)),
                  pl.BlockSpec((tm, tn), lambda i, j: (i, j)),
                  pl.BlockSpec((tm, tn), lambda i, j: (i, nj + j)),
                  pl.BlockSpec((tm, tn), lambda i, j: (i, 2 * nj + j)),
                  pl.BlockSpec((ka, tn), lambda i, j: (0, j)),
                  pl.BlockSpec((yb.shape[1], tn), lambda i, j: (0, j)),
                  pl.BlockSpec((2, yc.shape[1], tn), lambda i, j: (0, 0, j))],
        out_specs=pl.BlockSpec((tm, tn), lambda i, j: (i, j)),
        compiler_params=_cparams(("parallel", "parallel")),
        name="merge",
    )(ya, yb, yc, zg, zg, zg, wa, wb, wc2)


def _row_shift(x, off, t_idx, seq_len):
    rows = x.shape[0]
    rolled = pltpu.roll(x, (-off) % rows, axis=0)
    src = t_idx + off
    ok = jnp.logical_and(src >= 0, src < seq_len)
    return jnp.where(ok, rolled, 0.0)


def _centred_nb(x, t_idx, seq_len, grid_w):
    if grid_w is None:
        return 0.5 * (_row_shift(x, -1, t_idx, seq_len) + _row_shift(x, 1, t_idx, seq_len))
    col = t_idx % grid_w
    left = jnp.where(col >= 1, _row_shift(x, -1, t_idx, seq_len), 0.0)
    right = jnp.where(col < grid_w - 1, _row_shift(x, 1, t_idx, seq_len), 0.0)
    up = _row_shift(x, -grid_w, t_idx, seq_len)
    down = _row_shift(x, grid_w, t_idx, seq_len)
    return 0.25 * (up + down + left + right)


def _rwkv_prep_kernel(zr_ref, zk_ref, zv_ref, zl_ref,
                      mur_ref, muk_ref, muv_ref, mul_ref,
                      kk_w_ref, ka_w_ref, rk_w_ref, w0_ref, a0_ref,
                      w2_ref, a2_ref, g2_ref, sel_ref, selt_ref,
                      rv_o, wb_o, kk_o, bonus_o, g_o,
                      *, seq_len, grid_w, ranks):
    nseq = rv_o.shape[0]

    def emit(write, a, b):
        lane = lax.broadcasted_iota(jnp.int32, a.shape, 1)
        first = lane < RWKV_HEAD
        head0 = jnp.where(first, a, pltpu.roll(b, RWKV_HEAD, axis=1))
        head1 = jnp.where(first, pltpu.roll(a, RWKV_HEAD, axis=1), b)
        for q in range(nseq):
            write(q, 0, head0[q * seq_len:(q + 1) * seq_len])
            write(q, 1, head1[q * seq_len:(q + 1) * seq_len])

    def shifted(ref, mu_ref):
        x = ref[...]
        t_idx = lax.broadcasted_iota(jnp.int32, x.shape, 0) % seq_len
        return x + mu_ref[...] * (_centred_nb(x, t_idx, seq_len, grid_w) - x)

    r = shifted(zr_ref, mur_ref)
    k = shifted(zk_ref, muk_ref)
    v = shifted(zv_ref, muv_ref)
    lo = shifted(zl_ref, mul_ref)
    rd, ra, rg = ranks
    wl = lo[:, :rd]
    al = lo[:, rd:rd + ra]
    gl = lo[:, rd + ra:rd + ra + rg]
    sel = sel_ref[...]
    selt = selt_ref[...]

    def head_sum(x):
        return _dot_exact_r(_dot_exact_r(x, sel), selt)

    kk = k * kk_w_ref[...]
    kk = kk * lax.rsqrt(head_sum(kk * kk) + 1e-12)
    def write_rv(q, hh, val):
        rv_o[q, hh] = val

    emit(write_rv, r, v)
    tw = jnp.tanh(wl).astype(BF16)
    alb = al.astype(BF16)
    for d in range(N_DIR):
        w_log = -_softplus(-(w0_ref[d] + _dot(tw, w2_ref[d]))) - 0.5
        a_d = _sigmoid(a0_ref[d] + _dot(alb, a2_ref[d]))

        def write_wb(q, hh, val, d=d):
            wb_o[d, q, hh] = val

        def write_kk(q, hh, val, d=d):
            kk_o[d, q, hh] = val

        emit(write_wb, jnp.exp(-jnp.exp(w_log)), kk * a_d)
        emit(write_kk, k * (1.0 + (a_d - 1.0) * ka_w_ref[...]), kk)
    bonus_o[...] = head_sum(r * k * rk_w_ref[...]) * v
    g_o[...] = _dot(_sigmoid(gl).astype(BF16), g2_ref[...])


def _rwkv_prep_call(zr, zk, zv, zl, row0, nrows, seq_len, grid_w, pw):
    w = zr.shape[1]
    lw = zl.shape[1]
    cb = LANES
    rb = _seq_rows(seq_len, nrows, row0)
    assert w % cb == 0
    r0 = row0 // rb
    main = pl.BlockSpec((rb, cb), lambda i, j: (r0 + i, j))
    lspec = pl.BlockSpec((rb, lw), lambda i, j: (r0 + i, 0))
    colp = pl.BlockSpec((1, cb), lambda i, j: (0, j))
    dirp = pl.BlockSpec((N_DIR, 1, cb), lambda i, j: (0, 0, j))
    ospec = pl.BlockSpec((rb, cb), lambda i, j: (i, j))
    rd, ra, rg = pw["ranks"]
    nsel = pw["sel"].shape[1]
    assert cb == 2 * RWKV_HEAD
    nseq = rb // seq_len
    bsz = nrows // seq_len
    heads = w // RWKV_HEAD
    pk = jax.ShapeDtypeStruct((bsz, heads, seq_len, cb), F32)
    pkd = jax.ShapeDtypeStruct((N_DIR, bsz, heads, seq_len, cb), F32)
    pspec = pl.BlockSpec((nseq, 2, seq_len, cb), lambda i, j: (i, j, 0, 0))
    pdspec = pl.BlockSpec((N_DIR, nseq, 2, seq_len, cb), lambda i, j: (0, i, j, 0, 0))
    return pl.pallas_call(
        functools.partial(_rwkv_prep_kernel, seq_len=seq_len, grid_w=grid_w, ranks=pw["ranks"]),
        out_shape=[pk, pkd, pkd, jax.ShapeDtypeStruct((nrows, w), F32), jax.ShapeDtypeStruct((nrows, w), F32)],
        grid=(nrows // rb, w // cb),
        in_specs=[main, main, main, lspec,
                  colp, colp, colp, pl.BlockSpec((1, lw), lambda i, j: (0, 0)),
                  colp, colp, colp, dirp, dirp,
                  pl.BlockSpec((N_DIR, rd, cb), lambda i, j: (0, 0, j)),
                  pl.BlockSpec((N_DIR, ra, cb), lambda i, j: (0, 0, j)),
                  pl.BlockSpec((rg, cb), lambda i, j: (0, j)),
                  pl.BlockSpec((cb, nsel), lambda i, j: (0, 0)),
                  pl.BlockSpec((nsel, cb), lambda i, j: (0, 0))],
        out_specs=[pspec, pdspec, pdspec, ospec, ospec],
        compiler_params=_cparams(("parallel", "parallel")),
        name="rwkv_prep",
    )(zr, zk, zv, zl, pw["mu_r"], pw["mu_k"], pw["mu_v"], pw["mu_l"],
      pw["k_k"], pw["k_a"], pw["r_k"], pw["w0"], pw["a0"],
      pw["w2"], pw["a2"], pw["g2"], pw["sel"], pw["selt"])


def _load_time_chunk(ref, t0):
    idx = (0,) * (len(ref.shape) - 4) + (slice(None), slice(None), pl.ds(t0, SUBLANES), slice(None))
    x = ref[idx]
    return x.reshape(x.shape[0] * x.shape[1], SUBLANES, x.shape[-1])


def _store_time_rows(ref, s, val):
    steps, width = ref.shape[-2], ref.shape[-1]
    chains = math.prod(ref.shape[:-2])
    ref.reshape(chains * steps, width)[pl.ds(s, chains, stride=steps), :] = val


def _rwkv_scan_kernel(rva_ref, rvb_ref, wba_ref, wbb_ref, kka_ref, kkb_ref, s0_ref,
                      oa_ref, ob_ref, sf_ref, s_ref, out_ref, *ops_refs):
    i = pl.program_id(1)
    n = s_ref.shape[0]
    nj = n // SUBLANES
    steps = out_ref.shape[0]
    half = LANES // 2
    r_off, v_off, w_off, b_off, kd_off, kk_off = (q * n for q in range(6))

    @pl.when(i == 0)
    def _():
        s_ref[...] = s0_ref[...]

    srcs = ((rva_ref, rvb_ref), (wba_ref, wbb_ref), (kka_ref, kkb_ref))

    nparts = len(ops_refs)
    plen = steps // nparts

    def relayout(part, c):
        t0 = pl.multiple_of(part * plen + c * SUBLANES, SUBLANES)
        tb0 = pl.multiple_of(steps - SUBLANES - (part * plen + c * SUBLANES), SUBLANES)
        for p, (a_ref, b_ref) in enumerate(srcs):
            xa = jnp.swapaxes(_load_time_chunk(a_ref, t0), 0, 1)
            xb = jnp.swapaxes(_load_time_chunk(b_ref, tb0), 0, 1)
            for q in range(SUBLANES):
                x = jnp.concatenate([xa[q], xb[SUBLANES - 1 - q]], axis=0)
                ops_refs[part][c * SUBLANES + q, pl.ds(p * LANES, LANES), :] = x.T

    def unlay(t):
        tb_ = steps - 1 - t
        ot = out_ref[t].T
        _store_time_rows(oa_ref, t, ot[:half])
        _store_time_rows(ob_ref, tb_, ot[half:])

    out_ref[0] = jnp.zeros(out_ref.shape[1:], F32)

    def relayout_chunk(c, carry):
        relayout(0, c)
        return carry

    lax.fori_loop(0, plen // SUBLANES, relayout_chunk, 0)

    def state(k, j):
        return s_ref[k, pl.ds(j * SUBLANES, SUBLANES), :]

    for part in range(nparts):
        ops_ref = ops_refs[part]

        def row(s, r, ops_ref=ops_ref):
            return jnp.broadcast_to(ops_ref[s, pl.ds(r, 1), :], (SUBLANES, LANES))

        acc0 = [None] * nj
        for k in range(n):
            kkb = row(0, kk_off + k)
            for j in range(nj):
                p = state(k, j) * kkb
                acc0[j] = p if acc0[j] is None else acc0[j] + p

        def step(s, acc, part=part, ops_ref=ops_ref, row=row):
            t = part * plen + s
            unlay(jnp.maximum(t - 1, 0))
            s_next = jnp.minimum(s + 1, plen - 1)
            vv = [ops_ref[s, pl.ds(v_off + j * SUBLANES, SUBLANES), :] for j in range(nj)]
            out = [None] * nj
            acc_next = [None] * nj
            for k in range(n):
                wb = row(s, w_off + k)
                bb = row(s, b_off + k)
                kdb = row(s, kd_off + k)
                rb = row(s, r_off + k)
                kkn = row(s_next, kk_off + k)
                for j in range(nj):
                    s_new = state(k, j) * wb - acc[j] * bb + vv[j] * kdb
                    s_ref[k, pl.ds(j * SUBLANES, SUBLANES), :] = s_new
                    q = s_new * rb
                    out[j] = q if out[j] is None else out[j] + q
                    p = s_new * kkn
                    acc_next[j] = p if acc_next[j] is None else acc_next[j] + p
            for j in range(nj):
                out_ref[t, pl.ds(j * SUBLANES, SUBLANES), :] = out[j]
            return tuple(acc_next)

        lax.fori_loop(0, plen, step, tuple(acc0))
    unlay(steps - 1)

    @pl.when(i == pl.num_programs(1) - 1)
    def _():
        sf_ref[...] = s_ref[...]


def _rwkv_scan_call(rv, wb, kk, s0):
    bsz, heads, t, _ = rv.shape
    n = RWKV_HEAD
    bpb = (LANES // 2) // heads
    assert bpb * heads * 2 == LANES and bsz % bpb == 0
    nblk = bsz // bpb
    tb = min(RWKV_SCAN_TB, t)
    nparts = RWKV_SCAN_PARTS
    assert t % tb == 0 and tb % nparts == 0
    nt = t // tb
    fwd4 =pl.BlockSpec((bpb, heads, tb, LANES), lambda g, i: (g, 0, i, 0))
    bwd4 = pl.BlockSpec((bpb, heads, tb, LANES), lambda g, i: (g, 0, nt - 1 - i, 0))
    fwd5 = pl.BlockSpec((1, bpb, heads, tb, LANES), lambda g, i: (0, g, 0, i, 0))
    bwd5 = pl.BlockSpec((1, bpb, heads, tb, LANES), lambda g, i: (1, g, 0, nt - 1 - i, 0))
    st = pl.BlockSpec((n, n, LANES), lambda g, i: (0, 0, g))
    o_sds = jax.ShapeDtypeStruct((bsz, heads, t, n), F32)
    return pl.pallas_call(
        _rwkv_scan_kernel,
        out_shape=[o_sds, o_sds, jax.ShapeDtypeStruct((n, n, nblk * LANES), F32)],
        grid=(nblk, nt),
        in_specs=[fwd4, bwd4, fwd5, bwd5, fwd5, bwd5, st],
        out_specs=[pl.BlockSpec((bpb, heads, tb, n), lambda g, i: (g, 0, i, 0)),
                   pl.BlockSpec((bpb, heads, tb, n), lambda g, i: (g, 0, nt - 1 - i, 0)),
                   st],
        scratch_shapes=[pltpu.VMEM((n, n, LANES), F32),
                        pltpu.VMEM((tb, n, LANES), F32)]
        + [pltpu.VMEM((tb // nparts, 6 * n, LANES), F32)] * nparts,
        compiler_params=_cparams(("parallel", "arbitrary")),
        name="rwkv_scan",
    )(rv, rv, wb, wb, kk, kk, s0)


def _rwkv_post_kernel(of_ref, ob_ref, bonus_ref, g_ref, lng_ref, lnb_ref, sel_ref, selt_ref, o_ref,
                      *, head):
    sel = sel_ref[...]
    selt = selt_ref[...]

    def head_sum(x):
        return _dot_exact_r(_dot_exact_r(x, sel), selt)

    def rows(ref):
        parts = [jnp.concatenate([ref[q, 0], ref[q, 1]], axis=1) for q in range(ref.shape[0])]
        return parts[0] if len(parts) == 1 else jnp.concatenate(parts, axis=0)

    o = rows(of_ref) + rows(ob_ref)
    mu = head_sum(o) * (1.0 / head)
    dlt = o - mu
    var = head_sum(dlt * dlt) * (1.0 / head)
    on = dlt * lax.rsqrt(var + RWKV_LN_EPS) * lng_ref[...] + lnb_ref[...]
    o_ref[...] = ((on + bonus_ref[...]) * g_ref[...]).astype(o_ref.dtype)


def _rwkv_post_call(of, ob, bonus, g, pw):
    bsz, heads, t, n = of.shape
    m, w = bonus.shape
    cb = LANES
    nsel = pw["sel"].shape[1]
    tm = _seq_rows(t, m, 0)
    nseq = tm // t
    main = pl.BlockSpec((tm, cb), lambda i, j: (i, j))
    colp = pl.BlockSpec((1, cb), lambda i, j: (0, j))
    ospec = pl.BlockSpec((nseq, 2, t, n), lambda i, j: (i, j, 0, 0))
    return pl.pallas_call(
        functools.partial(_rwkv_post_kernel, head=RWKV_HEAD),
        out_shape=jax.ShapeDtypeStruct((m, w), BF16),
        grid=(m // tm, w // cb),
        in_specs=[ospec, ospec, main, main, colp, colp,
                  pl.BlockSpec((cb, nsel), lambda i, j: (0, 0)),
                  pl.BlockSpec((nsel, cb), lambda i, j: (0, 0))],
        out_specs=main,
        compiler_params=_cparams(("parallel", "parallel")),
        name="rwkv_post",
    )(of, ob, bonus, g, pw["ln_g"], pw["ln_b"], pw["sel"], pw["selt"])


def _conv_silu_kernel(x_ref, w_ref, b_ref, o_ref, *, seq_len):
    x = x_ref[...]
    rows = x.shape[0]
    kw = w_ref.shape[0]
    t_idx = lax.broadcasted_iota(jnp.int32, x.shape, 0) % seq_len
    y = b_ref[...] + jnp.zeros_like(x)
    for j in range(kw):
        off = j - kw // 2
        xs = x if off == 0 else _row_shift(x, off, t_idx, seq_len)
        y = y + w_ref[pl.ds(j, 1), :] * xs
    o_ref[...] = _silu(y)


def _conv_silu_call(x, row0, nrows, seq_len, w, b):
    kw, c = w.shape
    cb = _pick(c, 256)
    rb = _seq_rows(seq_len, nrows, row0)
    r0 = row0 // rb
    return pl.pallas_call(
        functools.partial(_conv_silu_kernel, seq_len=seq_len),
        out_shape=jax.ShapeDtypeStruct((nrows, c), F32),
        grid=(nrows // rb, c // cb),
        in_specs=[pl.BlockSpec((rb, cb), lambda i, j: (r0 + i, j)),
                  pl.BlockSpec((kw, cb), lambda i, j: (0, j)),
                  pl.BlockSpec((1, cb), lambda i, j: (0, j))],
        out_specs=pl.BlockSpec((rb, cb), lambda i, j: (i, j)),
        compiler_params=_cparams(("parallel", "parallel")),
        name="ssd_conv",
    )(x, w, b.reshape(1, c))


def _ssd_scan_kernel(x_ref, b_ref, c_ref, dt_ref, dtt_ref, bias_ref, alog_ref, biast_ref, alogt_ref,
                     tri_ref, trit_ref, e_ref, s0_ref, y_ref, sf_ref, st_ref,
                     *, heads, hdim, nstate, groups, has_init):
    d = pl.program_id(0)
    c = pl.program_id(2)
    nc = pl.num_programs(2)
    hpg = heads // groups
    gw = hpg * hdim

    @pl.when(c == 0)
    def _():
        if has_init:
            for g in range(groups):
                st_ref[pl.ds(g * nstate, nstate), :] = s0_ref[0, 0, pl.ds(g * gw, gw), :].T
        else:
            st_ref[...] = jnp.zeros_like(st_ref)

    tri = tri_ref[0]
    trit = trit_ref[0]
    lch = tri.shape[0]
    e = e_ref[...]
    dtp = _softplus(dt_ref[...] + bias_ref[0])
    a = -jnp.exp(alog_ref[0])
    da = dtp * a
    cum = _dot_exact_l(tri, da)
    dat = _softplus(dtt_ref[...] + biast_ref[0]) * (-jnp.exp(alogt_ref[0]))
    cumt = _dot_exact_r(dat, trit)
    total = jnp.sum(da, axis=0, keepdims=True)
    tot8 = jnp.broadcast_to(total, (SUBLANES, LANES))
    dt_full = _dot_exact_r(dtp, e)
    din_full = _dot_exact_r(jnp.exp(cum), e)
    dst_full = _dot_exact_r(jnp.exp(total - cum), e)
    tot_full = _dot_exact_r(jnp.exp(tot8), e)[0:1, :]
    x = x_ref[...]
    xdt = x * dt_full
    xdec = (xdt * dst_full).astype(BF16)
    xdt_b = xdt.astype(BF16)
    bm = b_ref[...]
    cm = c_ref[...]
    visible = tri > 0
    for g in range(groups):
        bg = bm[:, g * nstate:(g + 1) * nstate]
        cg = cm[:, g * nstate:(g + 1) * nstate].astype(BF16)
        bgb = bg.astype(BF16)
        cb = lax.dot_general(cg, bgb, (((1,), (1,)), ((), ())), preferred_element_type=F32)
        st_g = st_ref[pl.ds(g * nstate, nstate), :]
        y_off = _dot(cg, st_g.astype(BF16)) * din_full[:, g * gw:(g + 1) * gw]
        for hh in range(hpg):
            h = g * hpg + hh
            seg = cum[:, h:h + 1] - cumt[h:h + 1, :]
            lmat = jnp.exp(jnp.where(visible, seg, -jnp.inf))
            gmat = (cb * lmat).astype(BF16)
            yd = _dot(gmat, xdt_b[:, h * hdim:(h + 1) * hdim])
            y_ref[0, :, pl.ds(h * hdim, hdim)] = yd + y_off[:, hh * hdim:(hh + 1) * hdim]
        upd = _dot(bg.T.astype(BF16), xdec[:, g * gw:(g + 1) * gw])
        st_ref[pl.ds(g * nstate, nstate), :] = st_g * tot_full[:, g * gw:(g + 1) * gw] + upd

    @pl.when(c == nc - 1)
    def _():
        for g in range(groups):
            sf_ref[0, 0, pl.ds(g * gw, gw), :] = st_ref[pl.ds(g * nstate, nstate), :].T


def _ssd_scan_call(xs, bm, cm, zdt, dtt, row0, bsz, seq_len, pw, s0):
    hp = xs.shape[1]
    gn = bm.shape[1]
    heads = pw["heads"]
    hdim = hp // heads
    nstate = gn // SSD_GROUPS
    lch = min(SSD_CHUNK, seq_len)
    nc = seq_len // lch
    assert seq_len % lch == 0 and row0 % lch == 0
    c0 = row0 // lch
    has_init = s0 is not None
    if s0 is None:
        s0 = jnp.zeros((1, 1, hp, nstate), F32)

    def cidx(d, b, c):
        return b * nc + c + d * (nc - 1 - 2 * c)

    row = lambda d, b, c: (cidx(d, b, c), 0)
    s0_map = (lambda d, b, c: (b, d, 0, 0)) if has_init else (lambda d, b, c: (0, 0, 0, 0))
    dirp = pl.BlockSpec((1, 1, LANES), lambda d, b, c: (d, 0, 0))
    dirt = pl.BlockSpec((1, heads, 1), lambda d, b, c: (d, 0, 0))
    return pl.pallas_call(
        functools.partial(_ssd_scan_kernel, heads=heads, hdim=hdim, nstate=nstate,
                          groups=SSD_GROUPS, has_init=has_init),
        out_shape=[jax.ShapeDtypeStruct((N_DIR, bsz * seq_len, hp), F32),
                   jax.ShapeDtypeStruct((bsz, N_DIR, hp, nstate), F32)],
        grid=(N_DIR, bsz, nc),
        in_specs=[pl.BlockSpec((lch, hp), row),
                  pl.BlockSpec((lch, gn), row),
                  pl.BlockSpec((lch, gn), row),
                  pl.BlockSpec((lch, LANES), lambda d, b, c: (c0 + cidx(d, b, c), 0)),
                  pl.BlockSpec((heads, lch), lambda d, b, c: (0, c0 + cidx(d, b, c))),
                  dirp, dirp, dirt, dirt,
                  pl.BlockSpec((1, lch, lch), lambda d, b, c: (d, 0, 0)),
                  pl.BlockSpec((1, lch, lch), lambda d, b, c: (d, 0, 0)),
                  pl.BlockSpec((LANES, hp), lambda d, b, c: (0, 0)),
                  pl.BlockSpec((1, 1, hp, nstate), s0_map)],
        out_specs=[pl.BlockSpec((1, lch, hp), lambda d, b, c: (d, cidx(d, b, c), 0)),
                   pl.BlockSpec((1, 1, hp, nstate), lambda d, b, c: (b, d, 0, 0))],
        scratch_shapes=[pltpu.VMEM((gn, hp // SSD_GROUPS), F32)],
        compiler_params=_cparams(("arbitrary", "arbitrary", "arbitrary")),
        name="ssd_scan",
    )(xs, bm, cm, zdt, dtt, pw["bias"], pw["alog"], pw["bias_t"], pw["alog_t"],
      pw["tri"][:, :lch, :lch], pw["trit"][:, :lch, :lch], pw["expand"], s0)


def _ssd_post_kernel(x_ref, yf_ref, yb_ref, z_ref, d_ref, g_ref, o_ref):
    y = (d_ref[...] * x_ref[...] + yf_ref[0] + yb_ref[0]) * _silu(z_ref[...])
    y = y * lax.rsqrt(jnp.mean(y * y, axis=-1, keepdims=True) + EPS) * g_ref[...]
    o_ref[...] = y.astype(o_ref.dtype)


def _ssd_post_call(xs, ydir, zz, row0, d_full, g, tm):
    n, hp = xs.shape
    r0 = row0 // tm
    return pl.pallas_call(
        _ssd_post_kernel,
        out_shape=jax.ShapeDtypeStruct((n, hp), BF16),
        grid=(n // tm,),
        in_specs=[pl.BlockSpec((tm, hp), lambda i: (i, 0)),
                  pl.BlockSpec((1, tm, hp), lambda i: (0, i, 0)),
                  pl.BlockSpec((1, tm, hp), lambda i: (1, i, 0)),
                  pl.BlockSpec((tm, hp), lambda i: (r0 + i, 0)),
                  pl.BlockSpec((1, hp), lambda i: (0, 0)),
                  pl.BlockSpec((1, hp), lambda i: (0, 0))],
        out_specs=pl.BlockSpec((tm, hp), lambda i: (i, 0)),
        compiler_params=_cparams(("parallel",)),
        name="ssd_post",
    )(xs, ydir, ydir, zz, d_full, g)


def _s5_kernel(u_ref, bmat_ref, cmat_ref, lam_ref, s0_ref, y_ref, sf_ref,
               st_ref, buf_ref, *, slab_in, slab_state):
    d = pl.program_id(0)
    tb = pl.program_id(2)
    ntb = pl.num_programs(2)
    nb, steps = u_ref.shape[0], u_ref.shape[1]
    nslab = u_ref.shape[2] // slab_in
    spl = buf_ref.shape[0]
    ncol = slab_state // LANES
    assert slab_in == LANES

    @pl.when(tb == 0)
    def _():
        st_ref[...] = s0_ref[0]

    for s_base in range(0, nslab, spl):
        lam = []
        init = []
        for q in range(spl):
            s = s_base + q
            u = jnp.swapaxes(u_ref[:, :, pl.ds(s * slab_in, slab_in)], 0, 1)
            bu = _dot(u.reshape(steps * nb, slab_in).astype(BF16), bmat_ref[0, s])
            for c in range(2 * ncol):
                buf_ref[q, c] = bu[:, c * LANES:(c + 1) * LANES]
            for c in range(ncol):
                lanes = pl.ds(s * slab_state + c * LANES, LANES)
                lam.append((jnp.broadcast_to(lam_ref[0, 0, :, lanes], (nb, LANES)),
                            jnp.broadcast_to(lam_ref[0, 1, :, lanes], (nb, LANES))))
                init.append(st_ref[0, :, lanes])
                init.append(st_ref[1, :, lanes])

        def step(i, carry):
            te = i + d * (steps - 1 - 2 * i)
            rows = pl.ds(pl.multiple_of(te * nb, nb), nb)
            new = []
            for q in range(spl):
                for c in range(ncol):
                    lr, li = lam[q * ncol + c]
                    s_re = carry[2 * (q * ncol + c)]
                    s_im = carry[2 * (q * ncol + c) + 1]
                    n_re = lr * s_re - li * s_im + buf_ref[q, c, rows, :]
                    n_im = lr * s_im + li * s_re + buf_ref[q, ncol + c, rows, :]
                    buf_ref[q, c, rows, :] = n_re
                    buf_ref[q, ncol + c, rows, :] = n_im
                    new += [n_re, n_im]
            return tuple(new)

        fin = lax.fori_loop(0, steps, step, tuple(init), unroll=4)
        for q in range(spl):
            s = s_base + q
            for c in range(ncol):
                lanes = pl.ds(s * slab_state + c * LANES, LANES)
                st_ref[0, :, lanes] = fin[2 * (q * ncol + c)]
                st_ref[1, :, lanes] = fin[2 * (q * ncol + c) + 1]
            states = jnp.concatenate([buf_ref[q, c] for c in range(2 * ncol)], axis=1)
            y = _dot(states.astype(BF16), cmat_ref[0, s])
            y_ref[0, :, :, pl.ds(s * slab_in, slab_in)] = jnp.swapaxes(y.reshape(steps, nb, slab_in), 0, 1)

    @pl.when(tb == ntb - 1)
    def _():
        sf_ref[0] = st_ref[...]


def _s5_call(u3, seq0, bsz, pw, s0):
    _, t, w = u3.shape
    nb = min(SUBLANES, bsz)
    tb = min(S5_TB, t)
    assert t % tb == 0 and bsz % nb == 0 and seq0 % nb == 0
    ntb = t // tb
    b0 = seq0 // nb
    gp = pw["lam"].shape[-1]
    slab_in = pw["slab_in"]
    slab_state = pw["slab_state"]
    nslab = w // slab_in
    spl = 2 if nslab % 2 == 0 else 1

    def tidx(d, i):
        return i + d * (ntb - 1 - 2 * i)

    return pl.pallas_call(
        functools.partial(_s5_kernel, slab_in=slab_in, slab_state=slab_state),
        out_shape=[jax.ShapeDtypeStruct((N_DIR, bsz, t, w), F32),
                   jax.ShapeDtypeStruct((N_DIR, 2, bsz, gp), F32)],
        grid=(N_DIR, bsz // nb, ntb),
        in_specs=[pl.BlockSpec((nb, tb, w), lambda d, b, i: (b0 + b, tidx(d, i), 0)),
                  pl.BlockSpec((1, nslab, slab_in, 2 * slab_state), lambda d, b, i: (d, 0, 0, 0)),
                  pl.BlockSpec((1, nslab, 2 * slab_state, slab_in), lambda d, b, i: (d, 0, 0, 0)),
                  pl.BlockSpec((1, 2, 1, gp), lambda d, b, i: (d, 0, 0, 0)),
                  pl.BlockSpec((1, 2, nb, gp), lambda d, b, i: (d, 0, b, 0))],
        out_specs=[pl.BlockSpec((1, nb, tb, w), lambda d, b, i: (d, b, tidx(d, i), 0)),
                   pl.BlockSpec((1, 2, nb, gp), lambda d, b, i: (d, 0, b, 0))],
        scratch_shapes=[pltpu.VMEM((2, nb, gp), F32),
                        pltpu.VMEM((spl, 2 * slab_state // LANES, nb * tb, LANES), F32)],
        compiler_params=_cparams(("arbitrary", "arbitrary", "arbitrary")),
        name="s5_scan",
    )(u3, pw["bmat"], pw["cmat"], pw["lam"], s0)


def _s5_post_kernel(u_ref, yf_ref, yb_ref, d_ref, o_ref):
    y = d_ref[...] * u_ref[...] + yf_ref[0] + yb_ref[0]
    o_ref[...] = jax.nn.gelu(y).astype(o_ref.dtype)


def _s5_post_call(u, row0, ydir, d_full, tm):
    _, m, w = ydir.shape
    r0 = row0 // tm
    return pl.pallas_call(
        _s5_post_kernel,
        out_shape=jax.ShapeDtypeStruct((m, w), BF16),
        grid=(m // tm,),
        in_specs=[pl.BlockSpec((tm, w), lambda i: (r0 + i, 0)),
                  pl.BlockSpec((1, tm, w), lambda i: (0, i, 0)),
                  pl.BlockSpec((1, tm, w), lambda i: (1, i, 0)),
                  pl.BlockSpec((1, w), lambda i: (0, 0))],
        out_specs=pl.BlockSpec((tm, w), lambda i: (i, 0)),
        compiler_params=_cparams(("parallel",)),
        name="s5_post",
    )(u, ydir, ydir, d_full)


def _head_selectors(cb, head):
    nsel = LANES
    col = np.arange(cb)[:, None] // head
    sel = (col == np.arange(nsel)[None, :]).astype(np.float32)
    return jnp.asarray(sel, BF16), jnp.asarray(sel.T, BF16)


def _s5_weights(p, l, slab_groups):
    g, pst = p["s5_lambda_re"].shape[2:]
    cg = p["s5_b_re"].shape[-1]
    nslab = g // slab_groups
    eye = jnp.eye(slab_groups, dtype=F32)
    bmats, cmats, lams = [], [], []
    for d in range(N_DIR):
        lam_re, lam_im = p["s5_lambda_re"][l, d], p["s5_lambda_im"][l, d]
        delta = jnp.exp(p["s5_log_dt"][l, d])[:, None]
        mag = jnp.exp(lam_re * delta)
        lb_re, lb_im = mag * jnp.cos(lam_im * delta), mag * jnp.sin(lam_im * delta)
        den = lam_re * lam_re + lam_im * lam_im
        q_re = ((lb_re - 1.0) * lam_re + lb_im * lam_im) / den
        q_im = (lb_im * lam_re - (lb_re - 1.0) * lam_im) / den
        b_re, b_im = p["s5_b_re"][l, d], p["s5_b_im"][l, d]
        bb_re = q_re[..., None] * b_re - q_im[..., None] * b_im
        bb_im = q_re[..., None] * b_im + q_im[..., None] * b_re

        def in_blocks(bb):
            x = bb.reshape(nslab, slab_groups, pst, cg)
            return jnp.einsum("sgpc,gh->sgchp", x, eye).reshape(nslab, slab_groups * cg, slab_groups * pst)

        def out_blocks(cc):
            x = cc.reshape(nslab, slab_groups, cg, pst)
            return jnp.einsum("sgcp,gh->sgphc", x, eye).reshape(nslab, slab_groups * pst, slab_groups * cg)

        bmats.append(jnp.concatenate([in_blocks(bb_re), in_blocks(bb_im)], axis=-1))
        cmats.append(jnp.concatenate([out_blocks(p["s5_c_re"][l, d]), -out_blocks(p["s5_c_im"][l, d])], axis=-2))
        lams.append(jnp.stack([lb_re.reshape(1, g * pst), lb_im.reshape(1, g * pst)]))
    return dict(bmat=jnp.stack(bmats).astype(BF16), cmat=jnp.stack(cmats).astype(BF16),
                lam=jnp.stack(lams), slab_in=slab_groups * cg, slab_state=slab_groups * pst)


def _pad_cols(w, n):
    return jnp.pad(w, ((0, 0), (0, n - w.shape[1])))


def kernel(x_prompt, x_sample, state_rwkv, state_ssd, state_s5_re, state_s5_im, c, c_ctx, w_mod, b_mod, norm_g, ffn_w_in, ffn_w_out, w_in, rwkv_mu, rwkv_w0, rwkv_w2, rwkv_a0, rwkv_a2, rwkv_g2, rwkv_k_k, rwkv_k_a, rwkv_r_k, rwkv_ln_g, rwkv_ln_b, w_proj_a, ssd_conv_w, ssd_conv_b, ssd_dt_bias, ssd_a_log, ssd_d, ssd_norm_g, w_proj_b, s5_lambda_re, s5_lambda_im, s5_log_dt, s5_b_re, s5_b_im, s5_c_re, s5_c_im, s5_d, w_proj_c, w_out, final_norm_g):
    p = dict(s5_lambda_re=s5_lambda_re, s5_lambda_im=s5_lambda_im, s5_log_dt=s5_log_dt,
             s5_b_re=s5_b_re, s5_b_im=s5_b_im, s5_c_re=s5_c_re, s5_c_im=s5_c_im)
    bp, tp, dm = x_prompt.shape
    bs, ts, _ = x_sample.shape
    n_p, n_s = bp * tp, bs * ts
    m = n_p + n_s
    depth = w_mod.shape[0]
    d_ff = ffn_w_out.shape[2]
    ffp = _round_up(d_ff, 512)
    rw = rwkv_k_k.shape[1]
    rh = rw // RWKV_HEAD
    rd, ra, rg = rwkv_w2.shape[2], rwkv_a2.shape[2], rwkv_g2.shape[1]
    lora = rd + ra + rg
    sh = ssd_d.shape[1]
    sw = ssd_norm_g.shape[1]
    xbc_w = ssd_conv_w.shape[2]
    gn = (xbc_w - sw) // 2
    cw = s5_d.shape[1]
    s5_g, s5_p = s5_lambda_re.shape[2:]
    s5_cg = cw // s5_g
    slab_groups = max(1, min(s5_g, LANES // s5_cg))

    tm = _pick(math.gcd(n_p, ts), 1024, SUBLANES)
    n_tiles = m // tm
    tile_cond = np.array([0 if i * tm < n_p else 1 + (i * tm - n_p) // ts for i in range(n_tiles)])

    x = jnp.concatenate([x_prompt.reshape(n_p, dm), x_sample.reshape(n_s, dm)], axis=0)
    ncond = 1 + bs
    cond = jnp.concatenate([c_ctx[None, :], c], axis=0)
    cond8 = jnp.pad(cond, ((0, _round_up(ncond, SUBLANES) - ncond), (0, 0)))

    groups = [dict(b=bp, t=tp, row0=0, n=n_p, grid_w=None),
              dict(b=bs, t=ts, row0=n_p, n=n_s, grid_w=GRID_W)]

    sel, selt = _head_selectors(LANES, RWKV_HEAD)
    heads_per_blk = LANES // RWKV_HEAD
    tri_f = np.tril(np.ones((SSD_CHUNK, SSD_CHUNK), np.float32))
    tri = jnp.asarray(np.stack([tri_f, tri_f.T]), BF16)
    trit = jnp.asarray(np.stack([tri_f.T, tri_f]), BF16)
    expand = jnp.asarray((np.arange(LANES)[:, None] == (np.arange(sw)[None, :] // (sw // sh))).astype(np.float32), BF16)

    new_a, new_b, new_re, new_im = [], [], [], []
    for l in range(depth):
        mod = _mod_call(cond8, w_mod[l], b_mod[l])
        mods = mod.reshape(-1, N_MOD, dm)[tile_cond]
        sh1, sc1, g1, sh2, sc2, g2, sh3, sc3, g3 = [mods[:, i:i + 1, :] for i in range(N_MOD)]

        def ffn(x, idx, shv, scv, gv):
            wi = ffn_w_in[l, idx]
            w2 = jnp.stack([_pad_cols(wi[:, :d_ff], ffp), _pad_cols(wi[:, d_ff:], ffp)]).astype(BF16)
            wo = jnp.pad(ffn_w_out[l, idx], ((0, ffp - d_ff), (0, 0))).astype(BF16)
            h = _norm_mod_call(x, norm_g[l, idx * 2], shv, scv, tm)
            a = _mm_swiglu_call(h, w2, tm)
            return _mm_res_call(a, wo, x, gv, 0.5, tm)

        x = ffn(x, 0, sh1, sc1, g1)

        h = _norm_mod_call(x, norm_g[l, 1], sh2, sc2, tm)
        wl_in = w_in[l]
        offs = np.cumsum([0, rw, rw, rw, lora, sw, sw, gn, gn, sh, cw, 3 * dm])
        segs = [wl_in[:, offs[i]:offs[i + 1]].astype(BF16) for i in range(11)]
        segs[8] = _pad_cols(segs[8], LANES)
        zr, zk, zv, zl, zz, zxs, zbm, zcm, zdt, zc, zg = [_mm_call(h, wseg, tm) for wseg in segs]
        dtt = zdt[:, :sh].T

        mu = rwkv_mu[l]
        pw_r = dict(ranks=(rd, ra, rg), sel=sel, selt=selt,
                    mu_r=mu[None, :rw], mu_k=mu[None, rw:2 * rw], mu_v=mu[None, 2 * rw:3 * rw],
                    mu_l=mu[None, 3 * rw:],
                    k_k=rwkv_k_k[l][None], k_a=rwkv_k_a[l][None], r_k=rwkv_r_k[l].reshape(1, rw),
                    w0=rwkv_w0[l][:, None, :], a0=rwkv_a0[l][:, None, :],
                    w2=rwkv_w2[l].astype(BF16), a2=rwkv_a2[l].astype(BF16), g2=rwkv_g2[l].astype(BF16),
                    ln_g=rwkv_ln_g[l][None], ln_b=rwkv_ln_b[l][None])
        ya_parts, fin_a = [], None
        bpb = (LANES // 2) // rh
        for gi, gr in enumerate(groups):
            b_, t_ = gr["b"], gr["t"]
            nblk = b_ // bpb
            rv_, wb_, kk_, bonus_, g_ = _rwkv_prep_call(
                zr, zk, zv, zl, gr["row0"], gr["n"], t_, gr["grid_w"], pw_r)
            if gi == 0:
                s0c = jnp.zeros((RWKV_HEAD, RWKV_HEAD, nblk * LANES), F32)
            else:
                s0c = state_rwkv[:, l].reshape(nblk, bpb, N_DIR, rh, RWKV_HEAD, RWKV_HEAD)
                s0c = s0c.transpose(5, 4, 0, 2, 1, 3).reshape(RWKV_HEAD, RWKV_HEAD, nblk * LANES)
            o_f, o_b, sf_c = _rwkv_scan_call(rv_, wb_, kk_, s0c)
            ya_parts.append(_rwkv_post_call(o_f, o_b, bonus_, g_, pw_r))
            if gi == 0:
                fin_a = sf_c.reshape(RWKV_HEAD, RWKV_HEAD, nblk, N_DIR, bpb, rh)
                fin_a = fin_a.transpose(2, 4, 3, 5, 1, 0).reshape(b_, N_DIR, rh, RWKV_HEAD, RWKV_HEAD)
        y_a = jnp.concatenate(ya_parts)
        new_a.append(fin_a)

        cw_l, cb_l = ssd_conv_w[l], ssd_conv_b[l]
        pw_s = dict(heads=sh, tri=tri, trit=trit, expand=expand,
                    bias=jnp.pad(ssd_dt_bias[l], ((0, 0), (0, LANES - sh)))[:, None, :],
                    alog=jnp.pad(ssd_a_log[l], ((0, 0), (0, LANES - sh)))[:, None, :],
                    bias_t=ssd_dt_bias[l][:, :, None], alog_t=ssd_a_log[l][:, :, None])
        yb_parts, fin_b = [], None
        for gi, gr in enumerate(groups):
            xs_ = _conv_silu_call(zxs, gr["row0"], gr["n"], gr["t"], cw_l[:, :sw], cb_l[:sw])
            bm_ = _conv_silu_call(zbm, gr["row0"], gr["n"], gr["t"], cw_l[:, sw:sw + gn], cb_l[sw:sw + gn])
            cm_ = _conv_silu_call(zcm, gr["row0"], gr["n"], gr["t"], cw_l[:, sw + gn:], cb_l[sw + gn:])
            s0s = None if gi == 0 else state_ssd[:, l].reshape(gr["b"], N_DIR, sw, gn // SSD_GROUPS)
            ydir, sf_s = _ssd_scan_call(xs_, bm_, cm_, zdt, dtt, gr["row0"], gr["b"], gr["t"], pw_s, s0s)
            yb_parts.append(_ssd_post_call(xs_, ydir, zz, gr["row0"],
                                           jnp.repeat(ssd_d[l], sw // sh)[None, :], ssd_norm_g[l][None, :], tm))
            if gi == 0:
                fin_b = sf_s.reshape(gr["b"], N_DIR, sh, sw // sh, gn // SSD_GROUPS)
        y_b = jnp.concatenate(yb_parts)
        new_b.append(fin_b)

        pw_c = _s5_weights(p, l, slab_groups)
        yc_parts, fin_re, fin_im = [], None, None
        for gi, gr in enumerate(groups):
            b_, t_ = gr["b"], gr["t"]
            assert gr["row0"] % t_ == 0
            if gi == 0:
                s0 = jnp.zeros((N_DIR, 2, b_, s5_g * s5_p), F32)
            else:
                s0 = jnp.stack([state_s5_re[:, l], state_s5_im[:, l]])
                s0 = s0.reshape(2, b_, N_DIR, s5_g * s5_p).transpose(2, 0, 1, 3)
            y_dir, sf = _s5_call(zc.reshape(m // t_, t_, cw), gr["row0"] // t_, b_, pw_c, s0)
            y_dir = y_dir.reshape(N_DIR, gr["n"], cw)
            yc_parts.append(_s5_post_call(zc, gr["row0"], y_dir, s5_d[l][None, :], tm))
            if gi == 0:
                fin = sf.reshape(N_DIR, 2, b_, s5_g, s5_p).transpose(1, 2, 0, 3, 4)
                fin_re, fin_im = fin[0], fin[1]
        y_c = jnp.concatenate(yc_parts)
        new_re.append(fin_re)
        new_im.append(fin_im)

        wc = w_proj_c[l]
        merged = _merge_call(y_a, y_b, y_c, zg, w_proj_a[l].astype(BF16), w_proj_b[l].astype(BF16),
                             jnp.stack([wc[:, :dm], wc[:, dm:]]).astype(BF16), tm)
        x = _mm_res_call(merged, w_out[l].astype(BF16), x, g2, 1.0, tm)

        x = ffn(x, 1, sh3, sc3, g3)

    y = _final_norm_call(x, final_norm_g, tm)
    return (y[:n_p].reshape(bp, tp, dm), y[n_p:].reshape(bs, ts, dm),
            jnp.stack(new_a, axis=1), jnp.stack(new_b, axis=1),
            jnp.stack(new_re, axis=1), jnp.stack(new_im, axis=1))
```

```python
import functools
import math

import jax
import jax.numpy as jnp
import numpy as np
from jax import lax
from jax.experimental import pallas as pl
from jax.experimental.pallas import tpu as pltpu

F32 = jnp.float32
BF16 = jnp.bfloat16

LANES = 128
SUBLANES = 8
VMEM_LIMIT = 56 * 1024 * 1024

GRID_W = 64
SSD_CHUNK = 128
SSD_GROUPS = 2
N_DIR = 2
N_MOD = 9
EPS = 1e-6
RWKV_LN_EPS = 64e-5
RWKV_HEAD = 64
RWKV_SCAN_TB = 32
RWKV_SCAN_PARTS = 1
S5_TB = 64


def _cparams(sem):
    return pltpu.CompilerParams(dimension_semantics=sem, vmem_limit_bytes=VMEM_LIMIT)


def _pick(n, target, mult=LANES):
    best = None
    d = mult
    while d <= min(n, target):
        if n % d == 0:
            best = d
        d += mult
    return n if best is None else best


def _seq_rows(seq_len, nrows, row0, cap=2048):
    span = math.gcd(nrows, row0) if row0 else nrows
    assert span % seq_len == 0
    return _pick(span, max(cap, seq_len), seq_len)


def _round_up(n, m):
    return -(-n // m) * m


def _dot(a, b):
    return jnp.dot(a, b, preferred_element_type=F32)


def _split3(x):
    x1 = x.astype(BF16)
    r1 = x - x1.astype(F32)
    x2 = r1.astype(BF16)
    x3 = (r1 - x2.astype(F32)).astype(BF16)
    return x1, x2, x3


def _dot_exact_r(x, sel):
    return sum(_dot(p, sel) for p in _split3(x))


def _dot_exact_l(sel, x):
    return sum(_dot(sel, p) for p in _split3(x))


def _softplus(x):
    return jnp.maximum(x, 0.0) + jnp.log1p(jnp.exp(-jnp.abs(x)))


def _sigmoid(x):
    return 1.0 / (1.0 + jnp.exp(-x))


def _silu(x):
    return x * _sigmoid(x)


def _mod_kernel(c_ref, w_ref, b_ref, o_ref):
    c = c_ref[...]
    a = _silu(c).astype(BF16)
    o_ref[...] = _dot(a, w_ref[...].astype(BF16)) + b_ref[...]


def _mod_call(cond8, w, b):
    d, n = w.shape
    tn = _pick(n, 1024)
    return pl.pallas_call(
        _mod_kernel,
        out_shape=jax.ShapeDtypeStruct((cond8.shape[0], n), F32),
        grid=(n // tn,),
        in_specs=[pl.BlockSpec((cond8.shape[0], d), lambda j: (0, 0)),
                  pl.BlockSpec((d, tn), lambda j: (0, j)),
                  pl.BlockSpec((1, tn), lambda j: (0, j))],
        out_specs=pl.BlockSpec((cond8.shape[0], tn), lambda j: (0, j)),
        compiler_params=_cparams(("parallel",)),
        name="mod_proj",
    )(cond8, w, b.reshape(1, n))


def _norm_mod_kernel(x_ref, g_ref, sh_ref, sc_ref, o_ref):
    x = x_ref[...]
    y = x * lax.rsqrt(jnp.mean(x * x, axis=-1, keepdims=True) + EPS) * g_ref[...]
    o_ref[...] = (y * (1.0 + sc_ref[0]) + sh_ref[0]).astype(o_ref.dtype)


def _norm_mod_call(x, g, sh, sc, tm):
    m, d = x.shape
    return pl.pallas_call(
        _norm_mod_kernel,
        out_shape=jax.ShapeDtypeStruct((m, d), BF16),
        grid=(m // tm,),
        in_specs=[pl.BlockSpec((tm, d), lambda i: (i, 0)),
                  pl.BlockSpec((1, d), lambda i: (0, 0)),
                  pl.BlockSpec((1, 1, d), lambda i: (i, 0, 0)),
                  pl.BlockSpec((1, 1, d), lambda i: (i, 0, 0))],
        out_specs=pl.BlockSpec((tm, d), lambda i: (i, 0)),
        compiler_params=_cparams(("parallel",)),
        name="norm_mod",
    )(x, g.reshape(1, d), sh, sc)


def _final_norm_kernel(x_ref, g_ref, o_ref):
    x = x_ref[...]
    o_ref[...] = x * lax.rsqrt(jnp.mean(x * x, axis=-1, keepdims=True) + EPS) * g_ref[...]


def _final_norm_call(x, g, tm):
    m, d = x.shape
    return pl.pallas_call(
        _final_norm_kernel,
        out_shape=jax.ShapeDtypeStruct((m, d), F32),
        grid=(m // tm,),
        in_specs=[pl.BlockSpec((tm, d), lambda i: (i, 0)),
                  pl.BlockSpec((1, d), lambda i: (0, 0))],
        out_specs=pl.BlockSpec((tm, d), lambda i: (i, 0)),
        compiler_params=_cparams(("parallel",)),
        name="final_norm",
    )(x, g.reshape(1, d))


def _mm_kernel(a_ref, w_ref, o_ref):
    o_ref[...] = _dot(a_ref[...], w_ref[...]).astype(o_ref.dtype)


def _mm_call(a, w, tm, tn_target=1024, out_dtype=F32):
    m, k = a.shape
    n = w.shape[1]
    tn = _pick(n, tn_target)
    return pl.pallas_call(
        _mm_kernel,
        out_shape=jax.ShapeDtypeStruct((m, n), out_dtype),
        grid=(m // tm, n // tn),
        in_specs=[pl.BlockSpec((tm, k), lambda i, j: (i, 0)),
                  pl.BlockSpec((k, tn), lambda i, j: (0, j))],
        out_specs=pl.BlockSpec((tm, tn), lambda i, j: (i, j)),
        compiler_params=_cparams(("parallel", "parallel")),
        name="mm",
    )(a, w)


def _mm_swiglu_kernel(a_ref, w_ref, o_ref):
    a = a_ref[...]
    gate = _dot(a, w_ref[0])
    up = _dot(a, w_ref[1])
    o_ref[...] = (_silu(gate) * up).astype(o_ref.dtype)


def _mm_swiglu_call(a, w2, tm, tn_target=512):
    m, k = a.shape
    n = w2.shape[2]
    tn = _pick(n, tn_target)
    return pl.pallas_call(
        _mm_swiglu_kernel,
        out_shape=jax.ShapeDtypeStruct((m, n), BF16),
        grid=(m // tm, n // tn),
        in_specs=[pl.BlockSpec((tm, k), lambda i, j: (i, 0)),
                  pl.BlockSpec((2, k, tn), lambda i, j: (0, 0, j))],
        out_specs=pl.BlockSpec((tm, tn), lambda i, j: (i, j)),
        compiler_params=_cparams(("parallel", "parallel")),
        name="mm_swiglu",
    )(a, w2)


def _mm_res_kernel(a_ref, w_ref, x_ref, g_ref, o_ref, *, coef):
    o_ref[...] = x_ref[...] + (coef * g_ref[0]) * _dot(a_ref[...], w_ref[...])


def _mm_res_call(a, w, x, gate, coef, tm, tn_target=512):
    m, k = a.shape
    n = w.shape[1]
    tn = _pick(n, tn_target)
    return pl.pallas_call(
        functools.partial(_mm_res_kernel, coef=coef),
        out_shape=jax.ShapeDtypeStruct((m, n), F32),
        grid=(m // tm, n // tn),
        in_specs=[pl.BlockSpec((tm, k), lambda i, j: (i, 0)),
                  pl.BlockSpec((k, tn), lambda i, j: (0, j)),
                  pl.BlockSpec((tm, tn), lambda i, j: (i, j)),
                  pl.BlockSpec((1, 1, tn), lambda i, j: (i, 0, j))],
        out_specs=pl.BlockSpec((tm, tn), lambda i, j: (i, j)),
        compiler_params=_cparams(("parallel", "parallel")),
        name="mm_res",
    )(a, w, x, gate)


def _merge_kernel(ya_ref, yb_ref, yc_ref, ga_ref, gb_ref, gc_ref, wa_ref, wb_ref, wc_ref, o_ref):
    pa = _dot(ya_ref[...], wa_ref[...])
    pb = _dot(yb_ref[...], wb_ref[...])
    yc = yc_ref[...]
    val = _dot(yc, wc_ref[0])
    gate = _dot(yc, wc_ref[1])
    merged = (_sigmoid(ga_ref[...]) * pa + _sigmoid(gb_ref[...]) * pb
              + _sigmoid(gc_ref[...]) * (val * _sigmoid(gate)))
    o_ref[...] = merged.astype(o_ref.dtype)


def _merge_call(ya, yb, yc, zg, wa, wb, wc2, tm, tn_target=512):
    m, ka = ya.shape
    d = wa.shape[1]
    tn = _pick(d, tn_target)
    nj = d // tn
    return pl.pallas_call(
        _merge_kernel,
        out_shape=jax.ShapeDtypeStruct((m, d), BF16),
        grid=(m // tm, nj),
        in_specs=[pl.BlockSpec((tm, ka), lambda i, j: (i, 0)),
                  pl.BlockSpec((tm, yb.shape[1]), lambda i, j: (i, 0)),
                  pl.BlockSpec((tm, yc.shape[1]), lambda i, j: (i, 0)),
                  pl.BlockSpec((tm, tn), lambda i, j: (i, j)),
                  pl.BlockSpec((tm, tn), lambda i, j: (i, nj + j)),
                  pl.BlockSpec((tm, tn), lambda i, j: (i, 2 * nj + j)),
                  pl.BlockSpec((ka, tn), lambda i, j: (0, j)),
                  pl.BlockSpec((yb.shape[1], tn), lambda i, j: (0, j)),
                  pl.BlockSpec((2, yc.shape[1], tn), lambda i, j: (0, 0, j))],
        out_specs=pl.BlockSpec((tm, tn), lambda i, j: (i, j)),
        compiler_params=_cparams(("parallel", "parallel")),
        name="merge",
    )(ya, yb, yc, zg, zg, zg, wa, wb, wc2)


def _row_shift(x, off, t_idx, seq_len):
    rows = x.shape[0]
    rolled = pltpu.roll(x, (-off) % rows, axis=0)
    src = t_idx + off
    ok = jnp.logical_and(src >= 0, src < seq_len)
    return jnp.where(ok, rolled, 0.0)


def _centred_nb(x, t_idx, seq_len, grid_w):
    if grid_w is None:
        return 0.5 * (_row_shift(x, -1, t_idx, seq_len) + _row_shift(x, 1, t_idx, seq_len))
    col = t_idx % grid_w
    left = jnp.where(col >= 1, _row_shift(x, -1, t_idx, seq_len), 0.0)
    right = jnp.where(col < grid_w - 1, _row_shift(x, 1, t_idx, seq_len), 0.0)
    up = _row_shift(x, -grid_w, t_idx, seq_len)
    down = _row_shift(x, grid_w, t_idx, seq_len)
    return 0.25 * (up + down + left + right)


def _rwkv_prep_kernel(zr_ref, zk_ref, zv_ref, zl_ref,
                      mur_ref, muk_ref, muv_ref, mul_ref,
                      kk_w_ref, ka_w_ref, rk_w_ref, w0_ref, a0_ref,
                      w2_ref, a2_ref, g2_ref, sel_ref, selt_ref,
                      rv_o, wb_o, kk_o, bonus_o, g_o,
                      *, seq_len, grid_w, ranks):
    nseq = rv_o.shape[0]

    def emit(write, a, b):
        lane = lax.broadcasted_iota(jnp.int32, a.shape, 1)
        first = lane < RWKV_HEAD
        head0 = jnp.where(first, a, pltpu.roll(b, RWKV_HEAD, axis=1))
        head1 = jnp.where(first, pltpu.roll(a, RWKV_HEAD, axis=1), b)
        for q in range(nseq):
            write(q, 0, head0[q * seq_len:(q + 1) * seq_len])
            write(q, 1, head1[q * seq_len:(q + 1) * seq_len])

    def shifted(ref, mu_ref):
        x = ref[...]
        t_idx = lax.broadcasted_iota(jnp.int32, x.shape, 0) % seq_len
        return x + mu_ref[...] * (_centred_nb(x, t_idx, seq_len, grid_w) - x)

    r = shifted(zr_ref, mur_ref)
    k = shifted(zk_ref, muk_ref)
    v = shifted(zv_ref, muv_ref)
    lo = shifted(zl_ref, mul_ref)
    rd, ra, rg = ranks
    wl = lo[:, :rd]
    al = lo[:, rd:rd + ra]
    gl = lo[:, rd + ra:rd + ra + rg]
    sel = sel_ref[...]
    selt = selt_ref[...]

    def head_sum(x):
        return _dot_exact_r(_dot_exact_r(x, sel), selt)

    kk = k * kk_w_ref[...]
    kk = kk * lax.rsqrt(head_sum(kk * kk) + 1e-12)
    def write_rv(q, hh, val):
        rv_o[q, hh] = val

    emit(write_rv, r, v)
    tw = jnp.tanh(wl).astype(BF16)
    alb = al.astype(BF16)
    for d in range(N_DIR):
        w_log = -_softplus(-(w0_ref[d] + _dot(tw, w2_ref[d]))) - 0.5
        a_d = _sigmoid(a0_ref[d] + _dot(alb, a2_ref[d]))

        def write_wb(q, hh, val, d=d):
            wb_o[d, q, hh] = val

        def write_kk(q, hh, val, d=d):
            kk_o[d, q, hh] = val

        emit(write_wb, jnp.exp(-jnp.exp(w_log)), kk * a_d)
        emit(write_kk, k * (1.0 + (a_d - 1.0) * ka_w_ref[...]), kk)
    bonus_o[...] = head_sum(r * k * rk_w_ref[...]) * v
    g_o[...] = _dot(_sigmoid(gl).astype(BF16), g2_ref[...])


def _rwkv_prep_call(zr, zk, zv, zl, row0, nrows, seq_len, grid_w, pw):
    w = zr.shape[1]
    lw = sum(pw["ranks"])
    cb = LANES
    rb = _seq_rows(seq_len, nrows, row0)
    assert w % cb == 0
    r0 = row0 // rb
    main = pl.BlockSpec((rb, cb), lambda i, j: (r0 + i, j))
    lspec = pl.BlockSpec((rb, lw), lambda i, j: (r0 + i, 0))
    colp = pl.BlockSpec((1, cb), lambda i, j: (0, j))
    dirp = pl.BlockSpec((N_DIR, 1, cb), lambda i, j: (0, 0, j))
    ospec = pl.BlockSpec((rb, cb), lambda i, j: (i, j))
    rd, ra, rg = pw["ranks"]
    nsel = pw["sel"].shape[1]
    assert cb == 2 * RWKV_HEAD
    nseq = rb // seq_len
    bsz = nrows // seq_len
    heads = w // RWKV_HEAD
    pk = jax.ShapeDtypeStruct((bsz, heads, seq_len, cb), F32)
    pkd = jax.ShapeDtypeStruct((N_DIR, bsz, heads, seq_len, cb), F32)
    pspec = pl.BlockSpec((nseq, 2, seq_len, cb), lambda i, j: (i, j, 0, 0))
    pdspec = pl.BlockSpec((N_DIR, nseq, 2, seq_len, cb), lambda i, j: (0, i, j, 0, 0))
    return pl.pallas_call(
        functools.partial(_rwkv_prep_kernel, seq_len=seq_len, grid_w=grid_w, ranks=pw["ranks"]),
        out_shape=[pk, pkd, pkd, jax.ShapeDtypeStruct((nrows, w), F32), jax.ShapeDtypeStruct((nrows, w), F32)],
        grid=(nrows // rb, w // cb),
        in_specs=[main, main, main, lspec,
                  colp, colp, colp, pl.BlockSpec((1, lw), lambda i, j: (0, 0)),
                  colp, colp, colp, dirp, dirp,
                  pl.BlockSpec((N_DIR, rd, cb), lambda i, j: (0, 0, j)),
                  pl.BlockSpec((N_DIR, ra, cb), lambda i, j: (0, 0, j)),
                  pl.BlockSpec((rg, cb), lambda i, j: (0, j)),
                  pl.BlockSpec((cb, nsel), lambda i, j: (0, 0)),
                  pl.BlockSpec((nsel, cb), lambda i, j: (0, 0))],
        out_specs=[pspec, pdspec, pdspec, ospec, ospec],
        compiler_params=_cparams(("parallel", "parallel")),
        name="rwkv_prep",
    )(zr, zk, zv, zl, pw["mu_r"], pw["mu_k"], pw["mu_v"], pw["mu_l"],
      pw["k_k"], pw["k_a"], pw["r_k"], pw["w0"], pw["a0"],
      pw["w2"], pw["a2"], pw["g2"], pw["sel"], pw["selt"])


def _load_time_chunk(ref, t0):
    idx = (0,) * (len(ref.shape) - 4) + (slice(None), slice(None), pl.ds(t0, SUBLANES), slice(None))
    x = ref[idx]
    return x.reshape(x.shape[0] * x.shape[1], SUBLANES, x.shape[-1])


def _store_time_rows(ref, s, val):
    steps, width = ref.shape[-2], ref.shape[-1]
    chains = math.prod(ref.shape[:-2])
    ref.reshape(chains * steps, width)[pl.ds(s, chains, stride=steps), :] = val


def _rwkv_scan_kernel(rva_ref, rvb_ref, wba_ref, wbb_ref, kka_ref, kkb_ref, s0_ref,
                      oa_ref, ob_ref, sf_ref, s_ref, out_ref, *ops_refs):
    i = pl.program_id(1)
    n = s_ref.shape[0]
    nj = n // SUBLANES
    steps = out_ref.shape[0]
    half = LANES // 2
    r_off, v_off, w_off, b_off, kd_off, kk_off = (q * n for q in range(6))

    @pl.when(i == 0)
    def _():
        s_ref[...] = s0_ref[...]

    srcs = ((rva_ref, rvb_ref), (wba_ref, wbb_ref), (kka_ref, kkb_ref))

    nparts = len(ops_refs)
    plen = steps // nparts

    def relayout(part, c):
        t0 = pl.multiple_of(part * plen + c * SUBLANES, SUBLANES)
        tb0 = pl.multiple_of(steps - SUBLANES - (part * plen + c * SUBLANES), SUBLANES)
        for p, (a_ref, b_ref) in enumerate(srcs):
            xa = jnp.swapaxes(_load_time_chunk(a_ref, t0), 0, 1)
            xb = jnp.swapaxes(_load_time_chunk(b_ref, tb0), 0, 1)
            for q in range(SUBLANES):
                x = jnp.concatenate([xa[q], xb[SUBLANES - 1 - q]], axis=0)
                ops_refs[part][c * SUBLANES + q, pl.ds(p * LANES, LANES), :] = x.T

    def unlay(t):
        tb_ = steps - 1 - t
        ot = out_ref[t].T
        _store_time_rows(oa_ref, t, ot[:half])
        _store_time_rows(ob_ref, tb_, ot[half:])

    out_ref[0] = jnp.zeros(out_ref.shape[1:], F32)

    def relayout_chunk(c, carry):
        relayout(0, c)
        return carry

    lax.fori_loop(0, plen // SUBLANES, relayout_chunk, 0)

    def state(k, j):
        return s_ref[k, pl.ds(j * SUBLANES, SUBLANES), :]

    for part in range(nparts):
        ops_ref = ops_refs[part]

        def row(s, r, ops_ref=ops_ref):
            return jnp.broadcast_to(ops_ref[s, pl.ds(r, 1), :], (SUBLANES, LANES))

        acc0 = [None] * nj
        for k in range(n):
            kkb = row(0, kk_off + k)
            for j in range(nj):
                p = state(k, j) * kkb
                acc0[j] = p if acc0[j] is None else acc0[j] + p

        def step(s, acc, part=part, ops_ref=ops_ref, row=row):
            t = part * plen + s
            unlay(jnp.maximum(t - 1, 0))
            s_next = jnp.minimum(s + 1, plen - 1)
            vv = [ops_ref[s, pl.ds(v_off + j * SUBLANES, SUBLANES), :] for j in range(nj)]
            out = [None] * nj
            acc_next = [None] * nj
            for k in range(n):
                wb = row(s, w_off + k)
                bb = row(s, b_off + k)
                kdb = row(s, kd_off + k)
                rb = row(s, r_off + k)
                kkn = row(s_next, kk_off + k)
                for j in range(nj):
                    s_new = state(k, j) * wb - acc[j] * bb + vv[j] * kdb
                    s_ref[k, pl.ds(j * SUBLANES, SUBLANES), :] = s_new
                    q = s_new * rb
                    out[j] = q if out[j] is None else out[j] + q
                    p = s_new * kkn
                    acc_next[j] = p if acc_next[j] is None else acc_next[j] + p
            for j in range(nj):
                out_ref[t, pl.ds(j * SUBLANES, SUBLANES), :] = out[j]
            return tuple(acc_next)

        lax.fori_loop(0, plen, step, tuple(acc0))
    unlay(steps - 1)

    @pl.when(i == pl.num_programs(1) - 1)
    def _():
        sf_ref[...] = s_ref[...]


def _rwkv_scan_call(rv, wb, kk, s0):
    bsz, heads, t, _ = rv.shape
    n = RWKV_HEAD
    bpb = (LANES // 2) // heads
    assert bpb * heads * 2 == LANES and bsz % bpb == 0
    nblk = bsz // bpb
    tb = min(RWKV_SCAN_TB, t)
    nparts = RWKV_SCAN_PARTS
    assert t % tb == 0 and tb % nparts == 0
    nt = t // tb
    fwd4 =pl.BlockSpec((bpb, heads, tb, LANES), lambda g, i: (g, 0, i, 0))
    bwd4 = pl.BlockSpec((bpb, heads, tb, LANES), lambda g, i: (g, 0, nt - 1 - i, 0))
    fwd5 = pl.BlockSpec((1, bpb, heads, tb, LANES), lambda g, i: (0, g, 0, i, 0))
    bwd5 = pl.BlockSpec((1, bpb, heads, tb, LANES), lambda g, i: (1, g, 0, nt - 1 - i, 0))
    st = pl.BlockSpec((n, n, LANES), lambda g, i: (0, 0, g))
    o_sds = jax.ShapeDtypeStruct((bsz, heads, t, n), F32)
    return pl.pallas_call(
        _rwkv_scan_kernel,
        out_shape=[o_sds, o_sds, jax.ShapeDtypeStruct((n, n, nblk * LANES), F32)],
        grid=(nblk, nt),
        in_specs=[fwd4, bwd4, fwd5, bwd5, fwd5, bwd5, st],
        out_specs=[pl.BlockSpec((bpb, heads, tb, n), lambda g, i: (g, 0, i, 0)),
                   pl.BlockSpec((bpb, heads, tb, n), lambda g, i: (g, 0, nt - 1 - i, 0)),
                   st],
        scratch_shapes=[pltpu.VMEM((n, n, LANES), F32),
                        pltpu.VMEM((tb, n, LANES), F32)]
        + [pltpu.VMEM((tb // nparts, 6 * n, LANES), F32)] * nparts,
        compiler_params=_cparams(("parallel", "arbitrary")),
        name="rwkv_scan",
    )(rv, rv, wb, wb, kk, kk, s0)


def _rwkv_post_kernel(of_ref, ob_ref, bonus_ref, g_ref, lng_ref, lnb_ref, sel_ref, selt_ref, o_ref,
                      *, head):
    sel = sel_ref[...]
    selt = selt_ref[...]

    def head_sum(x):
        return _dot_exact_r(_dot_exact_r(x, sel), selt)

    def rows(ref):
        parts = [jnp.concatenate([ref[q, 0], ref[q, 1]], axis=1) for q in range(ref.shape[0])]
        return parts[0] if len(parts) == 1 else jnp.concatenate(parts, axis=0)

    o = rows(of_ref) + rows(ob_ref)
    mu = head_sum(o) * (1.0 / head)
    dlt = o - mu
    var = head_sum(dlt * dlt) * (1.0 / head)
    on = dlt * lax.rsqrt(var + RWKV_LN_EPS) * lng_ref[...] + lnb_ref[...]
    o_ref[...] = ((on + bonus_ref[...]) * g_ref[...]).astype(o_ref.dtype)


def _rwkv_post_call(of, ob, bonus, g, pw):
    bsz, heads, t, n = of.shape
    m, w = bonus.shape
    cb = LANES
    nsel = pw["sel"].shape[1]
    tm = _seq_rows(t, m, 0)
    nseq = tm // t
    main = pl.BlockSpec((tm, cb), lambda i, j: (i, j))
    colp = pl.BlockSpec((1, cb), lambda i, j: (0, j))
    ospec = pl.BlockSpec((nseq, 2, t, n), lambda i, j: (i, j, 0, 0))
    return pl.pallas_call(
        functools.partial(_rwkv_post_kernel, head=RWKV_HEAD),
        out_shape=jax.ShapeDtypeStruct((m, w), BF16),
        grid=(m // tm, w // cb),
        in_specs=[ospec, ospec, main, main, colp, colp,
                  pl.BlockSpec((cb, nsel), lambda i, j: (0, 0)),
                  pl.BlockSpec((nsel, cb), lambda i, j: (0, 0))],
        out_specs=main,
        compiler_params=_cparams(("parallel", "parallel")),
        name="rwkv_post",
    )(of, ob, bonus, g, pw["ln_g"], pw["ln_b"], pw["sel"], pw["selt"])


def _conv_silu_kernel(x_ref, w_ref, b_ref, o_ref, *, seq_len):
    x = x_ref[...]
    rows = x.shape[0]
    kw = w_ref.shape[0]
    t_idx = lax.broadcasted_iota(jnp.int32, x.shape, 0) % seq_len
    y = b_ref[...] + jnp.zeros_like(x)
    for j in range(kw):
        off = j - kw // 2
        xs = x if off == 0 else _row_shift(x, off, t_idx, seq_len)
        y = y + w_ref[pl.ds(j, 1), :] * xs
    o_ref[...] = _silu(y)


def _conv_silu_call(x, row0, nrows, seq_len, w, b, col_blk=0):
    kw, c = w.shape
    cb = _pick(c, 256)
    rb = _seq_rows(seq_len, nrows, row0)
    r0 = row0 // rb
    j0 = col_blk * (c // cb)
    return pl.pallas_call(
        functools.partial(_conv_silu_kernel, seq_len=seq_len),
        out_shape=jax.ShapeDtypeStruct((nrows, c), F32),
        grid=(nrows // rb, c // cb),
        in_specs=[pl.BlockSpec((rb, cb), lambda i, j: (r0 + i, j0 + j)),
                  pl.BlockSpec((kw, cb), lambda i, j: (0, j)),
                  pl.BlockSpec((1, cb), lambda i, j: (0, j))],
        out_specs=pl.BlockSpec((rb, cb), lambda i, j: (i, j)),
        compiler_params=_cparams(("parallel", "parallel")),
        name="ssd_conv",
    )(x, w, b.reshape(1, c))


def _ssd_scan_kernel(x_ref, b_ref, c_ref, dt_ref, dtt_ref, bias_ref, alog_ref, biast_ref, alogt_ref,
                     tri_ref, trit_ref, e_ref, s0_ref, y_ref, sf_ref, st_ref,
                     *, heads, hdim, nstate, groups, has_init):
    d = pl.program_id(0)
    c = pl.program_id(2)
    nc = pl.num_programs(2)
    hpg = heads // groups
    gw = hpg * hdim

    @pl.when(c == 0)
    def _():
        if has_init:
            for g in range(groups):
                st_ref[pl.ds(g * nstate, nstate), :] = s0_ref[0, 0, pl.ds(g * gw, gw), :].T
        else:
            st_ref[...] = jnp.zeros_like(st_ref)

    tri = tri_ref[0]
    trit = trit_ref[0]
    lch = tri.shape[0]
    e = e_ref[...]
    dtp = _softplus(dt_ref[...] + bias_ref[0])
    a = -jnp.exp(alog_ref[0])
    da = dtp * a
    cum = _dot_exact_l(tri, da)
    dat = _softplus(dtt_ref[...] + biast_ref[0]) * (-jnp.exp(alogt_ref[0]))
    cumt = _dot_exact_r(dat, trit)
    total = jnp.sum(da, axis=0, keepdims=True)
    tot8 = jnp.broadcast_to(total, (SUBLANES, LANES))
    dt_full = _dot_exact_r(dtp, e)
    din_full = _dot_exact_r(jnp.exp(cum), e)
    dst_full = _dot_exact_r(jnp.exp(total - cum), e)
    tot_full = _dot_exact_r(jnp.exp(tot8), e)[0:1, :]
    x = x_ref[...]
    xdt = x * dt_full
    xdec = (xdt * dst_full).astype(BF16)
    xdt_b = xdt.astype(BF16)
    bm = b_ref[...]
    cm = c_ref[...]
    visible = tri > 0
    for g in range(groups):
        bg = bm[:, g * nstate:(g + 1) * nstate]
        cg = cm[:, g * nstate:(g + 1) * nstate].astype(BF16)
        bgb = bg.astype(BF16)
        cb = lax.dot_general(cg, bgb, (((1,), (1,)), ((), ())), preferred_element_type=F32)
        st_g = st_ref[pl.ds(g * nstate, nstate), :]
        y_off = _dot(cg, st_g.astype(BF16)) * din_full[:, g * gw:(g + 1) * gw]
        for hh in range(hpg):
            h = g * hpg + hh
            seg = cum[:, h:h + 1] - cumt[h:h + 1, :]
            lmat = jnp.exp(jnp.where(visible, seg, -jnp.inf))
            gmat = (cb * lmat).astype(BF16)
            yd = _dot(gmat, xdt_b[:, h * hdim:(h + 1) * hdim])
            y_ref[0, :, pl.ds(h * hdim, hdim)] = yd + y_off[:, hh * hdim:(hh + 1) * hdim]
        upd = _dot(bg.T.astype(BF16), xdec[:, g * gw:(g + 1) * gw])
        st_ref[pl.ds(g * nstate, nstate), :] = st_g * tot_full[:, g * gw:(g + 1) * gw] + upd

    @pl.when(c == nc - 1)
    def _():
        for g in range(groups):
            sf_ref[0, 0, pl.ds(g * gw, gw), :] = st_ref[pl.ds(g * nstate, nstate), :].T


def _ssd_scan_call(xs, bm, cm, zdt, dt_blk, dtt, row0, bsz, seq_len, pw, s0):
    hp = xs.shape[1]
    gn = bm.shape[1]
    heads = pw["heads"]
    hdim = hp // heads
    nstate = gn // SSD_GROUPS
    lch = min(SSD_CHUNK, seq_len)
    nc = seq_len // lch
    assert seq_len % lch == 0 and row0 % lch == 0
    c0 = row0 // lch
    has_init = s0 is not None
    if s0 is None:
        s0 = jnp.zeros((1, 1, hp, nstate), F32)

    def cidx(d, b, c):
        return b * nc + c + d * (nc - 1 - 2 * c)

    row = lambda d, b, c: (cidx(d, b, c), 0)
    s0_map = (lambda d, b, c: (b, d, 0, 0)) if has_init else (lambda d, b, c: (0, 0, 0, 0))
    dirp = pl.BlockSpec((1, 1, LANES), lambda d, b, c: (d, 0, 0))
    dirt = pl.BlockSpec((1, heads, 1), lambda d, b, c: (d, 0, 0))
    return pl.pallas_call(
        functools.partial(_ssd_scan_kernel, heads=heads, hdim=hdim, nstate=nstate,
                          groups=SSD_GROUPS, has_init=has_init),
        out_shape=[jax.ShapeDtypeStruct((N_DIR, bsz * seq_len, hp), F32),
                   jax.ShapeDtypeStruct((bsz, N_DIR, hp, nstate), F32)],
        grid=(N_DIR, bsz, nc),
        in_specs=[pl.BlockSpec((lch, hp), row),
                  pl.BlockSpec((lch, gn), row),
                  pl.BlockSpec((lch, gn), row),
                  pl.BlockSpec((lch, LANES), lambda d, b, c: (c0 + cidx(d, b, c), dt_blk)),
                  pl.BlockSpec((heads, lch), lambda d, b, c: (0, c0 + cidx(d, b, c))),
                  dirp, dirp, dirt, dirt,
                  pl.BlockSpec((1, lch, lch), lambda d, b, c: (d, 0, 0)),
                  pl.BlockSpec((1, lch, lch), lambda d, b, c: (d, 0, 0)),
                  pl.BlockSpec((LANES, hp), lambda d, b, c: (0, 0)),
                  pl.BlockSpec((1, 1, hp, nstate), s0_map)],
        out_specs=[pl.BlockSpec((1, lch, hp), lambda d, b, c: (d, cidx(d, b, c), 0)),
                   pl.BlockSpec((1, 1, hp, nstate), lambda d, b, c: (b, d, 0, 0))],
        scratch_shapes=[pltpu.VMEM((gn, hp // SSD_GROUPS), F32)],
        compiler_params=_cparams(("arbitrary", "arbitrary", "arbitrary")),
        name="ssd_scan",
    )(xs, bm, cm, zdt, dtt, pw["bias"], pw["alog"], pw["bias_t"], pw["alog_t"],
      pw["tri"][:, :lch, :lch], pw["trit"][:, :lch, :lch], pw["expand"], s0)


def _ssd_post_kernel(x_ref, yf_ref, yb_ref, z_ref, d_ref, g_ref, o_ref):
    y = (d_ref[...] * x_ref[...] + yf_ref[0] + yb_ref[0]) * _silu(z_ref[...])
    y = y * lax.rsqrt(jnp.mean(y * y, axis=-1, keepdims=True) + EPS) * g_ref[...]
    o_ref[...] = y.astype(o_ref.dtype)


def _ssd_post_call(xs, ydir, zz, row0, d_full, g, tm):
    n, hp = xs.shape
    r0 = row0 // tm
    return pl.pallas_call(
        _ssd_post_kernel,
        out_shape=jax.ShapeDtypeStruct((n, hp), BF16),
        grid=(n // tm,),
        in_specs=[pl.BlockSpec((tm, hp), lambda i: (i, 0)),
                  pl.BlockSpec((1, tm, hp), lambda i: (0, i, 0)),
                  pl.BlockSpec((1, tm, hp), lambda i: (1, i, 0)),
                  pl.BlockSpec((tm, hp), lambda i: (r0 + i, 0)),
                  pl.BlockSpec((1, hp), lambda i: (0, 0)),
                  pl.BlockSpec((1, hp), lambda i: (0, 0))],
        out_specs=pl.BlockSpec((tm, hp), lambda i: (i, 0)),
        compiler_params=_cparams(("parallel",)),
        name="ssd_post",
    )(xs, ydir, ydir, zz, d_full, g)


def _s5_kernel(u_ref, bmat_ref, cmat_ref, lam_ref, s0_ref, y_ref, sf_ref,
               st_ref, buf_ref, *, slab_in, slab_state):
    d = pl.program_id(0)
    tb = pl.program_id(2)
    ntb = pl.num_programs(2)
    nb, steps = u_ref.shape[0], u_ref.shape[1]
    nslab = u_ref.shape[2] // slab_in
    spl = buf_ref.shape[0]
    ncol = slab_state // LANES
    assert slab_in == LANES

    @pl.when(tb == 0)
    def _():
        st_ref[...] = s0_ref[0]

    for s_base in range(0, nslab, spl):
        lam = []
        init = []
        for q in range(spl):
            s = s_base + q
            u = jnp.swapaxes(u_ref[:, :, pl.ds(s * slab_in, slab_in)], 0, 1)
            bu = _dot(u.reshape(steps * nb, slab_in).astype(BF16), bmat_ref[0, s])
            for c in range(2 * ncol):
                buf_ref[q, c] = bu[:, c * LANES:(c + 1) * LANES]
            for c in range(ncol):
                lanes = pl.ds(s * slab_state + c * LANES, LANES)
                lam.append((jnp.broadcast_to(lam_ref[0, 0, :, lanes], (nb, LANES)),
                            jnp.broadcast_to(lam_ref[0, 1, :, lanes], (nb, LANES))))
                init.append(st_ref[0, :, lanes])
                init.append(st_ref[1, :, lanes])

        def step(i, carry):
            te = i + d * (steps - 1 - 2 * i)
            rows = pl.ds(pl.multiple_of(te * nb, nb), nb)
            new = []
            for q in range(spl):
                for c in range(ncol):
                    lr, li = lam[q * ncol + c]
                    s_re = carry[2 * (q * ncol + c)]
                    s_im = carry[2 * (q * ncol + c) + 1]
                    n_re = lr * s_re - li * s_im + buf_ref[q, c, rows, :]
                    n_im = lr * s_im + li * s_re + buf_ref[q, ncol + c, rows, :]
                    buf_ref[q, c, rows, :] = n_re
                    buf_ref[q, ncol + c, rows, :] = n_im
                    new += [n_re, n_im]
            return tuple(new)

        fin = lax.fori_loop(0, steps, step, tuple(init), unroll=4)
        for q in range(spl):
            s = s_base + q
            for c in range(ncol):
                lanes = pl.ds(s * slab_state + c * LANES, LANES)
                st_ref[0, :, lanes] = fin[2 * (q * ncol + c)]
                st_ref[1, :, lanes] = fin[2 * (q * ncol + c) + 1]
            states = jnp.concatenate([buf_ref[q, c] for c in range(2 * ncol)], axis=1)
            y = _dot(states.astype(BF16), cmat_ref[0, s])
            y_ref[0, :, :, pl.ds(s * slab_in, slab_in)] = jnp.swapaxes(y.reshape(steps, nb, slab_in), 0, 1)

    @pl.when(tb == ntb - 1)
    def _():
        sf_ref[0] = st_ref[...]


def _s5_call(u3, seq0, bsz, pw, s0):
    _, t, w = u3.shape
    nb = min(SUBLANES, bsz)
    tb = min(S5_TB, t)
    assert t % tb == 0 and bsz % nb == 0 and seq0 % nb == 0
    ntb = t // tb
    b0 = seq0 // nb
    gp = pw["lam"].shape[-1]
    slab_in = pw["slab_in"]
    slab_state = pw["slab_state"]
    nslab = w // slab_in
    spl = 2 if nslab % 2 == 0 else 1

    def tidx(d, i):
        return i + d * (ntb - 1 - 2 * i)

    return pl.pallas_call(
        functools.partial(_s5_kernel, slab_in=slab_in, slab_state=slab_state),
        out_shape=[jax.ShapeDtypeStruct((N_DIR, bsz, t, w), F32),
                   jax.ShapeDtypeStruct((N_DIR, 2, bsz, gp), F32)],
        grid=(N_DIR, bsz // nb, ntb),
        in_specs=[pl.BlockSpec((nb, tb, w), lambda d, b, i: (b0 + b, tidx(d, i), 0)),
                  pl.BlockSpec((1, nslab, slab_in, 2 * slab_state), lambda d, b, i: (d, 0, 0, 0)),
                  pl.BlockSpec((1, nslab, 2 * slab_state, slab_in), lambda d, b, i: (d, 0, 0, 0)),
                  pl.BlockSpec((1, 2, 1, gp), lambda d, b, i: (d, 0, 0, 0)),
                  pl.BlockSpec((1, 2, nb, gp), lambda d, b, i: (d, 0, b, 0))],
        out_specs=[pl.BlockSpec((1, nb, tb, w), lambda d, b, i: (d, b, tidx(d, i), 0)),
                   pl.BlockSpec((1, 2, nb, gp), lambda d, b, i: (d, 0, b, 0))],
        scratch_shapes=[pltpu.VMEM((2, nb, gp), F32),
                        pltpu.VMEM((spl, 2 * slab_state // LANES, nb * tb, LANES), F32)],
        compiler_params=_cparams(("arbitrary", "arbitrary", "arbitrary")),
        name="s5_scan",
    )(u3, pw["bmat"], pw["cmat"], pw["lam"], s0)


def _s5_post_kernel(u_ref, yf_ref, yb_ref, d_ref, o_ref):
    y = d_ref[...] * u_ref[...] + yf_ref[0] + yb_ref[0]
    o_ref[...] = jax.nn.gelu(y).astype(o_ref.dtype)


def _s5_post_call(u, row0, ydir, d_full, tm):
    _, m, w = ydir.shape
    r0 = row0 // tm
    return pl.pallas_call(
        _s5_post_kernel,
        out_shape=jax.ShapeDtypeStruct((m, w), BF16),
        grid=(m // tm,),
        in_specs=[pl.BlockSpec((tm, w), lambda i: (r0 + i, 0)),
                  pl.BlockSpec((1, tm, w), lambda i: (0, i, 0)),
                  pl.BlockSpec((1, tm, w), lambda i: (1, i, 0)),
                  pl.BlockSpec((1, w), lambda i: (0, 0))],
        out_specs=pl.BlockSpec((tm, w), lambda i: (i, 0)),
        compiler_params=_cparams(("parallel",)),
        name="s5_post",
    )(u, ydir, ydir, d_full)


def _head_selectors(cb, head):
    nsel = LANES
    col = np.arange(cb)[:, None] // head
    sel = (col == np.arange(nsel)[None, :]).astype(np.float32)
    return jnp.asarray(sel, BF16), jnp.asarray(sel.T, BF16)


def _s5_weights(p, l, slab_groups):
    g, pst = p["s5_lambda_re"].shape[2:]
    cg = p["s5_b_re"].shape[-1]
    nslab = g // slab_groups
    eye = jnp.eye(slab_groups, dtype=F32)
    bmats, cmats, lams = [], [], []
    for d in range(N_DIR):
        lam_re, lam_im = p["s5_lambda_re"][l, d], p["s5_lambda_im"][l, d]
        delta = jnp.exp(p["s5_log_dt"][l, d])[:, None]
        mag = jnp.exp(lam_re * delta)
        lb_re, lb_im = mag * jnp.cos(lam_im * delta), mag * jnp.sin(lam_im * delta)
        den = lam_re * lam_re + lam_im * lam_im
        q_re = ((lb_re - 1.0) * lam_re + lb_im * lam_im) / den
        q_im = (lb_im * lam_re - (lb_re - 1.0) * lam_im) / den
        b_re, b_im = p["s5_b_re"][l, d], p["s5_b_im"][l, d]
        bb_re = q_re[..., None] * b_re - q_im[..., None] * b_im
        bb_im = q_re[..., None] * b_im + q_im[..., None] * b_re

        def in_blocks(bb):
            x = bb.reshape(nslab, slab_groups, pst, cg)
            return jnp.einsum("sgpc,gh->sgchp", x, eye).reshape(nslab, slab_groups * cg, slab_groups * pst)

        def out_blocks(cc):
            x = cc.reshape(nslab, slab_groups, cg, pst)
            return jnp.einsum("sgcp,gh->sgphc", x, eye).reshape(nslab, slab_groups * pst, slab_groups * cg)

        bmats.append(jnp.concatenate([in_blocks(bb_re), in_blocks(bb_im)], axis=-1))
        cmats.append(jnp.concatenate([out_blocks(p["s5_c_re"][l, d]), -out_blocks(p["s5_c_im"][l, d])], axis=-2))
        lams.append(jnp.stack([lb_re.reshape(1, g * pst), lb_im.reshape(1, g * pst)]))
    return dict(bmat=jnp.stack(bmats).astype(BF16), cmat=jnp.stack(cmats).astype(BF16),
                lam=jnp.stack(lams), slab_in=slab_groups * cg, slab_state=slab_groups * pst)


def _pad_cols(w, n):
    return jnp.pad(w, ((0, 0), (0, n - w.shape[1])))


def kernel(x_prompt, x_sample, state_rwkv, state_ssd, state_s5_re, state_s5_im, c, c_ctx, w_mod, b_mod, norm_g, ffn_w_in, ffn_w_out, w_in, rwkv_mu, rwkv_w0, rwkv_w2, rwkv_a0, rwkv_a2, rwkv_g2, rwkv_k_k, rwkv_k_a, rwkv_r_k, rwkv_ln_g, rwkv_ln_b, w_proj_a, ssd_conv_w, ssd_conv_b, ssd_dt_bias, ssd_a_log, ssd_d, ssd_norm_g, w_proj_b, s5_lambda_re, s5_lambda_im, s5_log_dt, s5_b_re, s5_b_im, s5_c_re, s5_c_im, s5_d, w_proj_c, w_out, final_norm_g):
    p = dict(s5_lambda_re=s5_lambda_re, s5_lambda_im=s5_lambda_im, s5_log_dt=s5_log_dt,
             s5_b_re=s5_b_re, s5_b_im=s5_b_im, s5_c_re=s5_c_re, s5_c_im=s5_c_im)
    bp, tp, dm = x_prompt.shape
    bs, ts, _ = x_sample.shape
    n_p, n_s = bp * tp, bs * ts
    m = n_p + n_s
    depth = w_mod.shape[0]
    d_ff = ffn_w_out.shape[2]
    ffp = _round_up(d_ff, 512)
    rw = rwkv_k_k.shape[1]
    rh = rw // RWKV_HEAD
    rd, ra, rg = rwkv_w2.shape[2], rwkv_a2.shape[2], rwkv_g2.shape[1]
    lora = rd + ra + rg
    sh = ssd_d.shape[1]
    sw = ssd_norm_g.shape[1]
    xbc_w = ssd_conv_w.shape[2]
    gn = (xbc_w - sw) // 2
    cw = s5_d.shape[1]
    s5_g, s5_p = s5_lambda_re.shape[2:]
    s5_cg = cw // s5_g
    slab_groups = max(1, min(s5_g, LANES // s5_cg))

    tm = _pick(math.gcd(n_p, ts), 1024, SUBLANES)
    tm_wide = _pick(m, 2 * tm, tm)
    n_tiles = m // tm
    tile_cond = np.array([0 if i * tm < n_p else 1 + (i * tm - n_p) // ts for i in range(n_tiles)])

    x = jnp.concatenate([x_prompt.reshape(n_p, dm), x_sample.reshape(n_s, dm)], axis=0)
    ncond = 1 + bs
    cond = jnp.concatenate([c_ctx[None, :], c], axis=0)
    cond8 = jnp.pad(cond, ((0, _round_up(ncond, SUBLANES) - ncond), (0, 0)))

    groups = [dict(b=bp, t=tp, row0=0, n=n_p, grid_w=None),
              dict(b=bs, t=ts, row0=n_p, n=n_s, grid_w=GRID_W)]

    sel, selt = _head_selectors(LANES, RWKV_HEAD)
    heads_per_blk = LANES // RWKV_HEAD
    tri_f = np.tril(np.ones((SSD_CHUNK, SSD_CHUNK), np.float32))
    tri = jnp.asarray(np.stack([tri_f, tri_f.T]), BF16)
    trit = jnp.asarray(np.stack([tri_f.T, tri_f]), BF16)
    expand = jnp.asarray((np.arange(LANES)[:, None] == (np.arange(sw)[None, :] // (sw // sh))).astype(np.float32), BF16)

    new_a, new_b, new_re, new_im = [], [], [], []
    for l in range(depth):
        mod = _mod_call(cond8, w_mod[l], b_mod[l])
        mods = mod.reshape(-1, N_MOD, dm)[tile_cond]
        sh1, sc1, g1, sh2, sc2, g2, sh3, sc3, g3 = [mods[:, i:i + 1, :] for i in range(N_MOD)]

        def ffn(x, idx, shv, scv, gv):
            wi = ffn_w_in[l, idx]
            w2 = jnp.stack([_pad_cols(wi[:, :d_ff], ffp), _pad_cols(wi[:, d_ff:], ffp)]).astype(BF16)
            wo = jnp.pad(ffn_w_out[l, idx], ((0, ffp - d_ff), (0, 0))).astype(BF16)
            h = _norm_mod_call(x, norm_g[l, idx * 2], shv, scv, tm)
            a = _mm_swiglu_call(h, w2, tm_wide)
            return _mm_res_call(a, wo, x, gv, 0.5, tm)

        x = ffn(x, 0, sh1, sc1, g1)

        h = _norm_mod_call(x, norm_g[l, 1], sh2, sc2, tm)
        wl_in = w_in[l]
        offs = np.cumsum([0, rw, rw, rw, lora, sw, sw, gn, gn, sh, cw, 3 * dm])
        segs = [wl_in[:, offs[i]:offs[i + 1]].astype(BF16) for i in range(11)]
        w_small = jnp.concatenate([segs[3], segs[6], segs[7], _pad_cols(segs[8], LANES)], axis=1)
        assert lora % gn == 0 and (lora + 2 * gn) % LANES == 0
        bm_blk, cm_blk, dt_blk = lora // gn, lora // gn + 1, (lora + 2 * gn) // LANES
        zr, zk, zv, zz, zxs, zc, zg, zs = [_mm_call(h, wseg, tm_wide)
                                           for wseg in (segs[0], segs[1], segs[2], segs[4], segs[5],
                                                        segs[9], segs[10], w_small)]
        dtt = zs[:, lora + 2 * gn:lora + 2 * gn + sh].T

        mu = rwkv_mu[l]
        pw_r = dict(ranks=(rd, ra, rg), sel=sel, selt=selt,
                    mu_r=mu[None, :rw], mu_k=mu[None, rw:2 * rw], mu_v=mu[None, 2 * rw:3 * rw],
                    mu_l=mu[None, 3 * rw:],
                    k_k=rwkv_k_k[l][None], k_a=rwkv_k_a[l][None], r_k=rwkv_r_k[l].reshape(1, rw),
                    w0=rwkv_w0[l][:, None, :], a0=rwkv_a0[l][:, None, :],
                    w2=rwkv_w2[l].astype(BF16), a2=rwkv_a2[l].astype(BF16), g2=rwkv_g2[l].astype(BF16),
                    ln_g=rwkv_ln_g[l][None], ln_b=rwkv_ln_b[l][None])
        ya_parts, fin_a = [], None
        bpb = (LANES // 2) // rh
        for gi, gr in enumerate(groups):
            b_, t_ = gr["b"], gr["t"]
            nblk = b_ // bpb
            rv_, wb_, kk_, bonus_, g_ = _rwkv_prep_call(
                zr, zk, zv, zs, gr["row0"], gr["n"], t_, gr["grid_w"], pw_r)
            if gi == 0:
                s0c = jnp.zeros((RWKV_HEAD, RWKV_HEAD, nblk * LANES), F32)
            else:
                s0c = state_rwkv[:, l].reshape(nblk, bpb, N_DIR, rh, RWKV_HEAD, RWKV_HEAD)
                s0c = s0c.transpose(5, 4, 0, 2, 1, 3).reshape(RWKV_HEAD, RWKV_HEAD, nblk * LANES)
            o_f, o_b, sf_c = _rwkv_scan_call(rv_, wb_, kk_, s0c)
            ya_parts.append(_rwkv_post_call(o_f, o_b, bonus_, g_, pw_r))
            if gi == 0:
                fin_a = sf_c.reshape(RWKV_HEAD, RWKV_HEAD, nblk, N_DIR, bpb, rh)
                fin_a = fin_a.transpose(2, 4, 3, 5, 1, 0).reshape(b_, N_DIR, rh, RWKV_HEAD, RWKV_HEAD)
        y_a = jnp.concatenate(ya_parts)
        new_a.append(fin_a)

        cw_l, cb_l = ssd_conv_w[l], ssd_conv_b[l]
        pw_s = dict(heads=sh, tri=tri, trit=trit, expand=expand,
                    bias=jnp.pad(ssd_dt_bias[l], ((0, 0), (0, LANES - sh)))[:, None, :],
                    alog=jnp.pad(ssd_a_log[l], ((0, 0), (0, LANES - sh)))[:, None, :],
                    bias_t=ssd_dt_bias[l][:, :, None], alog_t=ssd_a_log[l][:, :, None])
        yb_parts, fin_b = [], None
        for gi, gr in enumerate(groups):
            xs_ = _conv_silu_call(zxs, gr["row0"], gr["n"], gr["t"], cw_l[:, :sw], cb_l[:sw])
            bm_ = _conv_silu_call(zs, gr["row0"], gr["n"], gr["t"], cw_l[:, sw:sw + gn], cb_l[sw:sw + gn], bm_blk)
            cm_ = _conv_silu_call(zs, gr["row0"], gr["n"], gr["t"], cw_l[:, sw + gn:], cb_l[sw + gn:], cm_blk)
            s0s = None if gi == 0 else state_ssd[:, l].reshape(gr["b"], N_DIR, sw, gn // SSD_GROUPS)
            ydir, sf_s = _ssd_scan_call(xs_, bm_, cm_, zs, dt_blk, dtt, gr["row0"], gr["b"], gr["t"], pw_s, s0s)
            yb_parts.append(_ssd_post_call(xs_, ydir, zz, gr["row0"],
                                           jnp.repeat(ssd_d[l], sw // sh)[None, :], ssd_norm_g[l][None, :], tm))
            if gi == 0:
                fin_b = sf_s.reshape(gr["b"], N_DIR, sh, sw // sh, gn // SSD_GROUPS)
        y_b = jnp.concatenate(yb_parts)
        new_b.append(fin_b)

        pw_c = _s5_weights(p, l, slab_groups)
        yc_parts, fin_re, fin_im = [], None, None
        for gi, gr in enumerate(groups):
            b_, t_ = gr["b"], gr["t"]
            assert gr["row0"] % t_ == 0
            if gi == 0:
                s0 = jnp.zeros((N_DIR, 2, b_, s5_g * s5_p), F32)
            else:
                s0 = jnp.stack([state_s5_re[:, l], state_s5_im[:, l]])
                s0 = s0.reshape(2, b_, N_DIR, s5_g * s5_p).transpose(2, 0, 1, 3)
            y_dir, sf = _s5_call(zc.reshape(m // t_, t_, cw), gr["row0"] // t_, b_, pw_c, s0)
            y_dir = y_dir.reshape(N_DIR, gr["n"], cw)
            yc_parts.append(_s5_post_call(zc, gr["row0"], y_dir, s5_d[l][None, :], tm))
            if gi == 0:
                fin = sf.reshape(N_DIR, 2, b_, s5_g, s5_p).transpose(1, 2, 0, 3, 4)
                fin_re, fin_im = fin[0], fin[1]
        y_c = jnp.concatenate(yc_parts)
        new_re.append(fin_re)
        new_im.append(fin_im)

        wc = w_proj_c[l]
        merged = _merge_call(y_a, y_b, y_c, zg, w_proj_a[l].astype(BF16), w_proj_b[l].astype(BF16),
                             jnp.stack([wc[:, :dm], wc[:, dm:]]).astype(BF16), tm)
        x = _mm_res_call(merged, w_out[l].astype(BF16), x, g2, 1.0, tm)

        x = ffn(x, 1, sh3, sc3, g3)

    y = _final_norm_call(x, final_norm_g, tm)
    return (y[:n_p].reshape(bp, tp, dm), y[n_p:].reshape(bs, ts, dm),
            jnp.stack(new_a, axis=1), jnp.stack(new_b, axis=1),
            jnp.stack(new_re, axis=1), jnp.stack(new_im, axis=1))
```

```python
import functools
import math

import jax
import jax.numpy as jnp
import numpy as np
from jax import lax
from jax.experimental import pallas as pl
from jax.experimental.pallas import tpu as pltpu

F32 = jnp.float32
BF16 = jnp.bfloat16

LANES = 128
SUBLANES = 8
VMEM_LIMIT = 56 * 1024 * 1024

GRID_W = 64
SSD_CHUNK = 128
SSD_GROUPS = 2
N_DIR = 2
N_MOD = 9
EPS = 1e-6
RWKV_LN_EPS = 64e-5
RWKV_HEAD = 64
RWKV_SCAN_TB = 32
RWKV_SCAN_PARTS = 1
S5_TB = 64


def _cparams(sem):
    return pltpu.CompilerParams(dimension_semantics=sem, vmem_limit_bytes=VMEM_LIMIT)


def _pick(n, target, mult=LANES):
    best = None
    d = mult
    while d <= min(n, target):
        if n % d == 0:
            best = d
        d += mult
    return n if best is None else best


def _seq_rows(seq_len, nrows, row0, cap=2048):
    span = math.gcd(nrows, row0) if row0 else nrows
    assert span % seq_len == 0
    return _pick(span, max(cap, seq_len), seq_len)


def _round_up(n, m):
    return -(-n // m) * m


def _dot(a, b):
    return jnp.dot(a, b, preferred_element_type=F32)


def _split3(x):
    x1 = x.astype(BF16)
    r1 = x - x1.astype(F32)
    x2 = r1.astype(BF16)
    x3 = (r1 - x2.astype(F32)).astype(BF16)
    return x1, x2, x3


def _dot_exact_r(x, sel):
    return sum(_dot(p, sel) for p in _split3(x))


def _dot_exact_l(sel, x):
    return sum(_dot(sel, p) for p in _split3(x))


def _softplus(x):
    return jnp.maximum(x, 0.0) + jnp.log1p(jnp.exp(-jnp.abs(x)))


def _sigmoid(x):
    return 0.5 * (jnp.tanh(0.5 * x) + 1.0)


def _silu(x):
    return x * _sigmoid(x)


def _mod_kernel(c_ref, w_ref, b_ref, o_ref):
    c = c_ref[...]
    a = _silu(c).astype(BF16)
    o_ref[...] = _dot(a, w_ref[...].astype(BF16)) + b_ref[...]


def _mod_call(cond8, w, b):
    d, n = w.shape
    tn = _pick(n, 1024)
    return pl.pallas_call(
        _mod_kernel,
        out_shape=jax.ShapeDtypeStruct((cond8.shape[0], n), F32),
        grid=(n // tn,),
        in_specs=[pl.BlockSpec((cond8.shape[0], d), lambda j: (0, 0)),
                  pl.BlockSpec((d, tn), lambda j: (0, j)),
                  pl.BlockSpec((1, tn), lambda j: (0, j))],
        out_specs=pl.BlockSpec((cond8.shape[0], tn), lambda j: (0, j)),
        compiler_params=_cparams(("parallel",)),
        name="mod_proj",
    )(cond8, w, b.reshape(1, n))


def _norm_mod_kernel(x_ref, g_ref, sh_ref, sc_ref, o_ref):
    x = x_ref[...]
    y = x * lax.rsqrt(jnp.mean(x * x, axis=-1, keepdims=True) + EPS) * g_ref[...]
    o_ref[...] = (y * (1.0 + sc_ref[0]) + sh_ref[0]).astype(o_ref.dtype)


def _norm_mod_call(x, g, sh, sc, tm):
    m, d = x.shape
    return pl.pallas_call(
        _norm_mod_kernel,
        out_shape=jax.ShapeDtypeStruct((m, d), BF16),
        grid=(m // tm,),
        in_specs=[pl.BlockSpec((tm, d), lambda i: (i, 0)),
                  pl.BlockSpec((1, d), lambda i: (0, 0)),
                  pl.BlockSpec((1, 1, d), lambda i: (i, 0, 0)),
                  pl.BlockSpec((1, 1, d), lambda i: (i, 0, 0))],
        out_specs=pl.BlockSpec((tm, d), lambda i: (i, 0)),
        compiler_params=_cparams(("parallel",)),
        name="norm_mod",
    )(x, g.reshape(1, d), sh, sc)


def _final_norm_kernel(x_ref, g_ref, o_ref):
    x = x_ref[...]
    o_ref[...] = x * lax.rsqrt(jnp.mean(x * x, axis=-1, keepdims=True) + EPS) * g_ref[...]


def _final_norm_call(x, row0, nrows, g, tm):
    d = x.shape[1]
    r0 = row0 // tm
    return pl.pallas_call(
        _final_norm_kernel,
        out_shape=jax.ShapeDtypeStruct((nrows, d), F32),
        grid=(nrows // tm,),
        in_specs=[pl.BlockSpec((tm, d), lambda i: (r0 + i, 0)),
                  pl.BlockSpec((1, d), lambda i: (0, 0))],
        out_specs=pl.BlockSpec((tm, d), lambda i: (i, 0)),
        compiler_params=_cparams(("parallel",)),
        name="final_norm",
    )(x, g.reshape(1, d))


def _mm_kernel(a_ref, w_ref, o_ref):
    o_ref[...] = _dot(a_ref[...], w_ref[...]).astype(o_ref.dtype)


def _mm_call(a, w, tm, tn_target=1024, out_dtype=F32):
    m, k = a.shape
    n = w.shape[1]
    tn = _pick(n, tn_target)
    return pl.pallas_call(
        _mm_kernel,
        out_shape=jax.ShapeDtypeStruct((m, n), out_dtype),
        grid=(m // tm, n // tn),
        in_specs=[pl.BlockSpec((tm, k), lambda i, j: (i, 0)),
                  pl.BlockSpec((k, tn), lambda i, j: (0, j))],
        out_specs=pl.BlockSpec((tm, tn), lambda i, j: (i, j)),
        compiler_params=_cparams(("parallel", "parallel")),
        name="mm",
    )(a, w)


def _mm_swiglu_kernel(a_ref, w_ref, o_ref):
    a = a_ref[...]
    gate = _dot(a, w_ref[0])
    up = _dot(a, w_ref[1])
    o_ref[...] = (_silu(gate) * up).astype(o_ref.dtype)


def _mm_swiglu_call(a, w2, tm, tn_target=512):
    m, k = a.shape
    n = w2.shape[2]
    tn = _pick(n, tn_target)
    return pl.pallas_call(
        _mm_swiglu_kernel,
        out_shape=jax.ShapeDtypeStruct((m, n), BF16),
        grid=(m // tm, n // tn),
        in_specs=[pl.BlockSpec((tm, k), lambda i, j: (i, 0)),
                  pl.BlockSpec((2, k, tn), lambda i, j: (0, 0, j))],
        out_specs=pl.BlockSpec((tm, tn), lambda i, j: (i, j)),
        compiler_params=_cparams(("parallel", "parallel")),
        name="mm_swiglu",
    )(a, w2)


def _mm_res_kernel(a_ref, w_ref, x_ref, g_ref, o_ref, *, coef):
    o_ref[...] = x_ref[...] + (coef * g_ref[0]) * _dot(a_ref[...], w_ref[...])


def _mm_res_call(a, w, x, gate, coef, tm, tn_target=512):
    m, k = a.shape
    n = w.shape[1]
    tn = _pick(n, tn_target)
    return pl.pallas_call(
        functools.partial(_mm_res_kernel, coef=coef),
        out_shape=jax.ShapeDtypeStruct((m, n), F32),
        grid=(m // tm, n // tn),
        in_specs=[pl.BlockSpec((tm, k), lambda i, j: (i, 0)),
                  pl.BlockSpec((k, tn), lambda i, j: (0, j)),
                  pl.BlockSpec((tm, tn), lambda i, j: (i, j)),
                  pl.BlockSpec((1, 1, tn), lambda i, j: (i, 0, j))],
        out_specs=pl.BlockSpec((tm, tn), lambda i, j: (i, j)),
        compiler_params=_cparams(("parallel", "parallel")),
        name="mm_res",
    )(a, w, x, gate)


def _merge_kernel(ya_ref, yb_ref, yc_ref, ga_ref, gb_ref, gc_ref, wa_ref, wb_ref, wc_ref, o_ref):
    pa = _dot(ya_ref[...], wa_ref[...])
    pb = _dot(yb_ref[...], wb_ref[...])
    yc = yc_ref[...]
    val = _dot(yc, wc_ref[0])
    gate = _dot(yc, wc_ref[1])
    merged = (_sigmoid(ga_ref[...]) * pa + _sigmoid(gb_ref[...]) * pb
              + _sigmoid(gc_ref[...]) * (val * _sigmoid(gate)))
    o_ref[...] = merged.astype(o_ref.dtype)


def _merge_call(ya, yb, yc, zg, wa, wb, wc2, tm, tn_target=512):
    m, ka = ya.shape
    d = wa.shape[1]
    tn = _pick(d, tn_target)
    nj = d // tn
    return pl.pallas_call(
        _merge_kernel,
        out_shape=jax.ShapeDtypeStruct((m, d), BF16),
        grid=(m // tm, nj),
        in_specs=[pl.BlockSpec((tm, ka), lambda i, j: (i, 0)),
                  pl.BlockSpec((tm, yb.shape[1]), lambda i, j: (i, 0)),
                  pl.BlockSpec((tm, yc.shape[1]), lambda i, j: (i, 0)),
                  pl.BlockSpec((tm, tn), lambda i, j: (i, j)),
                  pl.BlockSpec((tm, tn), lambda i, j: (i, nj + j)),
                  pl.BlockSpec((tm, tn), lambda i, j: (i, 2 * nj + j)),
                  pl.BlockSpec((ka, tn), lambda i, j: (0, j)),
                  pl.BlockSpec((yb.shape[1], tn), lambda i, j: (0, j)),
                  pl.BlockSpec((2, yc.shape[1], tn), lambda i, j: (0, 0, j))],
        out_specs=pl.BlockSpec((tm, tn), lambda i, j: (i, j)),
        compiler_params=_cparams(("parallel", "parallel")),
        name="merge",
    )(ya, yb, yc, zg, zg, zg, wa, wb, wc2)


def _row_shift(x, off, t_idx, seq_len):
    rows = x.shape[0]
    rolled = pltpu.roll(x, (-off) % rows, axis=0)
    src = t_idx + off
    ok = jnp.logical_and(src >= 0, src < seq_len)
    return jnp.where(ok, rolled, 0.0)


def _centred_nb(x, t_idx, seq_len, grid_w):
    if grid_w is None:
        return 0.5 * (_row_shift(x, -1, t_idx, seq_len) + _row_shift(x, 1, t_idx, seq_len))
    col = t_idx % grid_w
    left = jnp.where(col >= 1, _row_shift(x, -1, t_idx, seq_len), 0.0)
    right = jnp.where(col < grid_w - 1, _row_shift(x, 1, t_idx, seq_len), 0.0)
    up = _row_shift(x, -grid_w, t_idx, seq_len)
    down = _row_shift(x, grid_w, t_idx, seq_len)
    return 0.25 * (up + down + left + right)


def _rwkv_prep_kernel(zr_ref, zk_ref, zv_ref, zl_ref,
                      mur_ref, muk_ref, muv_ref, mul_ref,
                      kk_w_ref, ka_w_ref, rk_w_ref, w0_ref, a0_ref,
                      w2_ref, a2_ref, g2_ref, sel_ref, selt_ref,
                      rv_o, wb_o, kk_o, bonus_o, g_o,
                      *, seq_len, grid_w, ranks):
    nseq = rv_o.shape[0]

    def emit(write, a, b):
        lane = lax.broadcasted_iota(jnp.int32, a.shape, 1)
        first = lane < RWKV_HEAD
        head0 = jnp.where(first, a, pltpu.roll(b, RWKV_HEAD, axis=1))
        head1 = jnp.where(first, pltpu.roll(a, RWKV_HEAD, axis=1), b)
        for q in range(nseq):
            write(q, 0, head0[q * seq_len:(q + 1) * seq_len])
            write(q, 1, head1[q * seq_len:(q + 1) * seq_len])

    def shifted(ref, mu_ref):
        x = ref[...]
        t_idx = lax.broadcasted_iota(jnp.int32, x.shape, 0) % seq_len
        return x + mu_ref[...] * (_centred_nb(x, t_idx, seq_len, grid_w) - x)

    r = shifted(zr_ref, mur_ref)
    k = shifted(zk_ref, muk_ref)
    v = shifted(zv_ref, muv_ref)
    lo = shifted(zl_ref, mul_ref)
    rd, ra, rg = ranks
    wl = lo[:, :rd]
    al = lo[:, rd:rd + ra]
    gl = lo[:, rd + ra:rd + ra + rg]
    sel = sel_ref[...]
    selt = selt_ref[...]

    def head_sum(x):
        return _dot_exact_r(_dot_exact_r(x, sel), selt)

    kk = k * kk_w_ref[...]
    kk = kk * lax.rsqrt(head_sum(kk * kk) + 1e-12)
    def write_rv(q, hh, val):
        rv_o[q, hh] = val

    emit(write_rv, r, v)
    tw = jnp.tanh(wl).astype(BF16)
    alb = al.astype(BF16)
    for d in range(N_DIR):
        w_log = -_softplus(-(w0_ref[d] + _dot(tw, w2_ref[d]))) - 0.5
        a_d = _sigmoid(a0_ref[d] + _dot(alb, a2_ref[d]))

        def write_wb(q, hh, val, d=d):
            wb_o[d, q, hh] = val

        def write_kk(q, hh, val, d=d):
            kk_o[d, q, hh] = val

        emit(write_wb, jnp.exp(-jnp.exp(w_log)), kk * a_d)
        emit(write_kk, k * (1.0 + (a_d - 1.0) * ka_w_ref[...]), kk)
    bonus_o[...] = head_sum(r * k * rk_w_ref[...]) * v
    g_o[...] = _dot(_sigmoid(gl).astype(BF16), g2_ref[...])


def _rwkv_prep_call(zr, zk, zv, zl, row0, nrows, seq_len, grid_w, pw):
    w = zr.shape[1]
    lw = sum(pw["ranks"])
    cb = LANES
    rb = _seq_rows(seq_len, nrows, row0)
    assert w % cb == 0
    r0 = row0 // rb
    main = pl.BlockSpec((rb, cb), lambda i, j: (r0 + i, j))
    lspec = pl.BlockSpec((rb, lw), lambda i, j: (r0 + i, 0))
    colp = pl.BlockSpec((1, cb), lambda i, j: (0, j))
    dirp = pl.BlockSpec((N_DIR, 1, cb), lambda i, j: (0, 0, j))
    ospec = pl.BlockSpec((rb, cb), lambda i, j: (i, j))
    rd, ra, rg = pw["ranks"]
    nsel = pw["sel"].shape[1]
    assert cb == 2 * RWKV_HEAD
    nseq = rb // seq_len
    bsz = nrows // seq_len
    heads = w // RWKV_HEAD
    pk = jax.ShapeDtypeStruct((bsz, heads, seq_len, cb), F32)
    pkd = jax.ShapeDtypeStruct((N_DIR, bsz, heads, seq_len, cb), F32)
    pspec = pl.BlockSpec((nseq, 2, seq_len, cb), lambda i, j: (i, j, 0, 0))
    pdspec = pl.BlockSpec((N_DIR, nseq, 2, seq_len, cb), lambda i, j: (0, i, j, 0, 0))
    return pl.pallas_call(
        functools.partial(_rwkv_prep_kernel, seq_len=seq_len, grid_w=grid_w, ranks=pw["ranks"]),
        out_shape=[pk, pkd, pkd, jax.ShapeDtypeStruct((nrows, w), F32), jax.ShapeDtypeStruct((nrows, w), F32)],
        grid=(nrows // rb, w // cb),
        in_specs=[main, main, main, lspec,
                  colp, colp, colp, pl.BlockSpec((1, lw), lambda i, j: (0, 0)),
                  colp, colp, colp, dirp, dirp,
                  pl.BlockSpec((N_DIR, rd, cb), lambda i, j: (0, 0, j)),
                  pl.BlockSpec((N_DIR, ra, cb), lambda i, j: (0, 0, j)),
                  pl.BlockSpec((rg, cb), lambda i, j: (0, j)),
                  pl.BlockSpec((cb, nsel), lambda i, j: (0, 0)),
                  pl.BlockSpec((nsel, cb), lambda i, j: (0, 0))],
        out_specs=[pspec, pdspec, pdspec, ospec, ospec],
        compiler_params=_cparams(("parallel", "parallel")),
        name="rwkv_prep",
    )(zr, zk, zv, zl, pw["mu_r"], pw["mu_k"], pw["mu_v"], pw["mu_l"],
      pw["k_k"], pw["k_a"], pw["r_k"], pw["w0"], pw["a0"],
      pw["w2"], pw["a2"], pw["g2"], pw["sel"], pw["selt"])


def _load_time_chunk(ref, t0):
    idx = (0,) * (len(ref.shape) - 4) + (slice(None), slice(None), pl.ds(t0, SUBLANES), slice(None))
    x = ref[idx]
    return x.reshape(x.shape[0] * x.shape[1], SUBLANES, x.shape[-1])


def _store_time_rows(ref, s, val):
    steps, width = ref.shape[-2], ref.shape[-1]
    chains = math.prod(ref.shape[:-2])
    ref.reshape(chains * steps, width)[pl.ds(s, chains, stride=steps), :] = val


def _rwkv_scan_kernel(rva_ref, rvb_ref, wba_ref, wbb_ref, kka_ref, kkb_ref, s0_ref,
                      oa_ref, ob_ref, sf_ref, s_ref, out_ref, *ops_refs):
    i = pl.program_id(1)
    n = s_ref.shape[0]
    nj = n // SUBLANES
    steps = out_ref.shape[0]
    half = LANES // 2
    r_off, v_off, w_off, b_off, kd_off, kk_off = (q * n for q in range(6))

    @pl.when(i == 0)
    def _():
        s_ref[...] = s0_ref[...]

    srcs = ((rva_ref, rvb_ref), (wba_ref, wbb_ref), (kka_ref, kkb_ref))

    nparts = len(ops_refs)
    plen = steps // nparts

    def relayout(part, c):
        t0 = pl.multiple_of(part * plen + c * SUBLANES, SUBLANES)
        tb0 = pl.multiple_of(steps - SUBLANES - (part * plen + c * SUBLANES), SUBLANES)
        for p, (a_ref, b_ref) in enumerate(srcs):
            xa = jnp.swapaxes(_load_time_chunk(a_ref, t0), 0, 1)
            xb = jnp.swapaxes(_load_time_chunk(b_ref, tb0), 0, 1)
            for q in range(SUBLANES):
                x = jnp.concatenate([xa[q], xb[SUBLANES - 1 - q]], axis=0)
                ops_refs[part][c * SUBLANES + q, pl.ds(p * LANES, LANES), :] = x.T

    def unlay(t):
        tb_ = steps - 1 - t
        ot = out_ref[t].T
        _store_time_rows(oa_ref, t, ot[:half])
        _store_time_rows(ob_ref, tb_, ot[half:])

    out_ref[0] = jnp.zeros(out_ref.shape[1:], F32)

    def relayout_chunk(c, carry):
        relayout(0, c)
        return carry

    lax.fori_loop(0, plen // SUBLANES, relayout_chunk, 0)

    def state(k, j):
        return s_ref[k, pl.ds(j * SUBLANES, SUBLANES), :]

    for part in range(nparts):
        ops_ref = ops_refs[part]

        def row(s, r, ops_ref=ops_ref):
            return jnp.broadcast_to(ops_ref[s, pl.ds(r, 1), :], (SUBLANES, LANES))

        acc0 = [None] * nj
        for k in range(n):
            kkb = row(0, kk_off + k)
            for j in range(nj):
                p = state(k, j) * kkb
                acc0[j] = p if acc0[j] is None else acc0[j] + p

        def step(s, acc, part=part, ops_ref=ops_ref, row=row):
            t = part * plen + s
            unlay(jnp.maximum(t - 1, 0))
            s_next = jnp.minimum(s + 1, plen - 1)
            vv = [ops_ref[s, pl.ds(v_off + j * SUBLANES, SUBLANES), :] for j in range(nj)]
            out = [None] * nj
            acc_next = [None] * nj
            for k in range(n):
                wb = row(s, w_off + k)
                bb = row(s, b_off + k)
                kdb = row(s, kd_off + k)
                rb = row(s, r_off + k)
                kkn = row(s_next, kk_off + k)
                for j in range(nj):
                    s_new = state(k, j) * wb - acc[j] * bb + vv[j] * kdb
                    s_ref[k, pl.ds(j * SUBLANES, SUBLANES), :] = s_new
                    q = s_new * rb
                    out[j] = q if out[j] is None else out[j] + q
                    p = s_new * kkn
                    acc_next[j] = p if acc_next[j] is None else acc_next[j] + p
            for j in range(nj):
                out_ref[t, pl.ds(j * SUBLANES, SUBLANES), :] = out[j]
            return tuple(acc_next)

        lax.fori_loop(0, plen, step, tuple(acc0))
    unlay(steps - 1)

    @pl.when(i == pl.num_programs(1) - 1)
    def _():
        sf_ref[...] = s_ref[...]


def _rwkv_scan_call(rv, wb, kk, s0):
    bsz, heads, t, _ = rv.shape
    n = RWKV_HEAD
    bpb = (LANES // 2) // heads
    assert bpb * heads * 2 == LANES and bsz % bpb == 0
    nblk = bsz // bpb
    tb = min(RWKV_SCAN_TB, t)
    nparts = RWKV_SCAN_PARTS
    assert t % tb == 0 and tb % nparts == 0
    nt = t // tb
    fwd4 =pl.BlockSpec((bpb, heads, tb, LANES), lambda g, i: (g, 0, i, 0))
    bwd4 = pl.BlockSpec((bpb, heads, tb, LANES), lambda g, i: (g, 0, nt - 1 - i, 0))
    fwd5 = pl.BlockSpec((1, bpb, heads, tb, LANES), lambda g, i: (0, g, 0, i, 0))
    bwd5 = pl.BlockSpec((1, bpb, heads, tb, LANES), lambda g, i: (1, g, 0, nt - 1 - i, 0))
    st = pl.BlockSpec((n, n, LANES), lambda g, i: (0, 0, g))
    o_sds = jax.ShapeDtypeStruct((bsz, heads, t, n), F32)
    return pl.pallas_call(
        _rwkv_scan_kernel,
        out_shape=[o_sds, o_sds, jax.ShapeDtypeStruct((n, n, nblk * LANES), F32)],
        grid=(nblk, nt),
        in_specs=[fwd4, bwd4, fwd5, bwd5, fwd5, bwd5, st],
        out_specs=[pl.BlockSpec((bpb, heads, tb, n), lambda g, i: (g, 0, i, 0)),
                   pl.BlockSpec((bpb, heads, tb, n), lambda g, i: (g, 0, nt - 1 - i, 0)),
                   st],
        scratch_shapes=[pltpu.VMEM((n, n, LANES), F32),
                        pltpu.VMEM((tb, n, LANES), F32)]
        + [pltpu.VMEM((tb // nparts, 6 * n, LANES), F32)] * nparts,
        compiler_params=_cparams(("parallel", "arbitrary")),
        name="rwkv_scan",
    )(rv, rv, wb, wb, kk, kk, s0)


def _rwkv_post_kernel(of_ref, ob_ref, bonus_ref, g_ref, lng_ref, lnb_ref, sel_ref, selt_ref, o_ref,
                      *, head):
    sel = sel_ref[...]
    selt = selt_ref[...]

    def head_sum(x):
        return _dot_exact_r(_dot_exact_r(x, sel), selt)

    def rows(ref):
        parts = [jnp.concatenate([ref[q, 0], ref[q, 1]], axis=1) for q in range(ref.shape[0])]
        return parts[0] if len(parts) == 1 else jnp.concatenate(parts, axis=0)

    o = rows(of_ref) + rows(ob_ref)
    mu = head_sum(o) * (1.0 / head)
    dlt = o - mu
    var = head_sum(dlt * dlt) * (1.0 / head)
    on = dlt * lax.rsqrt(var + RWKV_LN_EPS) * lng_ref[...] + lnb_ref[...]
    o_ref[...] = ((on + bonus_ref[...]) * g_ref[...]).astype(o_ref.dtype)


def _rwkv_post_call(of, ob, bonus, g, pw):
    bsz, heads, t, n = of.shape
    m, w = bonus.shape
    cb = LANES
    nsel = pw["sel"].shape[1]
    tm = _seq_rows(t, m, 0)
    nseq = tm // t
    main = pl.BlockSpec((tm, cb), lambda i, j: (i, j))
    colp = pl.BlockSpec((1, cb), lambda i, j: (0, j))
    ospec = pl.BlockSpec((nseq, 2, t, n), lambda i, j: (i, j, 0, 0))
    return pl.pallas_call(
        functools.partial(_rwkv_post_kernel, head=RWKV_HEAD),
        out_shape=jax.ShapeDtypeStruct((m, w), BF16),
        grid=(m // tm, w // cb),
        in_specs=[ospec, ospec, main, main, colp, colp,
                  pl.BlockSpec((cb, nsel), lambda i, j: (0, 0)),
                  pl.BlockSpec((nsel, cb), lambda i, j: (0, 0))],
        out_specs=main,
        compiler_params=_cparams(("parallel", "parallel")),
        name="rwkv_post",
    )(of, ob, bonus, g, pw["ln_g"], pw["ln_b"], pw["sel"], pw["selt"])


def _conv_silu_kernel(x_ref, w_ref, b_ref, o_ref, *, seq_len):
    x = x_ref[...]
    rows = x.shape[0]
    kw = w_ref.shape[0]
    t_idx = lax.broadcasted_iota(jnp.int32, x.shape, 0) % seq_len
    y = b_ref[...] + jnp.zeros_like(x)
    for j in range(kw):
        off = j - kw // 2
        xs = x if off == 0 else _row_shift(x, off, t_idx, seq_len)
        y = y + w_ref[pl.ds(j, 1), :] * xs
    o_ref[...] = _silu(y)


def _conv_silu_call(x, row0, nrows, seq_len, w, b, col_blk=0):
    kw, c = w.shape
    cb = _pick(c, 256)
    rb = _seq_rows(seq_len, nrows, row0)
    r0 = row0 // rb
    j0 = col_blk * (c // cb)
    return pl.pallas_call(
        functools.partial(_conv_silu_kernel, seq_len=seq_len),
        out_shape=jax.ShapeDtypeStruct((nrows, c), F32),
        grid=(nrows // rb, c // cb),
        in_specs=[pl.BlockSpec((rb, cb), lambda i, j: (r0 + i, j0 + j)),
                  pl.BlockSpec((kw, cb), lambda i, j: (0, j)),
                  pl.BlockSpec((1, cb), lambda i, j: (0, j))],
        out_specs=pl.BlockSpec((rb, cb), lambda i, j: (i, j)),
        compiler_params=_cparams(("parallel", "parallel")),
        name="ssd_conv",
    )(x, w, b.reshape(1, c))


def _ssd_scan_kernel(x_ref, b_ref, c_ref, dt_ref, dtt_ref, bias_ref, alog_ref, biast_ref, alogt_ref,
                     tri_ref, trit_ref, e_ref, s0_ref, y_ref, sf_ref, st_ref,
                     *, heads, hdim, nstate, groups, has_init):
    d = pl.program_id(0)
    c = pl.program_id(2)
    nc = pl.num_programs(2)
    hpg = heads // groups
    gw = hpg * hdim

    @pl.when(c == 0)
    def _():
        if has_init:
            for g in range(groups):
                st_ref[pl.ds(g * nstate, nstate), :] = s0_ref[0, 0, pl.ds(g * gw, gw), :].T
        else:
            st_ref[...] = jnp.zeros_like(st_ref)

    tri = tri_ref[0]
    trit = trit_ref[0]
    lch = tri.shape[0]
    e = e_ref[...]
    dtp = _softplus(dt_ref[...] + bias_ref[0])
    a = -jnp.exp(alog_ref[0])
    da = dtp * a
    cum = _dot_exact_l(tri, da)
    dat = _softplus(dtt_ref[...] + biast_ref[0]) * (-jnp.exp(alogt_ref[0]))
    cumt = _dot_exact_r(dat, trit)
    total = jnp.sum(da, axis=0, keepdims=True)
    tot8 = jnp.broadcast_to(total, (SUBLANES, LANES))
    dt_full = _dot_exact_r(dtp, e)
    din_full = _dot_exact_r(jnp.exp(cum), e)
    dst_full = _dot_exact_r(jnp.exp(total - cum), e)
    tot_full = _dot_exact_r(jnp.exp(tot8), e)[0:1, :]
    x = x_ref[...]
    xdt = x * dt_full
    xdec = (xdt * dst_full).astype(BF16)
    xdt_b = xdt.astype(BF16)
    bm = b_ref[...]
    cm = c_ref[...]
    visible = tri > 0
    for g in range(groups):
        bg = bm[:, g * nstate:(g + 1) * nstate]
        cg = cm[:, g * nstate:(g + 1) * nstate].astype(BF16)
        bgb = bg.astype(BF16)
        cb = lax.dot_general(cg, bgb, (((1,), (1,)), ((), ())), preferred_element_type=F32)
        st_g = st_ref[pl.ds(g * nstate, nstate), :]
        y_off = _dot(cg, st_g.astype(BF16)) * din_full[:, g * gw:(g + 1) * gw]
        for hh in range(hpg):
            h = g * hpg + hh
            seg = cum[:, h:h + 1] - cumt[h:h + 1, :]
            lmat = jnp.exp(jnp.where(visible, seg, -jnp.inf))
            gmat = (cb * lmat).astype(BF16)
            yd = _dot(gmat, xdt_b[:, h * hdim:(h + 1) * hdim])
            y_ref[0, :, pl.ds(h * hdim, hdim)] = yd + y_off[:, hh * hdim:(hh + 1) * hdim]
        upd = _dot(bg.T.astype(BF16), xdec[:, g * gw:(g + 1) * gw])
        st_ref[pl.ds(g * nstate, nstate), :] = st_g * tot_full[:, g * gw:(g + 1) * gw] + upd

    @pl.when(c == nc - 1)
    def _():
        for g in range(groups):
            sf_ref[0, 0, pl.ds(g * gw, gw), :] = st_ref[pl.ds(g * nstate, nstate), :].T


def _ssd_scan_call(xs, bm, cm, zdt, dt_blk, dtt, row0, bsz, seq_len, pw, s0):
    hp = xs.shape[1]
    gn = bm.shape[1]
    heads = pw["heads"]
    hdim = hp // heads
    nstate = gn // SSD_GROUPS
    lch = min(SSD_CHUNK, seq_len)
    nc = seq_len // lch
    assert seq_len % lch == 0 and row0 % lch == 0
    c0 = row0 // lch
    has_init = s0 is not None
    if s0 is None:
        s0 = jnp.zeros((1, 1, hp, nstate), F32)

    def cidx(d, b, c):
        return b * nc + c + d * (nc - 1 - 2 * c)

    row = lambda d, b, c: (cidx(d, b, c), 0)
    s0_map = (lambda d, b, c: (b, d, 0, 0)) if has_init else (lambda d, b, c: (0, 0, 0, 0))
    dirp = pl.BlockSpec((1, 1, LANES), lambda d, b, c: (d, 0, 0))
    dirt = pl.BlockSpec((1, heads, 1), lambda d, b, c: (d, 0, 0))
    return pl.pallas_call(
        functools.partial(_ssd_scan_kernel, heads=heads, hdim=hdim, nstate=nstate,
                          groups=SSD_GROUPS, has_init=has_init),
        out_shape=[jax.ShapeDtypeStruct((N_DIR, bsz * seq_len, hp), F32),
                   jax.ShapeDtypeStruct((bsz, N_DIR, hp, nstate), F32)],
        grid=(N_DIR, bsz, nc),
        in_specs=[pl.BlockSpec((lch, hp), row),
                  pl.BlockSpec((lch, gn), row),
                  pl.BlockSpec((lch, gn), row),
                  pl.BlockSpec((lch, LANES), lambda d, b, c: (c0 + cidx(d, b, c), dt_blk)),
                  pl.BlockSpec((heads, lch), lambda d, b, c: (0, c0 + cidx(d, b, c))),
                  dirp, dirp, dirt, dirt,
                  pl.BlockSpec((1, lch, lch), lambda d, b, c: (d, 0, 0)),
                  pl.BlockSpec((1, lch, lch), lambda d, b, c: (d, 0, 0)),
                  pl.BlockSpec((LANES, hp), lambda d, b, c: (0, 0)),
                  pl.BlockSpec((1, 1, hp, nstate), s0_map)],
        out_specs=[pl.BlockSpec((1, lch, hp), lambda d, b, c: (d, cidx(d, b, c), 0)),
                   pl.BlockSpec((1, 1, hp, nstate), lambda d, b, c: (b, d, 0, 0))],
        scratch_shapes=[pltpu.VMEM((gn, hp // SSD_GROUPS), F32)],
        compiler_params=_cparams(("arbitrary", "arbitrary", "arbitrary")),
        name="ssd_scan",
    )(xs, bm, cm, zdt, dtt, pw["bias"], pw["alog"], pw["bias_t"], pw["alog_t"],
      pw["tri"][:, :lch, :lch], pw["trit"][:, :lch, :lch], pw["expand"], s0)


def _ssd_post_kernel(x_ref, yf_ref, yb_ref, z_ref, d_ref, g_ref, o_ref):
    y = (d_ref[...] * x_ref[...] + yf_ref[0] + yb_ref[0]) * _silu(z_ref[...])
    y = y * lax.rsqrt(jnp.mean(y * y, axis=-1, keepdims=True) + EPS) * g_ref[...]
    o_ref[...] = y.astype(o_ref.dtype)


def _ssd_post_call(xs, ydir, zz, row0, d_full, g, tm):
    n, hp = xs.shape
    r0 = row0 // tm
    return pl.pallas_call(
        _ssd_post_kernel,
        out_shape=jax.ShapeDtypeStruct((n, hp), BF16),
        grid=(n // tm,),
        in_specs=[pl.BlockSpec((tm, hp), lambda i: (i, 0)),
                  pl.BlockSpec((1, tm, hp), lambda i: (0, i, 0)),
                  pl.BlockSpec((1, tm, hp), lambda i: (1, i, 0)),
                  pl.BlockSpec((tm, hp), lambda i: (r0 + i, 0)),
                  pl.BlockSpec((1, hp), lambda i: (0, 0)),
                  pl.BlockSpec((1, hp), lambda i: (0, 0))],
        out_specs=pl.BlockSpec((tm, hp), lambda i: (i, 0)),
        compiler_params=_cparams(("parallel",)),
        name="ssd_post",
    )(xs, ydir, ydir, zz, d_full, g)


def _s5_kernel(u_ref, bmat_ref, cmat_ref, lam_ref, s0_ref, y_ref, sf_ref,
               st_ref, buf_ref, *, slab_in, slab_state):
    d = pl.program_id(0)
    tb = pl.program_id(2)
    ntb = pl.num_programs(2)
    nb, steps = u_ref.shape[0], u_ref.shape[1]
    nslab = u_ref.shape[2] // slab_in
    spl = buf_ref.shape[0]
    ncol = slab_state // LANES
    assert slab_in == LANES

    @pl.when(tb == 0)
    def _():
        st_ref[...] = s0_ref[0]

    for s_base in range(0, nslab, spl):
        lam = []
        init = []
        for q in range(spl):
            s = s_base + q
            u = jnp.swapaxes(u_ref[:, :, pl.ds(s * slab_in, slab_in)], 0, 1)
            bu = _dot(u.reshape(steps * nb, slab_in).astype(BF16), bmat_ref[0, s])
            for c in range(2 * ncol):
                buf_ref[q, c] = bu[:, c * LANES:(c + 1) * LANES]
            for c in range(ncol):
                lanes = pl.ds(s * slab_state + c * LANES, LANES)
                lam.append((jnp.broadcast_to(lam_ref[0, 0, :, lanes], (nb, LANES)),
                            jnp.broadcast_to(lam_ref[0, 1, :, lanes], (nb, LANES))))
                init.append(st_ref[0, :, lanes])
                init.append(st_ref[1, :, lanes])

        def step(i, carry):
            te = i + d * (steps - 1 - 2 * i)
            rows = pl.ds(pl.multiple_of(te * nb, nb), nb)
            new = []
            for q in range(spl):
                for c in range(ncol):
                    lr, li = lam[q * ncol + c]
                    s_re = carry[2 * (q * ncol + c)]
                    s_im = carry[2 * (q * ncol + c) + 1]
                    n_re = lr * s_re - li * s_im + buf_ref[q, c, rows, :]
                    n_im = lr * s_im + li * s_re + buf_ref[q, ncol + c, rows, :]
                    buf_ref[q, c, rows, :] = n_re
                    buf_ref[q, ncol + c, rows, :] = n_im
                    new += [n_re, n_im]
            return tuple(new)

        fin = lax.fori_loop(0, steps, step, tuple(init), unroll=4)
        for q in range(spl):
            s = s_base + q
            for c in range(ncol):
                lanes = pl.ds(s * slab_state + c * LANES, LANES)
                st_ref[0, :, lanes] = fin[2 * (q * ncol + c)]
                st_ref[1, :, lanes] = fin[2 * (q * ncol + c) + 1]
            states = jnp.concatenate([buf_ref[q, c] for c in range(2 * ncol)], axis=1)
            y = _dot(states.astype(BF16), cmat_ref[0, s])
            y_ref[0, :, :, pl.ds(s * slab_in, slab_in)] = jnp.swapaxes(y.reshape(steps, nb, slab_in), 0, 1)

    @pl.when(tb == ntb - 1)
    def _():
        sf_ref[0] = st_ref[...]


def _s5_call(u3, seq0, bsz, pw, s0):
    _, t, w = u3.shape
    nb = min(SUBLANES, bsz)
    tb = min(S5_TB, t)
    assert t % tb == 0 and bsz % nb == 0 and seq0 % nb == 0
    ntb = t // tb
    b0 = seq0 // nb
    gp = pw["lam"].shape[-1]
    slab_in = pw["slab_in"]
    slab_state = pw["slab_state"]
    nslab = w // slab_in
    spl = 2 if nslab % 2 == 0 else 1

    def tidx(d, i):
        return i + d * (ntb - 1 - 2 * i)

    return pl.pallas_call(
        functools.partial(_s5_kernel, slab_in=slab_in, slab_state=slab_state),
        out_shape=[jax.ShapeDtypeStruct((N_DIR, bsz, t, w), F32),
                   jax.ShapeDtypeStruct((N_DIR, 2, bsz, gp), F32)],
        grid=(N_DIR, bsz // nb, ntb),
        in_specs=[pl.BlockSpec((nb, tb, w), lambda d, b, i: (b0 + b, tidx(d, i), 0)),
                  pl.BlockSpec((1, nslab, slab_in, 2 * slab_state), lambda d, b, i: (d, 0, 0, 0)),
                  pl.BlockSpec((1, nslab, 2 * slab_state, slab_in), lambda d, b, i: (d, 0, 0, 0)),
                  pl.BlockSpec((1, 2, 1, gp), lambda d, b, i: (d, 0, 0, 0)),
                  pl.BlockSpec((1, 2, nb, gp), lambda d, b, i: (d, 0, b, 0))],
        out_specs=[pl.BlockSpec((1, nb, tb, w), lambda d, b, i: (d, b, tidx(d, i), 0)),
                   pl.BlockSpec((1, 2, nb, gp), lambda d, b, i: (d, 0, b, 0))],
        scratch_shapes=[pltpu.VMEM((2, nb, gp), F32),
                        pltpu.VMEM((spl, 2 * slab_state // LANES, nb * tb, LANES), F32)],
        compiler_params=_cparams(("arbitrary", "arbitrary", "arbitrary")),
        name="s5_scan",
    )(u3, pw["bmat"], pw["cmat"], pw["lam"], s0)


def _s5_post_kernel(u_ref, yf_ref, yb_ref, d_ref, o_ref):
    y = d_ref[...] * u_ref[...] + yf_ref[0] + yb_ref[0]
    o_ref[...] = jax.nn.gelu(y).astype(o_ref.dtype)


def _s5_post_call(u, row0, ydir, d_full, tm):
    _, m, w = ydir.shape
    r0 = row0 // tm
    return pl.pallas_call(
        _s5_post_kernel,
        out_shape=jax.ShapeDtypeStruct((m, w), BF16),
        grid=(m // tm,),
        in_specs=[pl.BlockSpec((tm, w), lambda i: (r0 + i, 0)),
                  pl.BlockSpec((1, tm, w), lambda i: (0, i, 0)),
                  pl.BlockSpec((1, tm, w), lambda i: (1, i, 0)),
                  pl.BlockSpec((1, w), lambda i: (0, 0))],
        out_specs=pl.BlockSpec((tm, w), lambda i: (i, 0)),
        compiler_params=_cparams(("parallel",)),
        name="s5_post",
    )(u, ydir, ydir, d_full)


def _head_selectors(cb, head):
    nsel = LANES
    col = np.arange(cb)[:, None] // head
    sel = (col == np.arange(nsel)[None, :]).astype(np.float32)
    return jnp.asarray(sel, BF16), jnp.asarray(sel.T, BF16)


def _s5_weights(p, l, slab_groups):
    g, pst = p["s5_lambda_re"].shape[2:]
    cg = p["s5_b_re"].shape[-1]
    nslab = g // slab_groups
    eye = jnp.eye(slab_groups, dtype=F32)
    bmats, cmats, lams = [], [], []
    for d in range(N_DIR):
        lam_re, lam_im = p["s5_lambda_re"][l, d], p["s5_lambda_im"][l, d]
        delta = jnp.exp(p["s5_log_dt"][l, d])[:, None]
        mag = jnp.exp(lam_re * delta)
        lb_re, lb_im = mag * jnp.cos(lam_im * delta), mag * jnp.sin(lam_im * delta)
        den = lam_re * lam_re + lam_im * lam_im
        q_re = ((lb_re - 1.0) * lam_re + lb_im * lam_im) / den
        q_im = (lb_im * lam_re - (lb_re - 1.0) * lam_im) / den
        b_re, b_im = p["s5_b_re"][l, d], p["s5_b_im"][l, d]
        bb_re = q_re[..., None] * b_re - q_im[..., None] * b_im
        bb_im = q_re[..., None] * b_im + q_im[..., None] * b_re

        def in_blocks(bb):
            x = bb.reshape(nslab, slab_groups, pst, cg)
            return jnp.einsum("sgpc,gh->sgchp", x, eye).reshape(nslab, slab_groups * cg, slab_groups * pst)

        def out_blocks(cc):
            x = cc.reshape(nslab, slab_groups, cg, pst)
            return jnp.einsum("sgcp,gh->sgphc", x, eye).reshape(nslab, slab_groups * pst, slab_groups * cg)

        bmats.append(jnp.concatenate([in_blocks(bb_re), in_blocks(bb_im)], axis=-1))
        cmats.append(jnp.concatenate([out_blocks(p["s5_c_re"][l, d]), -out_blocks(p["s5_c_im"][l, d])], axis=-2))
        lams.append(jnp.stack([lb_re.reshape(1, g * pst), lb_im.reshape(1, g * pst)]))
    return dict(bmat=jnp.stack(bmats).astype(BF16), cmat=jnp.stack(cmats).astype(BF16),
                lam=jnp.stack(lams), slab_in=slab_groups * cg, slab_state=slab_groups * pst)


def _pad_cols(w, n):
    return jnp.pad(w, ((0, 0), (0, n - w.shape[1])))


def kernel(x_prompt, x_sample, state_rwkv, state_ssd, state_s5_re, state_s5_im, c, c_ctx, w_mod, b_mod, norm_g, ffn_w_in, ffn_w_out, w_in, rwkv_mu, rwkv_w0, rwkv_w2, rwkv_a0, rwkv_a2, rwkv_g2, rwkv_k_k, rwkv_k_a, rwkv_r_k, rwkv_ln_g, rwkv_ln_b, w_proj_a, ssd_conv_w, ssd_conv_b, ssd_dt_bias, ssd_a_log, ssd_d, ssd_norm_g, w_proj_b, s5_lambda_re, s5_lambda_im, s5_log_dt, s5_b_re, s5_b_im, s5_c_re, s5_c_im, s5_d, w_proj_c, w_out, final_norm_g):
    p = dict(s5_lambda_re=s5_lambda_re, s5_lambda_im=s5_lambda_im, s5_log_dt=s5_log_dt,
             s5_b_re=s5_b_re, s5_b_im=s5_b_im, s5_c_re=s5_c_re, s5_c_im=s5_c_im)
    bp, tp, dm = x_prompt.shape
    bs, ts, _ = x_sample.shape
    n_p, n_s = bp * tp, bs * ts
    m = n_p + n_s
    depth = w_mod.shape[0]
    d_ff = ffn_w_out.shape[2]
    ffp = _round_up(d_ff, 512)
    rw = rwkv_k_k.shape[1]
    rh = rw // RWKV_HEAD
    rd, ra, rg = rwkv_w2.shape[2], rwkv_a2.shape[2], rwkv_g2.shape[1]
    lora = rd + ra + rg
    sh = ssd_d.shape[1]
    sw = ssd_norm_g.shape[1]
    xbc_w = ssd_conv_w.shape[2]
    gn = (xbc_w - sw) // 2
    cw = s5_d.shape[1]
    s5_g, s5_p = s5_lambda_re.shape[2:]
    s5_cg = cw // s5_g
    slab_groups = max(1, min(s5_g, LANES // s5_cg))

    tm = _pick(math.gcd(n_p, ts), 1024, SUBLANES)
    tm_wide = _pick(m, 2 * tm, tm)
    n_tiles = m // tm
    tile_cond = np.array([0 if i * tm < n_p else 1 + (i * tm - n_p) // ts for i in range(n_tiles)])

    x = jnp.concatenate([x_prompt.reshape(n_p, dm), x_sample.reshape(n_s, dm)], axis=0)
    ncond = 1 + bs
    cond = jnp.concatenate([c_ctx[None, :], c], axis=0)
    cond8 = jnp.pad(cond, ((0, _round_up(ncond, SUBLANES) - ncond), (0, 0)))

    groups = [dict(b=bp, t=tp, row0=0, n=n_p, grid_w=None),
              dict(b=bs, t=ts, row0=n_p, n=n_s, grid_w=GRID_W)]

    sel, selt = _head_selectors(LANES, RWKV_HEAD)
    heads_per_blk = LANES // RWKV_HEAD
    tri_f = np.tril(np.ones((SSD_CHUNK, SSD_CHUNK), np.float32))
    tri = jnp.asarray(np.stack([tri_f, tri_f.T]), BF16)
    trit = jnp.asarray(np.stack([tri_f.T, tri_f]), BF16)
    expand = jnp.asarray((np.arange(LANES)[:, None] == (np.arange(sw)[None, :] // (sw // sh))).astype(np.float32), BF16)

    new_a, new_b, new_re, new_im = [], [], [], []
    for l in range(depth):
        mod = _mod_call(cond8, w_mod[l], b_mod[l])
        mods = mod.reshape(-1, N_MOD, dm)[tile_cond]
        sh1, sc1, g1, sh2, sc2, g2, sh3, sc3, g3 = [mods[:, i:i + 1, :] for i in range(N_MOD)]

        def ffn(x, idx, shv, scv, gv):
            wi = ffn_w_in[l, idx]
            w2 = jnp.stack([_pad_cols(wi[:, :d_ff], ffp), _pad_cols(wi[:, d_ff:], ffp)]).astype(BF16)
            wo = jnp.pad(ffn_w_out[l, idx], ((0, ffp - d_ff), (0, 0))).astype(BF16)
            h = _norm_mod_call(x, norm_g[l, idx * 2], shv, scv, tm)
            a = _mm_swiglu_call(h, w2, tm_wide)
            return _mm_res_call(a, wo, x, gv, 0.5, tm)

        x = ffn(x, 0, sh1, sc1, g1)

        h = _norm_mod_call(x, norm_g[l, 1], sh2, sc2, tm)
        wl_in = w_in[l]
        offs = np.cumsum([0, rw, rw, rw, lora, sw, sw, gn, gn, sh, cw, 3 * dm])
        segs = [wl_in[:, offs[i]:offs[i + 1]].astype(BF16) for i in range(11)]
        w_small = jnp.concatenate([segs[3], segs[6], segs[7], _pad_cols(segs[8], LANES)], axis=1)
        assert lora % gn == 0 and (lora + 2 * gn) % LANES == 0
        bm_blk, cm_blk, dt_blk = lora // gn, lora // gn + 1, (lora + 2 * gn) // LANES
        zr, zk, zv, zz, zxs, zc, zg, zs = [_mm_call(h, wseg, tm_wide)
                                           for wseg in (segs[0], segs[1], segs[2], segs[4], segs[5],
                                                        segs[9], segs[10], w_small)]
        dtt = zs[:, lora + 2 * gn:lora + 2 * gn + sh].T

        mu = rwkv_mu[l]
        pw_r = dict(ranks=(rd, ra, rg), sel=sel, selt=selt,
                    mu_r=mu[None, :rw], mu_k=mu[None, rw:2 * rw], mu_v=mu[None, 2 * rw:3 * rw],
                    mu_l=mu[None, 3 * rw:],
                    k_k=rwkv_k_k[l][None], k_a=rwkv_k_a[l][None], r_k=rwkv_r_k[l].reshape(1, rw),
                    w0=rwkv_w0[l][:, None, :], a0=rwkv_a0[l][:, None, :],
                    w2=rwkv_w2[l].astype(BF16), a2=rwkv_a2[l].astype(BF16), g2=rwkv_g2[l].astype(BF16),
                    ln_g=rwkv_ln_g[l][None], ln_b=rwkv_ln_b[l][None])
        ya_parts, fin_a = [], None
        bpb = (LANES // 2) // rh
        for gi, gr in enumerate(groups):
            b_, t_ = gr["b"], gr["t"]
            nblk = b_ // bpb
            rv_, wb_, kk_, bonus_, g_ = _rwkv_prep_call(
                zr, zk, zv, zs, gr["row0"], gr["n"], t_, gr["grid_w"], pw_r)
            if gi == 0:
                s0c = jnp.zeros((RWKV_HEAD, RWKV_HEAD, nblk * LANES), F32)
            else:
                s0c = state_rwkv[:, l].reshape(nblk, bpb, N_DIR, rh, RWKV_HEAD, RWKV_HEAD)
                s0c = s0c.transpose(5, 4, 0, 2, 1, 3).reshape(RWKV_HEAD, RWKV_HEAD, nblk * LANES)
            o_f, o_b, sf_c = _rwkv_scan_call(rv_, wb_, kk_, s0c)
            ya_parts.append(_rwkv_post_call(o_f, o_b, bonus_, g_, pw_r))
            if gi == 0:
                fin_a = sf_c.reshape(RWKV_HEAD, RWKV_HEAD, nblk, N_DIR, bpb, rh)
                fin_a = fin_a.transpose(2, 4, 3, 5, 1, 0).reshape(b_, N_DIR, rh, RWKV_HEAD, RWKV_HEAD)
        y_a = jnp.concatenate(ya_parts)
        new_a.append(fin_a)

        cw_l, cb_l = ssd_conv_w[l], ssd_conv_b[l]
        pw_s = dict(heads=sh, tri=tri, trit=trit, expand=expand,
                    bias=jnp.pad(ssd_dt_bias[l], ((0, 0), (0, LANES - sh)))[:, None, :],
                    alog=jnp.pad(ssd_a_log[l], ((0, 0), (0, LANES - sh)))[:, None, :],
                    bias_t=ssd_dt_bias[l][:, :, None], alog_t=ssd_a_log[l][:, :, None])
        yb_parts, fin_b = [], None
        for gi, gr in enumerate(groups):
            xs_ = _conv_silu_call(zxs, gr["row0"], gr["n"], gr["t"], cw_l[:, :sw], cb_l[:sw])
            bm_ = _conv_silu_call(zs, gr["row0"], gr["n"], gr["t"], cw_l[:, sw:sw + gn], cb_l[sw:sw + gn], bm_blk)
            cm_ = _conv_silu_call(zs, gr["row0"], gr["n"], gr["t"], cw_l[:, sw + gn:], cb_l[sw + gn:], cm_blk)
            s0s = None if gi == 0 else state_ssd[:, l].reshape(gr["b"], N_DIR, sw, gn // SSD_GROUPS)
            ydir, sf_s = _ssd_scan_call(xs_, bm_, cm_, zs, dt_blk, dtt, gr["row0"], gr["b"], gr["t"], pw_s, s0s)
            yb_parts.append(_ssd_post_call(xs_, ydir, zz, gr["row0"],
                                           jnp.repeat(ssd_d[l], sw // sh)[None, :], ssd_norm_g[l][None, :], tm))
            if gi == 0:
                fin_b = sf_s.reshape(gr["b"], N_DIR, sh, sw // sh, gn // SSD_GROUPS)
        y_b = jnp.concatenate(yb_parts)
        new_b.append(fin_b)

        pw_c = _s5_weights(p, l, slab_groups)
        yc_parts, fin_re, fin_im = [], None, None
        for gi, gr in enumerate(groups):
            b_, t_ = gr["b"], gr["t"]
            assert gr["row0"] % t_ == 0
            if gi == 0:
                s0 = jnp.zeros((N_DIR, 2, b_, s5_g * s5_p), F32)
            else:
                s0 = jnp.stack([state_s5_re[:, l], state_s5_im[:, l]])
                s0 = s0.reshape(2, b_, N_DIR, s5_g * s5_p).transpose(2, 0, 1, 3)
            y_dir, sf = _s5_call(zc.reshape(m // t_, t_, cw), gr["row0"] // t_, b_, pw_c, s0)
            y_dir = y_dir.reshape(N_DIR, gr["n"], cw)
            yc_parts.append(_s5_post_call(zc, gr["row0"], y_dir, s5_d[l][None, :], tm))
            if gi == 0:
                fin = sf.reshape(N_DIR, 2, b_, s5_g, s5_p).transpose(1, 2, 0, 3, 4)
                fin_re, fin_im = fin[0], fin[1]
        y_c = jnp.concatenate(yc_parts)
        new_re.append(fin_re)
        new_im.append(fin_im)

        wc = w_proj_c[l]
        merged = _merge_call(y_a, y_b, y_c, zg, w_proj_a[l].astype(BF16), w_proj_b[l].astype(BF16),
                             jnp.stack([wc[:, :dm], wc[:, dm:]]).astype(BF16), tm)
        x = _mm_res_call(merged, w_out[l].astype(BF16), x, g2, 1.0, tm)

        x = ffn(x, 1, sh3, sc3, g3)

    y_p = _final_norm_call(x, 0, n_p, final_norm_g, tm)
    y_s = _final_norm_call(x, n_p, n_s, final_norm_g, tm)
    return (y_p.reshape(bp, tp, dm), y_s.reshape(bs, ts, dm),
            jnp.stack(new_a, axis=1), jnp.stack(new_b, axis=1),
            jnp.stack(new_re, axis=1), jnp.stack(new_im, axis=1))
```

```python
import functools
import math

import jax
import jax.numpy as jnp
import numpy as np
from jax import lax
from jax.experimental import pallas as pl
from jax.experimental.pallas import tpu as pltpu

F32 = jnp.float32
BF16 = jnp.bfloat16

LANES = 128
SUBLANES = 8
VMEM_LIMIT = 56 * 1024 * 1024

GRID_W = 64
SSD_CHUNK = 128
SSD_GROUPS = 2
N_DIR = 2
N_MOD = 9
EPS = 1e-6
RWKV_LN_EPS = 64e-5
RWKV_HEAD = 64
RWKV_SCAN_TB = 32
RWKV_SCAN_PARTS = 1
S5_TB = 64


def _cparams(sem):
    return pltpu.CompilerParams(dimension_semantics=sem, vmem_limit_bytes=VMEM_LIMIT)


def _pick(n, target, mult=LANES):
    best = None
    d = mult
    while d <= min(n, target):
        if n % d == 0:
            best = d
        d += mult
    return n if best is None else best


def _seq_rows(seq_len, nrows, row0, cap=2048):
    span = math.gcd(nrows, row0) if row0 else nrows
    assert span % seq_len == 0
    return _pick(span, max(cap, seq_len), seq_len)


def _round_up(n, m):
    return -(-n // m) * m


def _dot(a, b):
    return jnp.dot(a, b, preferred_element_type=F32)


def _split3(x):
    x1 = x.astype(BF16)
    r1 = x - x1.astype(F32)
    x2 = r1.astype(BF16)
    x3 = (r1 - x2.astype(F32)).astype(BF16)
    return x1, x2, x3


def _dot_exact_r(x, sel):
    return sum(_dot(p, sel) for p in _split3(x))


def _dot_exact_l(sel, x):
    return sum(_dot(sel, p) for p in _split3(x))


def _softplus(x):
    return jnp.maximum(x, 0.0) + jnp.log1p(jnp.exp(-jnp.abs(x)))


def _sigmoid(x):
    return 0.5 * (jnp.tanh(0.5 * x) + 1.0)


def _silu(x):
    return x * _sigmoid(x)


def _mod_kernel(c_ref, w_ref, b_ref, o_ref):
    c = c_ref[...]
    a = _silu(c).astype(BF16)
    o_ref[...] = _dot(a, w_ref[0].astype(BF16)) + b_ref[0]


def _mod_call(cond8, w_all, b_all, layer):
    depth, d, n = w_all.shape
    tn = _pick(n, 1024)
    return pl.pallas_call(
        _mod_kernel,
        out_shape=jax.ShapeDtypeStruct((cond8.shape[0], n), F32),
        grid=(n // tn,),
        in_specs=[pl.BlockSpec((cond8.shape[0], d), lambda j: (0, 0)),
                  pl.BlockSpec((1, d, tn), lambda j: (layer, 0, j)),
                  pl.BlockSpec((1, 1, tn), lambda j: (layer, 0, j))],
        out_specs=pl.BlockSpec((cond8.shape[0], tn), lambda j: (0, j)),
        compiler_params=_cparams(("parallel",)),
        name="mod_proj",
    )(cond8, w_all, b_all.reshape(depth, 1, n))


def _norm_mod_kernel(x_ref, g_ref, sh_ref, sc_ref, o_ref):
    x = x_ref[...]
    y = x * lax.rsqrt(jnp.mean(x * x, axis=-1, keepdims=True) + EPS) * g_ref[...]
    o_ref[...] = (y * (1.0 + sc_ref[0]) + sh_ref[0]).astype(o_ref.dtype)


def _norm_mod_call(x, g, sh, sc, tm):
    m, d = x.shape
    return pl.pallas_call(
        _norm_mod_kernel,
        out_shape=jax.ShapeDtypeStruct((m, d), BF16),
        grid=(m // tm,),
        in_specs=[pl.BlockSpec((tm, d), lambda i: (i, 0)),
                  pl.BlockSpec((1, d), lambda i: (0, 0)),
                  pl.BlockSpec((1, 1, d), lambda i: (i, 0, 0)),
                  pl.BlockSpec((1, 1, d), lambda i: (i, 0, 0))],
        out_specs=pl.BlockSpec((tm, d), lambda i: (i, 0)),
        compiler_params=_cparams(("parallel",)),
        name="norm_mod",
    )(x, g.reshape(1, d), sh, sc)


def _final_norm_kernel(x_ref, g_ref, o_ref):
    x = x_ref[...]
    o_ref[...] = x * lax.rsqrt(jnp.mean(x * x, axis=-1, keepdims=True) + EPS) * g_ref[...]


def _final_norm_call(x, row0, nrows, g, tm):
    d = x.shape[1]
    r0 = row0 // tm
    return pl.pallas_call(
        _final_norm_kernel,
        out_shape=jax.ShapeDtypeStruct((nrows, d), F32),
        grid=(nrows // tm,),
        in_specs=[pl.BlockSpec((tm, d), lambda i: (r0 + i, 0)),
                  pl.BlockSpec((1, d), lambda i: (0, 0))],
        out_specs=pl.BlockSpec((tm, d), lambda i: (i, 0)),
        compiler_params=_cparams(("parallel",)),
        name="final_norm",
    )(x, g.reshape(1, d))


def _mm_kernel(a_ref, w_ref, o_ref):
    o_ref[...] = _dot(a_ref[...], w_ref[...]).astype(o_ref.dtype)


def _mm_call(a, w, tm, tn_target=1024, out_dtype=F32):
    m, k = a.shape
    n = w.shape[1]
    tn = _pick(n, tn_target)
    return pl.pallas_call(
        _mm_kernel,
        out_shape=jax.ShapeDtypeStruct((m, n), out_dtype),
        grid=(m // tm, n // tn),
        in_specs=[pl.BlockSpec((tm, k), lambda i, j: (i, 0)),
                  pl.BlockSpec((k, tn), lambda i, j: (0, j))],
        out_specs=pl.BlockSpec((tm, tn), lambda i, j: (i, j)),
        compiler_params=_cparams(("parallel", "parallel")),
        name="mm",
    )(a, w)


def _mm_swiglu_kernel(a_ref, wg_ref, wu_ref, o_ref):
    a = a_ref[...]
    gate = _dot(a, wg_ref[...])
    up = _dot(a, wu_ref[...])
    o_ref[...] = (_silu(gate) * up).astype(o_ref.dtype)


def _mm_swiglu_call(a, w_gate_up, w_up, tm, tn_target=512):
    m, k = a.shape
    n = w_up.shape[1]
    tn = tn_target if n >= tn_target else n
    return pl.pallas_call(
        _mm_swiglu_kernel,
        out_shape=jax.ShapeDtypeStruct((m, n), BF16),
        grid=(m // tm, pl.cdiv(n, tn)),
        in_specs=[pl.BlockSpec((tm, k), lambda i, j: (i, 0)),
                  pl.BlockSpec((k, tn), lambda i, j: (0, j)),
                  pl.BlockSpec((k, tn), lambda i, j: (0, j))],
        out_specs=pl.BlockSpec((tm, tn), lambda i, j: (i, j)),
        compiler_params=_cparams(("parallel", "parallel")),
        name="mm_swiglu",
    )(a, w_gate_up, w_up)


def _mm_res_kernel(a_ref, w_ref, x_ref, g_ref, o_ref, *, coef):
    o_ref[...] = x_ref[...] + (coef * g_ref[0]) * _dot(a_ref[...], w_ref[...])


def _mm_res_call(a, w, x, gate, coef, tm, tn_target=512):
    m, k = a.shape
    n = w.shape[1]
    tn = _pick(n, tn_target)
    return pl.pallas_call(
        functools.partial(_mm_res_kernel, coef=coef),
        out_shape=jax.ShapeDtypeStruct((m, n), F32),
        grid=(m // tm, n // tn),
        in_specs=[pl.BlockSpec((tm, k), lambda i, j: (i, 0)),
                  pl.BlockSpec((k, tn), lambda i, j: (0, j)),
                  pl.BlockSpec((tm, tn), lambda i, j: (i, j)),
                  pl.BlockSpec((1, 1, tn), lambda i, j: (i, 0, j))],
        out_specs=pl.BlockSpec((tm, tn), lambda i, j: (i, j)),
        compiler_params=_cparams(("parallel", "parallel")),
        name="mm_res",
    )(a, w, x, gate)


def _merge_kernel(ya_ref, yb_ref, yc_ref, ga_ref, gb_ref, gc_ref, wa_ref, wb_ref, wc_ref, o_ref):
    pa = _dot(ya_ref[...], wa_ref[...])
    pb = _dot(yb_ref[...], wb_ref[...])
    yc = yc_ref[...]
    val = _dot(yc, wc_ref[0])
    gate = _dot(yc, wc_ref[1])
    merged = (_sigmoid(ga_ref[...]) * pa + _sigmoid(gb_ref[...]) * pb
              + _sigmoid(gc_ref[...]) * (val * _sigmoid(gate)))
    o_ref[...] = merged.astype(o_ref.dtype)


def _merge_call(ya, yb, yc, zg, wa, wb, wc2, tm, tn_target=512):
    m, ka = ya.shape
    d = wa.shape[1]
    tn = _pick(d, tn_target)
    nj = d // tn
    return pl.pallas_call(
        _merge_kernel,
        out_shape=jax.ShapeDtypeStruct((m, d), BF16),
        grid=(m // tm, nj),
        in_specs=[pl.BlockSpec((tm, ka), lambda i, j: (i, 0)),
                  pl.BlockSpec((tm, yb.shape[1]), lambda i, j: (i, 0)),
                  pl.BlockSpec((tm, yc.shape[1]), lambda i, j: (i, 0)),
                  pl.BlockSpec((tm, tn), lambda i, j: (i, j)),
                  pl.BlockSpec((tm, tn), lambda i, j: (i, nj + j)),
                  pl.BlockSpec((tm, tn), lambda i, j: (i, 2 * nj + j)),
                  pl.BlockSpec((ka, tn), lambda i, j: (0, j)),
                  pl.BlockSpec((yb.shape[1], tn), lambda i, j: (0, j)),
                  pl.BlockSpec((2, yc.shape[1], tn), lambda i, j: (0, 0, j))],
        out_specs=pl.BlockSpec((tm, tn), lambda i, j: (i, j)),
        compiler_params=_cparams(("parallel", "parallel")),
        name="merge",
    )(ya, yb, yc, zg, zg, zg, wa, wb, wc2)


def _row_shift(x, off, t_idx, seq_len):
    rows = x.shape[0]
    rolled = pltpu.roll(x, (-off) % rows, axis=0)
    src = t_idx + off
    ok = jnp.logical_and(src >= 0, src < seq_len)
    return jnp.where(ok, rolled, 0.0)


def _centred_nb(x, t_idx, seq_len, grid_w):
    if grid_w is None:
        return 0.5 * (_row_shift(x, -1, t_idx, seq_len) + _row_shift(x, 1, t_idx, seq_len))
    col = t_idx % grid_w
    left = jnp.where(col >= 1, _row_shift(x, -1, t_idx, seq_len), 0.0)
    right = jnp.where(col < grid_w - 1, _row_shift(x, 1, t_idx, seq_len), 0.0)
    up = _row_shift(x, -grid_w, t_idx, seq_len)
    down = _row_shift(x, grid_w, t_idx, seq_len)
    return 0.25 * (up + down + left + right)


def _rwkv_prep_kernel(zr_ref, zk_ref, zv_ref, zl_ref,
                      mur_ref, muk_ref, muv_ref, mul_ref,
                      kk_w_ref, ka_w_ref, rk_w_ref, w0_ref, a0_ref,
                      w2_ref, a2_ref, g2_ref, sel_ref, selt_ref,
                      rv_o, wb_o, kk_o, bonus_o, g_o,
                      *, seq_len, grid_w, ranks):
    nseq = rv_o.shape[0]

    def emit(write, a, b):
        lane = lax.broadcasted_iota(jnp.int32, a.shape, 1)
        first = lane < RWKV_HEAD
        head0 = jnp.where(first, a, pltpu.roll(b, RWKV_HEAD, axis=1))
        head1 = jnp.where(first, pltpu.roll(a, RWKV_HEAD, axis=1), b)
        for q in range(nseq):
            write(q, 0, head0[q * seq_len:(q + 1) * seq_len])
            write(q, 1, head1[q * seq_len:(q + 1) * seq_len])

    def shifted(ref, mu_ref):
        x = ref[...]
        t_idx = lax.broadcasted_iota(jnp.int32, x.shape, 0) % seq_len
        return x + mu_ref[...] * (_centred_nb(x, t_idx, seq_len, grid_w) - x)

    r = shifted(zr_ref, mur_ref)
    k = shifted(zk_ref, muk_ref)
    v = shifted(zv_ref, muv_ref)
    lo = shifted(zl_ref, mul_ref)
    rd, ra, rg = ranks
    wl = lo[:, :rd]
    al = lo[:, rd:rd + ra]
    gl = lo[:, rd + ra:rd + ra + rg]
    sel = sel_ref[...]
    selt = selt_ref[...]

    def head_sum(x):
        return _dot_exact_r(_dot_exact_r(x, sel), selt)

    kk = k * kk_w_ref[...]
    kk = kk * lax.rsqrt(head_sum(kk * kk) + 1e-12)
    def write_rv(q, hh, val):
        rv_o[q, hh] = val

    emit(write_rv, r, v)
    tw = jnp.tanh(wl).astype(BF16)
    alb = al.astype(BF16)
    for d in range(N_DIR):
        w_log = -_softplus(-(w0_ref[d] + _dot(tw, w2_ref[d]))) - 0.5
        a_d = _sigmoid(a0_ref[d] + _dot(alb, a2_ref[d]))

        def write_wb(q, hh, val, d=d):
            wb_o[d, q, hh] = val

        def write_kk(q, hh, val, d=d):
            kk_o[d, q, hh] = val

        emit(write_wb, jnp.exp(-jnp.exp(w_log)), kk * a_d)
        emit(write_kk, k * (1.0 + (a_d - 1.0) * ka_w_ref[...]), kk)
    bonus_o[...] = head_sum(r * k * rk_w_ref[...]) * v
    g_o[...] = _dot(_sigmoid(gl).astype(BF16), g2_ref[...])


def _rwkv_prep_call(zr, zk, zv, zl, row0, nrows, seq_len, grid_w, pw):
    w = zr.shape[1]
    lw = sum(pw["ranks"])
    cb = LANES
    rb = _seq_rows(seq_len, nrows, row0)
    assert w % cb == 0
    r0 = row0 // rb
    main = pl.BlockSpec((rb, cb), lambda i, j: (r0 + i, j))
    lspec = pl.BlockSpec((rb, lw), lambda i, j: (r0 + i, 0))
    colp = pl.BlockSpec((1, cb), lambda i, j: (0, j))
    dirp = pl.BlockSpec((N_DIR, 1, cb), lambda i, j: (0, 0, j))
    ospec = pl.BlockSpec((rb, cb), lambda i, j: (i, j))
    rd, ra, rg = pw["ranks"]
    nsel = pw["sel"].shape[1]
    assert cb == 2 * RWKV_HEAD
    nseq = rb // seq_len
    bsz = nrows // seq_len
    heads = w // RWKV_HEAD
    pk = jax.ShapeDtypeStruct((bsz, heads, seq_len, cb), F32)
    pkd = jax.ShapeDtypeStruct((N_DIR, bsz, heads, seq_len, cb), F32)
    pspec = pl.BlockSpec((nseq, 2, seq_len, cb), lambda i, j: (i, j, 0, 0))
    pdspec = pl.BlockSpec((N_DIR, nseq, 2, seq_len, cb), lambda i, j: (0, i, j, 0, 0))
    return pl.pallas_call(
        functools.partial(_rwkv_prep_kernel, seq_len=seq_len, grid_w=grid_w, ranks=pw["ranks"]),
        out_shape=[pk, pkd, pkd, jax.ShapeDtypeStruct((nrows, w), F32), jax.ShapeDtypeStruct((nrows, w), F32)],
        grid=(nrows // rb, w // cb),
        in_specs=[main, main, main, lspec,
                  colp, colp, colp, pl.BlockSpec((1, lw), lambda i, j: (0, 0)),
                  colp, colp, colp, dirp, dirp,
                  pl.BlockSpec((N_DIR, rd, cb), lambda i, j: (0, 0, j)),
                  pl.BlockSpec((N_DIR, ra, cb), lambda i, j: (0, 0, j)),
                  pl.BlockSpec((rg, cb), lambda i, j: (0, j)),
                  pl.BlockSpec((cb, nsel), lambda i, j: (0, 0)),
                  pl.BlockSpec((nsel, cb), lambda i, j: (0, 0))],
        out_specs=[pspec, pdspec, pdspec, ospec, ospec],
        compiler_params=_cparams(("parallel", "parallel")),
        name="rwkv_prep",
    )(zr, zk, zv, zl, pw["mu_r"], pw["mu_k"], pw["mu_v"], pw["mu_l"],
      pw["k_k"], pw["k_a"], pw["r_k"], pw["w0"], pw["a0"],
      pw["w2"], pw["a2"], pw["g2"], pw["sel"], pw["selt"])


def _load_time_chunk(ref, t0):
    idx = (0,) * (len(ref.shape) - 4) + (slice(None), slice(None), pl.ds(t0, SUBLANES), slice(None))
    x = ref[idx]
    return x.reshape(x.shape[0] * x.shape[1], SUBLANES, x.shape[-1])


def _store_time_rows(ref, s, val):
    steps, width = ref.shape[-2], ref.shape[-1]
    chains = math.prod(ref.shape[:-2])
    ref.reshape(chains * steps, width)[pl.ds(s, chains, stride=steps), :] = val


def _rwkv_scan_kernel(rva_ref, rvb_ref, wba_ref, wbb_ref, kka_ref, kkb_ref, s0_ref,
                      oa_ref, ob_ref, sf_ref, s_ref, out_ref, *ops_refs):
    i = pl.program_id(1)
    n = s_ref.shape[0]
    nj = n // SUBLANES
    steps = out_ref.shape[0]
    half = LANES // 2
    r_off, v_off, w_off, b_off, kd_off, kk_off = (q * n for q in range(6))

    @pl.when(i == 0)
    def _():
        s_ref[...] = s0_ref[...]

    srcs = ((rva_ref, rvb_ref), (wba_ref, wbb_ref), (kka_ref, kkb_ref))

    nparts = len(ops_refs)
    plen = steps // nparts

    def relayout(part, c):
        t0 = pl.multiple_of(part * plen + c * SUBLANES, SUBLANES)
        tb0 = pl.multiple_of(steps - SUBLANES - (part * plen + c * SUBLANES), SUBLANES)
        for p, (a_ref, b_ref) in enumerate(srcs):
            xa = jnp.swapaxes(_load_time_chunk(a_ref, t0), 0, 1)
            xb = jnp.swapaxes(_load_time_chunk(b_ref, tb0), 0, 1)
            for q in range(SUBLANES):
                x = jnp.concatenate([xa[q], xb[SUBLANES - 1 - q]], axis=0)
                ops_refs[part][c * SUBLANES + q, pl.ds(p * LANES, LANES), :] = x.T

    def unlay(t):
        tb_ = steps - 1 - t
        ot = out_ref[t].T
        _store_time_rows(oa_ref, t, ot[:half])
        _store_time_rows(ob_ref, tb_, ot[half:])

    out_ref[0] = jnp.zeros(out_ref.shape[1:], F32)

    def relayout_chunk(c, carry):
        relayout(0, c)
        return carry

    lax.fori_loop(0, plen // SUBLANES, relayout_chunk, 0)

    def state(k, j):
        return s_ref[k, pl.ds(j * SUBLANES, SUBLANES), :]

    for part in range(nparts):
        ops_ref = ops_refs[part]

        def row(s, r, ops_ref=ops_ref):
            return jnp.broadcast_to(ops_ref[s, pl.ds(r, 1), :], (SUBLANES, LANES))

        acc0 = [None] * nj
        for k in range(n):
            kkb = row(0, kk_off + k)
            for j in range(nj):
                p = state(k, j) * kkb
                acc0[j] = p if acc0[j] is None else acc0[j] + p

        def step(s, acc, part=part, ops_ref=ops_ref, row=row):
            t = part * plen + s
            unlay(jnp.maximum(t - 1, 0))
            s_next = jnp.minimum(s + 1, plen - 1)
            vv = [ops_ref[s, pl.ds(v_off + j * SUBLANES, SUBLANES), :] for j in range(nj)]
            out = [None] * nj
            acc_next = [None] * nj
            for k in range(n):
                wb = row(s, w_off + k)
                bb = row(s, b_off + k)
                kdb = row(s, kd_off + k)
                rb = row(s, r_off + k)
                kkn = row(s_next, kk_off + k)
                for j in range(nj):
                    s_new = state(k, j) * wb - acc[j] * bb + vv[j] * kdb
                    s_ref[k, pl.ds(j * SUBLANES, SUBLANES), :] = s_new
                    q = s_new * rb
                    out[j] = q if out[j] is None else out[j] + q
                    p = s_new * kkn
                    acc_next[j] = p if acc_next[j] is None else acc_next[j] + p
            for j in range(nj):
                out_ref[t, pl.ds(j * SUBLANES, SUBLANES), :] = out[j]
            return tuple(acc_next)

        lax.fori_loop(0, plen, step, tuple(acc0))
    unlay(steps - 1)

    @pl.when(i == pl.num_programs(1) - 1)
    def _():
        sf_ref[...] = s_ref[...]


def _rwkv_scan_call(rv, wb, kk, s0):
    bsz, heads, t, _ = rv.shape
    n = RWKV_HEAD
    bpb = (LANES // 2) // heads
    assert bpb * heads * 2 == LANES and bsz % bpb == 0
    nblk = bsz // bpb
    tb = min(RWKV_SCAN_TB, t)
    nparts = RWKV_SCAN_PARTS
    assert t % tb == 0 and tb % nparts == 0
    nt = t // tb
    fwd4 =pl.BlockSpec((bpb, heads, tb, LANES), lambda g, i: (g, 0, i, 0))
    bwd4 = pl.BlockSpec((bpb, heads, tb, LANES), lambda g, i: (g, 0, nt - 1 - i, 0))
    fwd5 = pl.BlockSpec((1, bpb, heads, tb, LANES), lambda g, i: (0, g, 0, i, 0))
    bwd5 = pl.BlockSpec((1, bpb, heads, tb, LANES), lambda g, i: (1, g, 0, nt - 1 - i, 0))
    st = pl.BlockSpec((n, n, LANES), lambda g, i: (0, 0, g))
    o_sds = jax.ShapeDtypeStruct((bsz, heads, t, n), F32)
    return pl.pallas_call(
        _rwkv_scan_kernel,
        out_shape=[o_sds, o_sds, jax.ShapeDtypeStruct((n, n, nblk * LANES), F32)],
        grid=(nblk, nt),
        in_specs=[fwd4, bwd4, fwd5, bwd5, fwd5, bwd5, st],
        out_specs=[pl.BlockSpec((bpb, heads, tb, n), lambda g, i: (g, 0, i, 0)),
                   pl.BlockSpec((bpb, heads, tb, n), lambda g, i: (g, 0, nt - 1 - i, 0)),
                   st],
        scratch_shapes=[pltpu.VMEM((n, n, LANES), F32),
                        pltpu.VMEM((tb, n, LANES), F32)]
        + [pltpu.VMEM((tb // nparts, 6 * n, LANES), F32)] * nparts,
        compiler_params=_cparams(("parallel", "arbitrary")),
        name="rwkv_scan",
    )(rv, rv, wb, wb, kk, kk, s0)


def _rwkv_post_kernel(of_ref, ob_ref, bonus_ref, g_ref, lng_ref, lnb_ref, sel_ref, selt_ref, o_ref,
                      *, head):
    sel = sel_ref[...]
    selt = selt_ref[...]

    def head_sum(x):
        return _dot_exact_r(_dot_exact_r(x, sel), selt)

    def rows(ref):
        parts = [jnp.concatenate([ref[q, 0], ref[q, 1]], axis=1) for q in range(ref.shape[0])]
        return parts[0] if len(parts) == 1 else jnp.concatenate(parts, axis=0)

    o = rows(of_ref) + rows(ob_ref)
    mu = head_sum(o) * (1.0 / head)
    dlt = o - mu
    var = head_sum(dlt * dlt) * (1.0 / head)
    on = dlt * lax.rsqrt(var + RWKV_LN_EPS) * lng_ref[...] + lnb_ref[...]
    o_ref[...] = ((on + bonus_ref[...]) * g_ref[...]).astype(o_ref.dtype)


def _rwkv_post_call(of, ob, bonus, g, pw):
    bsz, heads, t, n = of.shape
    m, w = bonus.shape
    cb = LANES
    nsel = pw["sel"].shape[1]
    tm = _seq_rows(t, m, 0)
    nseq = tm // t
    main = pl.BlockSpec((tm, cb), lambda i, j: (i, j))
    colp = pl.BlockSpec((1, cb), lambda i, j: (0, j))
    ospec = pl.BlockSpec((nseq, 2, t, n), lambda i, j: (i, j, 0, 0))
    return pl.pallas_call(
        functools.partial(_rwkv_post_kernel, head=RWKV_HEAD),
        out_shape=jax.ShapeDtypeStruct((m, w), BF16),
        grid=(m // tm, w // cb),
        in_specs=[ospec, ospec, main, main, colp, colp,
                  pl.BlockSpec((cb, nsel), lambda i, j: (0, 0)),
                  pl.BlockSpec((nsel, cb), lambda i, j: (0, 0))],
        out_specs=main,
        compiler_params=_cparams(("parallel", "parallel")),
        name="rwkv_post",
    )(of, ob, bonus, g, pw["ln_g"], pw["ln_b"], pw["sel"], pw["selt"])


def _conv_silu_kernel(x_ref, w_ref, b_ref, o_ref, *, seq_len):
    x = x_ref[...]
    rows = x.shape[0]
    kw = w_ref.shape[0]
    t_idx = lax.broadcasted_iota(jnp.int32, x.shape, 0) % seq_len
    y = b_ref[...] + jnp.zeros_like(x)
    for j in range(kw):
        off = j - kw // 2
        xs = x if off == 0 else _row_shift(x, off, t_idx, seq_len)
        y = y + w_ref[pl.ds(j, 1), :] * xs
    o_ref[...] = _silu(y)


def _conv_silu_call(x, row0, nrows, seq_len, w, b, col_blk=0):
    kw, c = w.shape
    cb = _pick(c, 256)
    rb = _seq_rows(seq_len, nrows, row0)
    r0 = row0 // rb
    j0 = col_blk * (c // cb)
    return pl.pallas_call(
        functools.partial(_conv_silu_kernel, seq_len=seq_len),
        out_shape=jax.ShapeDtypeStruct((nrows, c), F32),
        grid=(nrows // rb, c // cb),
        in_specs=[pl.BlockSpec((rb, cb), lambda i, j: (r0 + i, j0 + j)),
                  pl.BlockSpec((kw, cb), lambda i, j: (0, j)),
                  pl.BlockSpec((1, cb), lambda i, j: (0, j))],
        out_specs=pl.BlockSpec((rb, cb), lambda i, j: (i, j)),
        compiler_params=_cparams(("parallel", "parallel")),
        name="ssd_conv",
    )(x, w, b.reshape(1, c))


def _ssd_scan_kernel(x_ref, b_ref, c_ref, dt_ref, dtt_ref, bias_ref, alog_ref, biast_ref, alogt_ref,
                     tri_ref, trit_ref, e_ref, s0_ref, y_ref, sf_ref, st_ref,
                     *, heads, hdim, nstate, groups, has_init):
    d = pl.program_id(0)
    c = pl.program_id(2)
    nc = pl.num_programs(2)
    hpg = heads // groups
    gw = hpg * hdim

    @pl.when(c == 0)
    def _():
        if has_init:
            for g in range(groups):
                st_ref[pl.ds(g * nstate, nstate), :] = s0_ref[0, 0, pl.ds(g * gw, gw), :].T
        else:
            st_ref[...] = jnp.zeros_like(st_ref)

    tri = tri_ref[0]
    trit = trit_ref[0]
    lch = tri.shape[0]
    e = e_ref[...]
    dtp = _softplus(dt_ref[...] + bias_ref[0])
    a = -jnp.exp(alog_ref[0])
    da = dtp * a
    cum = _dot_exact_l(tri, da)
    dat = _softplus(dtt_ref[...] + biast_ref[0]) * (-jnp.exp(alogt_ref[0]))
    cumt = _dot_exact_r(dat, trit)
    total = jnp.sum(da, axis=0, keepdims=True)
    tot8 = jnp.broadcast_to(total, (SUBLANES, LANES))
    dt_full = _dot_exact_r(dtp, e)
    din_full = _dot_exact_r(jnp.exp(cum), e)
    dst_full = _dot_exact_r(jnp.exp(total - cum), e)
    tot_full = _dot_exact_r(jnp.exp(tot8), e)[0:1, :]
    x = x_ref[...]
    xdt = x * dt_full
    xdec = (xdt * dst_full).astype(BF16)
    xdt_b = xdt.astype(BF16)
    bm = b_ref[...]
    cm = c_ref[...]
    visible = tri > 0
    for g in range(groups):
        bg = bm[:, g * nstate:(g + 1) * nstate]
        cg = cm[:, g * nstate:(g + 1) * nstate].astype(BF16)
        bgb = bg.astype(BF16)
        cb = lax.dot_general(cg, bgb, (((1,), (1,)), ((), ())), preferred_element_type=F32)
        st_g = st_ref[pl.ds(g * nstate, nstate), :]
        y_off = _dot(cg, st_g.astype(BF16)) * din_full[:, g * gw:(g + 1) * gw]
        for hh in range(hpg):
            h = g * hpg + hh
            seg = cum[:, h:h + 1] - cumt[h:h + 1, :]
            lmat = jnp.exp(jnp.where(visible, seg, -jnp.inf))
            gmat = (cb * lmat).astype(BF16)
            yd = _dot(gmat, xdt_b[:, h * hdim:(h + 1) * hdim])
            y_ref[0, :, pl.ds(h * hdim, hdim)] = yd + y_off[:, hh * hdim:(hh + 1) * hdim]
        upd = _dot(bg.T.astype(BF16), xdec[:, g * gw:(g + 1) * gw])
        st_ref[pl.ds(g * nstate, nstate), :] = st_g * tot_full[:, g * gw:(g + 1) * gw] + upd

    @pl.when(c == nc - 1)
    def _():
        for g in range(groups):
            sf_ref[0, 0, pl.ds(g * gw, gw), :] = st_ref[pl.ds(g * nstate, nstate), :].T


def _ssd_scan_call(xs, bm, cm, zdt, dt_blk, dtt, row0, bsz, seq_len, pw, s0):
    hp = xs.shape[1]
    gn = bm.shape[1]
    heads = pw["heads"]
    hdim = hp // heads
    nstate = gn // SSD_GROUPS
    lch = min(SSD_CHUNK, seq_len)
    nc = seq_len // lch
    assert seq_len % lch == 0 and row0 % lch == 0
    c0 = row0 // lch
    has_init = s0 is not None
    if s0 is None:
        s0 = jnp.zeros((1, 1, hp, nstate), F32)

    def cidx(d, b, c):
        return b * nc + c + d * (nc - 1 - 2 * c)

    row = lambda d, b, c: (cidx(d, b, c), 0)
    s0_map = (lambda d, b, c: (b, d, 0, 0)) if has_init else (lambda d, b, c: (0, 0, 0, 0))
    dirp = pl.BlockSpec((1, 1, LANES), lambda d, b, c: (d, 0, 0))
    dirt = pl.BlockSpec((1, heads, 1), lambda d, b, c: (d, 0, 0))
    return pl.pallas_call(
        functools.partial(_ssd_scan_kernel, heads=heads, hdim=hdim, nstate=nstate,
                          groups=SSD_GROUPS, has_init=has_init),
        out_shape=[jax.ShapeDtypeStruct((N_DIR, bsz * seq_len, hp), F32),
                   jax.ShapeDtypeStruct((bsz, N_DIR, hp, nstate), F32)],
        grid=(N_DIR, bsz, nc),
        in_specs=[pl.BlockSpec((lch, hp), row),
                  pl.BlockSpec((lch, gn), row),
                  pl.BlockSpec((lch, gn), row),
                  pl.BlockSpec((lch, LANES), lambda d, b, c: (c0 + cidx(d, b, c), dt_blk)),
                  pl.BlockSpec((heads, lch), lambda d, b, c: (0, c0 + cidx(d, b, c))),
                  dirp, dirp, dirt, dirt,
                  pl.BlockSpec((1, lch, lch), lambda d, b, c: (d, 0, 0)),
                  pl.BlockSpec((1, lch, lch), lambda d, b, c: (d, 0, 0)),
                  pl.BlockSpec((LANES, hp), lambda d, b, c: (0, 0)),
                  pl.BlockSpec((1, 1, hp, nstate), s0_map)],
        out_specs=[pl.BlockSpec((1, lch, hp), lambda d, b, c: (d, cidx(d, b, c), 0)),
                   pl.BlockSpec((1, 1, hp, nstate), lambda d, b, c: (b, d, 0, 0))],
        scratch_shapes=[pltpu.VMEM((gn, hp // SSD_GROUPS), F32)],
        compiler_params=_cparams(("arbitrary", "arbitrary", "arbitrary")),
        name="ssd_scan",
    )(xs, bm, cm, zdt, dtt, pw["bias"], pw["alog"], pw["bias_t"], pw["alog_t"],
      pw["tri"][:, :lch, :lch], pw["trit"][:, :lch, :lch], pw["expand"], s0)


def _ssd_post_kernel(x_ref, yf_ref, yb_ref, z_ref, d_ref, g_ref, o_ref):
    y = (d_ref[...] * x_ref[...] + yf_ref[0] + yb_ref[0]) * _silu(z_ref[...])
    y = y * lax.rsqrt(jnp.mean(y * y, axis=-1, keepdims=True) + EPS) * g_ref[...]
    o_ref[...] = y.astype(o_ref.dtype)


def _ssd_post_call(xs, ydir, zz, row0, d_full, g, tm):
    n, hp = xs.shape
    r0 = row0 // tm
    return pl.pallas_call(
        _ssd_post_kernel,
        out_shape=jax.ShapeDtypeStruct((n, hp), BF16),
        grid=(n // tm,),
        in_specs=[pl.BlockSpec((tm, hp), lambda i: (i, 0)),
                  pl.BlockSpec((1, tm, hp), lambda i: (0, i, 0)),
                  pl.BlockSpec((1, tm, hp), lambda i: (1, i, 0)),
                  pl.BlockSpec((tm, hp), lambda i: (r0 + i, 0)),
                  pl.BlockSpec((1, hp), lambda i: (0, 0)),
                  pl.BlockSpec((1, hp), lambda i: (0, 0))],
        out_specs=pl.BlockSpec((tm, hp), lambda i: (i, 0)),
        compiler_params=_cparams(("parallel",)),
        name="ssd_post",
    )(xs, ydir, ydir, zz, d_full, g)


def _s5_kernel(u_ref, bmat_ref, cmat_ref, lam_ref, s0_ref, y_ref, sf_ref,
               st_ref, buf_ref, *, slab_in, slab_state):
    d = pl.program_id(0)
    tb = pl.program_id(2)
    ntb = pl.num_programs(2)
    nb, steps = u_ref.shape[0], u_ref.shape[1]
    nslab = u_ref.shape[2] // slab_in
    spl = buf_ref.shape[0]
    ncol = slab_state // LANES
    assert slab_in == LANES

    @pl.when(tb == 0)
    def _():
        st_ref[...] = s0_ref[0]

    for s_base in range(0, nslab, spl):
        lam = []
        init = []
        for q in range(spl):
            s = s_base + q
            u = jnp.swapaxes(u_ref[:, :, pl.ds(s * slab_in, slab_in)], 0, 1)
            bu = _dot(u.reshape(steps * nb, slab_in).astype(BF16), bmat_ref[0, s])
            for c in range(2 * ncol):
                buf_ref[q, c] = bu[:, c * LANES:(c + 1) * LANES]
            for c in range(ncol):
                lanes = pl.ds(s * slab_state + c * LANES, LANES)
                lam.append((jnp.broadcast_to(lam_ref[0, 0, :, lanes], (nb, LANES)),
                            jnp.broadcast_to(lam_ref[0, 1, :, lanes], (nb, LANES))))
                init.append(st_ref[0, :, lanes])
                init.append(st_ref[1, :, lanes])

        def step(i, carry):
            te = i + d * (steps - 1 - 2 * i)
            rows = pl.ds(pl.multiple_of(te * nb, nb), nb)
            new = []
            for q in range(spl):
                for c in range(ncol):
                    lr, li = lam[q * ncol + c]
                    s_re = carry[2 * (q * ncol + c)]
                    s_im = carry[2 * (q * ncol + c) + 1]
                    n_re = lr * s_re - li * s_im + buf_ref[q, c, rows, :]
                    n_im = lr * s_im + li * s_re + buf_ref[q, ncol + c, rows, :]
                    buf_ref[q, c, rows, :] = n_re
                    buf_ref[q, ncol + c, rows, :] = n_im
                    new += [n_re, n_im]
            return tuple(new)

        fin = lax.fori_loop(0, steps, step, tuple(init), unroll=4)
        for q in range(spl):
            s = s_base + q
            for c in range(ncol):
                lanes = pl.ds(s * slab_state + c * LANES, LANES)
                st_ref[0, :, lanes] = fin[2 * (q * ncol + c)]
                st_ref[1, :, lanes] = fin[2 * (q * ncol + c) + 1]
            states = jnp.concatenate([buf_ref[q, c] for c in range(2 * ncol)], axis=1)
            y = _dot(states.astype(BF16), cmat_ref[0, s])
            y_ref[0, :, :, pl.ds(s * slab_in, slab_in)] = jnp.swapaxes(y.reshape(steps, nb, slab_in), 0, 1)

    @pl.when(tb == ntb - 1)
    def _():
        sf_ref[0] = st_ref[...]


def _s5_call(u3, seq0, bsz, pw, s0):
    _, t, w = u3.shape
    nb = min(SUBLANES, bsz)
    tb = min(S5_TB, t)
    assert t % tb == 0 and bsz % nb == 0 and seq0 % nb == 0
    ntb = t // tb
    b0 = seq0 // nb
    gp = pw["lam"].shape[-1]
    slab_in = pw["slab_in"]
    slab_state = pw["slab_state"]
    nslab = w // slab_in
    spl = 2 if nslab % 2 == 0 else 1

    def tidx(d, i):
        return i + d * (ntb - 1 - 2 * i)

    return pl.pallas_call(
        functools.partial(_s5_kernel, slab_in=slab_in, slab_state=slab_state),
        out_shape=[jax.ShapeDtypeStruct((N_DIR, bsz, t, w), F32),
                   jax.ShapeDtypeStruct((N_DIR, 2, bsz, gp), F32)],
        grid=(N_DIR, bsz // nb, ntb),
        in_specs=[pl.BlockSpec((nb, tb, w), lambda d, b, i: (b0 + b, tidx(d, i), 0)),
                  pl.BlockSpec((1, nslab, slab_in, 2 * slab_state), lambda d, b, i: (d, 0, 0, 0)),
                  pl.BlockSpec((1, nslab, 2 * slab_state, slab_in), lambda d, b, i: (d, 0, 0, 0)),
                  pl.BlockSpec((1, 2, 1, gp), lambda d, b, i: (d, 0, 0, 0)),
                  pl.BlockSpec((1, 2, nb, gp), lambda d, b, i: (d, 0, b, 0))],
        out_specs=[pl.BlockSpec((1, nb, tb, w), lambda d, b, i: (d, b, tidx(d, i), 0)),
                   pl.BlockSpec((1, 2, nb, gp), lambda d, b, i: (d, 0, b, 0))],
        scratch_shapes=[pltpu.VMEM((2, nb, gp), F32),
                        pltpu.VMEM((spl, 2 * slab_state // LANES, nb * tb, LANES), F32)],
        compiler_params=_cparams(("arbitrary", "arbitrary", "arbitrary")),
        name="s5_scan",
    )(u3, pw["bmat"], pw["cmat"], pw["lam"], s0)


def _s5_post_kernel(u_ref, yf_ref, yb_ref, d_ref, o_ref):
    y = d_ref[...] * u_ref[...] + yf_ref[0] + yb_ref[0]
    o_ref[...] = jax.nn.gelu(y).astype(o_ref.dtype)


def _s5_post_call(u, row0, ydir, d_full, tm):
    _, m, w = ydir.shape
    r0 = row0 // tm
    return pl.pallas_call(
        _s5_post_kernel,
        out_shape=jax.ShapeDtypeStruct((m, w), BF16),
        grid=(m // tm,),
        in_specs=[pl.BlockSpec((tm, w), lambda i: (r0 + i, 0)),
                  pl.BlockSpec((1, tm, w), lambda i: (0, i, 0)),
                  pl.BlockSpec((1, tm, w), lambda i: (1, i, 0)),
                  pl.BlockSpec((1, w), lambda i: (0, 0))],
        out_specs=pl.BlockSpec((tm, w), lambda i: (i, 0)),
        compiler_params=_cparams(("parallel",)),
        name="s5_post",
    )(u, ydir, ydir, d_full)


def _head_selectors(cb, head):
    nsel = LANES
    col = np.arange(cb)[:, None] // head
    sel = (col == np.arange(nsel)[None, :]).astype(np.float32)
    return jnp.asarray(sel, BF16), jnp.asarray(sel.T, BF16)


def _s5_weights(p, l, slab_groups):
    g, pst = p["s5_lambda_re"].shape[2:]
    cg = p["s5_b_re"].shape[-1]
    nslab = g // slab_groups
    eye = jnp.eye(slab_groups, dtype=F32)
    bmats, cmats, lams = [], [], []
    for d in range(N_DIR):
        lam_re, lam_im = p["s5_lambda_re"][l, d], p["s5_lambda_im"][l, d]
        delta = jnp.exp(p["s5_log_dt"][l, d])[:, None]
        mag = jnp.exp(lam_re * delta)
        lb_re, lb_im = mag * jnp.cos(lam_im * delta), mag * jnp.sin(lam_im * delta)
        den = lam_re * lam_re + lam_im * lam_im
        q_re = ((lb_re - 1.0) * lam_re + lb_im * lam_im) / den
        q_im = (lb_im * lam_re - (lb_re - 1.0) * lam_im) / den
        b_re, b_im = p["s5_b_re"][l, d], p["s5_b_im"][l, d]
        bb_re = q_re[..., None] * b_re - q_im[..., None] * b_im
        bb_im = q_re[..., None] * b_im + q_im[..., None] * b_re

        def in_blocks(bb):
            x = bb.reshape(nslab, slab_groups, pst, cg)
            return jnp.einsum("sgpc,gh->sgchp", x, eye).reshape(nslab, slab_groups * cg, slab_groups * pst)

        def out_blocks(cc):
            x = cc.reshape(nslab, slab_groups, cg, pst)
            return jnp.einsum("sgcp,gh->sgphc", x, eye).reshape(nslab, slab_groups * pst, slab_groups * cg)

        bmats.append(jnp.concatenate([in_blocks(bb_re), in_blocks(bb_im)], axis=-1))
        cmats.append(jnp.concatenate([out_blocks(p["s5_c_re"][l, d]), -out_blocks(p["s5_c_im"][l, d])], axis=-2))
        lams.append(jnp.stack([lb_re.reshape(1, g * pst), lb_im.reshape(1, g * pst)]))
    return dict(bmat=jnp.stack(bmats).astype(BF16), cmat=jnp.stack(cmats).astype(BF16),
                lam=jnp.stack(lams), slab_in=slab_groups * cg, slab_state=slab_groups * pst)


def _pad_cols(w, n):
    return jnp.pad(w, ((0, 0), (0, n - w.shape[1])))


def kernel(x_prompt, x_sample, state_rwkv, state_ssd, state_s5_re, state_s5_im, c, c_ctx, w_mod, b_mod, norm_g, ffn_w_in, ffn_w_out, w_in, rwkv_mu, rwkv_w0, rwkv_w2, rwkv_a0, rwkv_a2, rwkv_g2, rwkv_k_k, rwkv_k_a, rwkv_r_k, rwkv_ln_g, rwkv_ln_b, w_proj_a, ssd_conv_w, ssd_conv_b, ssd_dt_bias, ssd_a_log, ssd_d, ssd_norm_g, w_proj_b, s5_lambda_re, s5_lambda_im, s5_log_dt, s5_b_re, s5_b_im, s5_c_re, s5_c_im, s5_d, w_proj_c, w_out, final_norm_g):
    p = dict(s5_lambda_re=s5_lambda_re, s5_lambda_im=s5_lambda_im, s5_log_dt=s5_log_dt,
             s5_b_re=s5_b_re, s5_b_im=s5_b_im, s5_c_re=s5_c_re, s5_c_im=s5_c_im)
    bp, tp, dm = x_prompt.shape
    bs, ts, _ = x_sample.shape
    n_p, n_s = bp * tp, bs * ts
    m = n_p + n_s
    depth = w_mod.shape[0]
    d_ff = ffn_w_out.shape[2]
    ffp = _round_up(d_ff, 512)
    rw = rwkv_k_k.shape[1]
    rh = rw // RWKV_HEAD
    rd, ra, rg = rwkv_w2.shape[2], rwkv_a2.shape[2], rwkv_g2.shape[1]
    lora = rd + ra + rg
    sh = ssd_d.shape[1]
    sw = ssd_norm_g.shape[1]
    xbc_w = ssd_conv_w.shape[2]
    gn = (xbc_w - sw) // 2
    cw = s5_d.shape[1]
    s5_g, s5_p = s5_lambda_re.shape[2:]
    s5_cg = cw // s5_g
    slab_groups = max(1, min(s5_g, LANES // s5_cg))

    tm = _pick(math.gcd(n_p, ts), 1024, SUBLANES)
    tm_wide = _pick(m, 2 * tm, tm)
    n_tiles = m // tm
    tile_cond = np.array([0 if i * tm < n_p else 1 + (i * tm - n_p) // ts for i in range(n_tiles)])

    x = jnp.concatenate([x_prompt.reshape(n_p, dm), x_sample.reshape(n_s, dm)], axis=0)
    ncond = 1 + bs
    cond = jnp.concatenate([c_ctx[None, :], c], axis=0)
    cond8 = jnp.pad(cond, ((0, _round_up(ncond, SUBLANES) - ncond), (0, 0)))

    groups = [dict(b=bp, t=tp, row0=0, n=n_p, grid_w=None),
              dict(b=bs, t=ts, row0=n_p, n=n_s, grid_w=GRID_W)]

    sel, selt = _head_selectors(LANES, RWKV_HEAD)
    heads_per_blk = LANES // RWKV_HEAD
    tri_f = np.tril(np.ones((SSD_CHUNK, SSD_CHUNK), np.float32))
    tri = jnp.asarray(np.stack([tri_f, tri_f.T]), BF16)
    trit = jnp.asarray(np.stack([tri_f.T, tri_f]), BF16)
    expand = jnp.asarray((np.arange(LANES)[:, None] == (np.arange(sw)[None, :] // (sw // sh))).astype(np.float32), BF16)

    new_a, new_b, new_re, new_im = [], [], [], []
    for l in range(depth):
        mod = _mod_call(cond8, w_mod, b_mod, l)
        mods = mod.reshape(-1, N_MOD, dm)[tile_cond]
        sh1, sc1, g1, sh2, sc2, g2, sh3, sc3, g3 = [mods[:, i:i + 1, :] for i in range(N_MOD)]

        def ffn(x, idx, shv, scv, gv):
            wi = ffn_w_in[l, idx].astype(BF16)
            wo = ffn_w_out[l, idx].astype(BF16)
            h = _norm_mod_call(x, norm_g[l, idx * 2], shv, scv, tm)
            a = _mm_swiglu_call(h, wi, wi[:, d_ff:], tm)
            return _mm_res_call(a, wo, x, gv, 0.5, tm)

        x = ffn(x, 0, sh1, sc1, g1)

        h = _norm_mod_call(x, norm_g[l, 1], sh2, sc2, tm)
        wl_in = w_in[l]
        offs = np.cumsum([0, rw, rw, rw, lora, sw, sw, gn, gn, sh, cw, 3 * dm])
        segs = [wl_in[:, offs[i]:offs[i + 1]].astype(BF16) for i in range(11)]
        w_small = jnp.concatenate([segs[3], segs[6], segs[7], _pad_cols(segs[8], LANES)], axis=1)
        assert lora % gn == 0 and (lora + 2 * gn) % LANES == 0
        bm_blk, cm_blk, dt_blk = lora // gn, lora // gn + 1, (lora + 2 * gn) // LANES
        zr, zk, zv, zz, zxs, zc, zg, zs = [_mm_call(h, wseg, tm_wide)
                                           for wseg in (segs[0], segs[1], segs[2], segs[4], segs[5],
                                                        segs[9], segs[10], w_small)]
        dtt = zs[:, lora + 2 * gn:lora + 2 * gn + sh].T

        mu = rwkv_mu[l]
        pw_r = dict(ranks=(rd, ra, rg), sel=sel, selt=selt,
                    mu_r=mu[None, :rw], mu_k=mu[None, rw:2 * rw], mu_v=mu[None, 2 * rw:3 * rw],
                    mu_l=mu[None, 3 * rw:],
                    k_k=rwkv_k_k[l][None], k_a=rwkv_k_a[l][None], r_k=rwkv_r_k[l].reshape(1, rw),
                    w0=rwkv_w0[l][:, None, :], a0=rwkv_a0[l][:, None, :],
                    w2=rwkv_w2[l].astype(BF16), a2=rwkv_a2[l].astype(BF16), g2=rwkv_g2[l].astype(BF16),
                    ln_g=rwkv_ln_g[l][None], ln_b=rwkv_ln_b[l][None])
        ya_parts, fin_a = [], None
        bpb = (LANES // 2) // rh
        for gi, gr in enumerate(groups):
            b_, t_ = gr["b"], gr["t"]
            nblk = b_ // bpb
            rv_, wb_, kk_, bonus_, g_ = _rwkv_prep_call(
                zr, zk, zv, zs, gr["row0"], gr["n"], t_, gr["grid_w"], pw_r)
            if gi == 0:
                s0c = jnp.zeros((RWKV_HEAD, RWKV_HEAD, nblk * LANES), F32)
            else:
                s0c = state_rwkv[:, l].reshape(nblk, bpb, N_DIR, rh, RWKV_HEAD, RWKV_HEAD)
                s0c = s0c.transpose(5, 4, 0, 2, 1, 3).reshape(RWKV_HEAD, RWKV_HEAD, nblk * LANES)
            o_f, o_b, sf_c = _rwkv_scan_call(rv_, wb_, kk_, s0c)
            ya_parts.append(_rwkv_post_call(o_f, o_b, bonus_, g_, pw_r))
            if gi == 0:
                fin_a = sf_c.reshape(RWKV_HEAD, RWKV_HEAD, nblk, N_DIR, bpb, rh)
                fin_a = fin_a.transpose(2, 4, 3, 5, 1, 0).reshape(b_, N_DIR, rh, RWKV_HEAD, RWKV_HEAD)
        y_a = jnp.concatenate(ya_parts)
        new_a.append(fin_a)

        cw_l, cb_l = ssd_conv_w[l], ssd_conv_b[l]
        pw_s = dict(heads=sh, tri=tri, trit=trit, expand=expand,
                    bias=jnp.pad(ssd_dt_bias[l], ((0, 0), (0, LANES - sh)))[:, None, :],
                    alog=jnp.pad(ssd_a_log[l], ((0, 0), (0, LANES - sh)))[:, None, :],
                    bias_t=ssd_dt_bias[l][:, :, None], alog_t=ssd_a_log[l][:, :, None])
        yb_parts, fin_b = [], None
        for gi, gr in enumerate(groups):
            xs_ = _conv_silu_call(zxs, gr["row0"], gr["n"], gr["t"], cw_l[:, :sw], cb_l[:sw])
            bm_ = _conv_silu_call(zs, gr["row0"], gr["n"], gr["t"], cw_l[:, sw:sw + gn], cb_l[sw:sw + gn], bm_blk)
            cm_ = _conv_silu_call(zs, gr["row0"], gr["n"], gr["t"], cw_l[:, sw + gn:], cb_l[sw + gn:], cm_blk)
            s0s = None if gi == 0 else state_ssd[:, l].reshape(gr["b"], N_DIR, sw, gn // SSD_GROUPS)
            ydir, sf_s = _ssd_scan_call(xs_, bm_, cm_, zs, dt_blk, dtt, gr["row0"], gr["b"], gr["t"], pw_s, s0s)
            yb_parts.append(_ssd_post_call(xs_, ydir, zz, gr["row0"],
                                           jnp.repeat(ssd_d[l], sw // sh)[None, :], ssd_norm_g[l][None, :], tm))
            if gi == 0:
                fin_b = sf_s.reshape(gr["b"], N_DIR, sh, sw // sh, gn // SSD_GROUPS)
        y_b = jnp.concatenate(yb_parts)
        new_b.append(fin_b)

        pw_c = _s5_weights(p, l, slab_groups)
        yc_parts, fin_re, fin_im = [], None, None
        for gi, gr in enumerate(groups):
            b_, t_ = gr["b"], gr["t"]
            assert gr["row0"] % t_ == 0
            if gi == 0:
                s0 = jnp.zeros((N_DIR, 2, b_, s5_g * s5_p), F32)
            else:
                s0 = jnp.stack([state_s5_re[:, l], state_s5_im[:, l]])
                s0 = s0.reshape(2, b_, N_DIR, s5_g * s5_p).transpose(2, 0, 1, 3)
            y_dir, sf = _s5_call(zc.reshape(m // t_, t_, cw), gr["row0"] // t_, b_, pw_c, s0)
            y_dir = y_dir.reshape(N_DIR, gr["n"], cw)
            yc_parts.append(_s5_post_call(zc, gr["row0"], y_dir, s5_d[l][None, :], tm))
            if gi == 0:
                fin = sf.reshape(N_DIR, 2, b_, s5_g, s5_p).transpose(1, 2, 0, 3, 4)
                fin_re, fin_im = fin[0], fin[1]
        y_c = jnp.concatenate(yc_parts)
        new_re.append(fin_re)
        new_im.append(fin_im)

        wc = w_proj_c[l]
        merged = _merge_call(y_a, y_b, y_c, zg, w_proj_a[l].astype(BF16), w_proj_b[l].astype(BF16),
                             jnp.stack([wc[:, :dm], wc[:, dm:]]).astype(BF16), tm)
        x = _mm_res_call(merged, w_out[l].astype(BF16), x, g2, 1.0, tm)

        x = ffn(x, 1, sh3, sc3, g3)

    y_p = _final_norm_call(x, 0, n_p, final_norm_g, tm)
    y_s = _final_norm_call(x, n_p, n_s, final_norm_g, tm)
    return (y_p.reshape(bp, tp, dm), y_s.reshape(bs, ts, dm),
            jnp.stack(new_a, axis=1), jnp.stack(new_b, axis=1),
            jnp.stack(new_re, axis=1), jnp.stack(new_im, axis=1))
```

```python
import functools
import math

import jax
import jax.numpy as jnp
import numpy as np
from jax import lax
from jax.experimental import pallas as pl
from jax.experimental.pallas import tpu as pltpu

F32 = jnp.float32
BF16 = jnp.bfloat16

LANES = 128
SUBLANES = 8
VMEM_LIMIT = 56 * 1024 * 1024

GRID_W = 64
SSD_CHUNK = 128
SSD_GROUPS = 2
N_DIR = 2
N_MOD = 9
EPS = 1e-6
RWKV_LN_EPS = 64e-5
RWKV_HEAD = 64
RWKV_SCAN_TB = 32
RWKV_SCAN_PARTS = 1
S5_TB = 64


def _cparams(sem):
    return pltpu.CompilerParams(dimension_semantics=sem, vmem_limit_bytes=VMEM_LIMIT)


def _pick(n, target, mult=LANES):
    best = None
    d = mult
    while d <= min(n, target):
        if n % d == 0:
            best = d
        d += mult
    return n if best is None else best


def _seq_rows(seq_len, nrows, row0, cap=2048):
    span = math.gcd(nrows, row0) if row0 else nrows
    assert span % seq_len == 0
    return _pick(span, max(cap, seq_len), seq_len)


def _round_up(n, m):
    return -(-n // m) * m


def _dot(a, b):
    return jnp.dot(a, b, preferred_element_type=F32)


def _split3(x):
    x1 = x.astype(BF16)
    r1 = x - x1.astype(F32)
    x2 = r1.astype(BF16)
    x3 = (r1 - x2.astype(F32)).astype(BF16)
    return x1, x2, x3


def _dot_exact_r(x, sel):
    return sum(_dot(p, sel) for p in _split3(x))


def _dot_exact_l(sel, x):
    return sum(_dot(sel, p) for p in _split3(x))


def _softplus(x):
    return jnp.maximum(x, 0.0) + jnp.log1p(jnp.exp(-jnp.abs(x)))


def _sigmoid(x):
    return 0.5 * (jnp.tanh(0.5 * x) + 1.0)


def _silu(x):
    return x * _sigmoid(x)


def _mod_kernel(c_ref, w_ref, b_ref, o_ref):
    c = c_ref[...]
    a = _silu(c).astype(BF16)
    o_ref[...] = _dot(a, w_ref[0].astype(BF16)) + b_ref[0]


def _mod_call(cond8, w_all, b_all, layer):
    depth, d, n = w_all.shape
    tn = _pick(n, 1024)
    return pl.pallas_call(
        _mod_kernel,
        out_shape=jax.ShapeDtypeStruct((cond8.shape[0], n), F32),
        grid=(n // tn,),
        in_specs=[pl.BlockSpec((cond8.shape[0], d), lambda j: (0, 0)),
                  pl.BlockSpec((1, d, tn), lambda j: (layer, 0, j)),
                  pl.BlockSpec((1, 1, tn), lambda j: (layer, 0, j))],
        out_specs=pl.BlockSpec((cond8.shape[0], tn), lambda j: (0, j)),
        compiler_params=_cparams(("parallel",)),
        name="mod_proj",
    )(cond8, w_all, b_all.reshape(depth, 1, n))


def _norm_mod_kernel(x_ref, g_ref, sh_ref, sc_ref, o_ref):
    x = x_ref[...]
    y = x * lax.rsqrt(jnp.mean(x * x, axis=-1, keepdims=True) + EPS) * g_ref[...]
    o_ref[...] = (y * (1.0 + sc_ref[0]) + sh_ref[0]).astype(o_ref.dtype)


def _norm_mod_call(x, g, sh, sc, tm):
    m, d = x.shape
    return pl.pallas_call(
        _norm_mod_kernel,
        out_shape=jax.ShapeDtypeStruct((m, d), BF16),
        grid=(m // tm,),
        in_specs=[pl.BlockSpec((tm, d), lambda i: (i, 0)),
                  pl.BlockSpec((1, d), lambda i: (0, 0)),
                  pl.BlockSpec((1, 1, d), lambda i: (i, 0, 0)),
                  pl.BlockSpec((1, 1, d), lambda i: (i, 0, 0))],
        out_specs=pl.BlockSpec((tm, d), lambda i: (i, 0)),
        compiler_params=_cparams(("parallel",)),
        name="norm_mod",
    )(x, g.reshape(1, d), sh, sc)


def _final_norm_kernel(x_ref, g_ref, o_ref):
    x = x_ref[...]
    o_ref[...] = x * lax.rsqrt(jnp.mean(x * x, axis=-1, keepdims=True) + EPS) * g_ref[...]


def _final_norm_call(x, row0, nrows, g, tm):
    d = x.shape[1]
    r0 = row0 // tm
    return pl.pallas_call(
        _final_norm_kernel,
        out_shape=jax.ShapeDtypeStruct((nrows, d), F32),
        grid=(nrows // tm,),
        in_specs=[pl.BlockSpec((tm, d), lambda i: (r0 + i, 0)),
                  pl.BlockSpec((1, d), lambda i: (0, 0))],
        out_specs=pl.BlockSpec((tm, d), lambda i: (i, 0)),
        compiler_params=_cparams(("parallel",)),
        name="final_norm",
    )(x, g.reshape(1, d))


def _mm_kernel(a_ref, w_ref, o_ref):
    o_ref[...] = _dot(a_ref[...], w_ref[...]).astype(o_ref.dtype)


def _mm_call(a, w, tm, tn_target=1024, out_dtype=F32):
    m, k = a.shape
    n = w.shape[1]
    tn = _pick(n, tn_target)
    return pl.pallas_call(
        _mm_kernel,
        out_shape=jax.ShapeDtypeStruct((m, n), out_dtype),
        grid=(m // tm, n // tn),
        in_specs=[pl.BlockSpec((tm, k), lambda i, j: (i, 0)),
                  pl.BlockSpec((k, tn), lambda i, j: (0, j))],
        out_specs=pl.BlockSpec((tm, tn), lambda i, j: (i, j)),
        compiler_params=_cparams(("parallel", "parallel")),
        name="mm",
    )(a, w)


def _mm_swiglu_kernel(a_ref, wg_ref, wu_ref, o_ref):
    a = a_ref[...]
    gate = _dot(a, wg_ref[...])
    up = _dot(a, wu_ref[...])
    o_ref[...] = (_silu(gate) * up).astype(o_ref.dtype)


def _mm_swiglu_call(a, w_gate_up, w_up, tm, tn_target=512):
    m, k = a.shape
    n = w_up.shape[1]
    tn = tn_target if n >= tn_target else n
    return pl.pallas_call(
        _mm_swiglu_kernel,
        out_shape=jax.ShapeDtypeStruct((m, n), BF16),
        grid=(m // tm, pl.cdiv(n, tn)),
        in_specs=[pl.BlockSpec((tm, k), lambda i, j: (i, 0)),
                  pl.BlockSpec((k, tn), lambda i, j: (0, j)),
                  pl.BlockSpec((k, tn), lambda i, j: (0, j))],
        out_specs=pl.BlockSpec((tm, tn), lambda i, j: (i, j)),
        compiler_params=_cparams(("parallel", "parallel")),
        name="mm_swiglu",
    )(a, w_gate_up, w_up)


def _mm_res_kernel(a_ref, w_ref, x_ref, g_ref, o_ref, *, coef):
    o_ref[...] = x_ref[...] + (coef * g_ref[0]) * _dot(a_ref[...], w_ref[...])


def _mm_res_call(a, w, x, gate, coef, tm, tn_target=512):
    m, k = a.shape
    n = w.shape[1]
    tn = _pick(n, tn_target)
    return pl.pallas_call(
        functools.partial(_mm_res_kernel, coef=coef),
        out_shape=jax.ShapeDtypeStruct((m, n), F32),
        grid=(m // tm, n // tn),
        in_specs=[pl.BlockSpec((tm, k), lambda i, j: (i, 0)),
                  pl.BlockSpec((k, tn), lambda i, j: (0, j)),
                  pl.BlockSpec((tm, tn), lambda i, j: (i, j)),
                  pl.BlockSpec((1, 1, tn), lambda i, j: (i, 0, j))],
        out_specs=pl.BlockSpec((tm, tn), lambda i, j: (i, j)),
        compiler_params=_cparams(("parallel", "parallel")),
        name="mm_res",
    )(a, w, x, gate)


def _merge_kernel(ya_ref, yb_ref, yc_ref, ga_ref, gb_ref, gc_ref, wa_ref, wb_ref, wc_ref, o_ref):
    pa = _dot(ya_ref[...], wa_ref[...])
    pb = _dot(yb_ref[...], wb_ref[...])
    yc = yc_ref[...]
    val = _dot(yc, wc_ref[0])
    gate = _dot(yc, wc_ref[1])
    merged = (_sigmoid(ga_ref[...]) * pa + _sigmoid(gb_ref[...]) * pb
              + _sigmoid(gc_ref[...]) * (val * _sigmoid(gate)))
    o_ref[...] = merged.astype(o_ref.dtype)


def _merge_call(ya, yb, yc, zg, wa, wb, wc2, tm, tn_target=512):
    m, ka = ya.shape
    d = wa.shape[1]
    tn = _pick(d, tn_target)
    nj = d // tn
    return pl.pallas_call(
        _merge_kernel,
        out_shape=jax.ShapeDtypeStruct((m, d), BF16),
        grid=(m // tm, nj),
        in_specs=[pl.BlockSpec((tm, ka), lambda i, j: (i, 0)),
                  pl.BlockSpec((tm, yb.shape[1]), lambda i, j: (i, 0)),
                  pl.BlockSpec((tm, yc.shape[1]), lambda i, j: (i, 0)),
                  pl.BlockSpec((tm, tn), lambda i, j: (i, j)),
                  pl.BlockSpec((tm, tn), lambda i, j: (i, nj + j)),
                  pl.BlockSpec((tm, tn), lambda i, j: (i, 2 * nj + j)),
                  pl.BlockSpec((ka, tn), lambda i, j: (0, j)),
                  pl.BlockSpec((yb.shape[1], tn), lambda i, j: (0, j)),
                  pl.BlockSpec((2, yc.shape[1], tn), lambda i, j: (0, 0, j))],
        out_specs=pl.BlockSpec((tm, tn), lambda i, j: (i, j)),
        compiler_params=_cparams(("parallel", "parallel")),
        name="merge",
    )(ya, yb, yc, zg, zg, zg, wa, wb, wc2)


def _row_shift(x, off, t_idx, seq_len):
    rows = x.shape[0]
    rolled = pltpu.roll(x, (-off) % rows, axis=0)
    src = t_idx + off
    ok = jnp.logical_and(src >= 0, src < seq_len)
    return jnp.where(ok, rolled, 0.0)


def _centred_nb(x, t_idx, seq_len, grid_w):
    if grid_w is None:
        return 0.5 * (_row_shift(x, -1, t_idx, seq_len) + _row_shift(x, 1, t_idx, seq_len))
    col = t_idx % grid_w
    left = jnp.where(col >= 1, _row_shift(x, -1, t_idx, seq_len), 0.0)
    right = jnp.where(col < grid_w - 1, _row_shift(x, 1, t_idx, seq_len), 0.0)
    up = _row_shift(x, -grid_w, t_idx, seq_len)
    down = _row_shift(x, grid_w, t_idx, seq_len)
    return 0.25 * (up + down + left + right)


def _rwkv_prep_kernel(zr_ref, zk_ref, zv_ref, zl_ref,
                      mur_ref, muk_ref, muv_ref, mul_ref,
                      kk_w_ref, ka_w_ref, rk_w_ref, w0_ref, a0_ref,
                      w2_ref, a2_ref, g2_ref, sel_ref, selt_ref,
                      rv_o, wb_o, kk_o, bonus_o, g_o,
                      *, seq_len, grid_w, ranks):
    nseq = rv_o.shape[0]

    def emit(write, a, b):
        lane = lax.broadcasted_iota(jnp.int32, a.shape, 1)
        first = lane < RWKV_HEAD
        head0 = jnp.where(first, a, pltpu.roll(b, RWKV_HEAD, axis=1))
        head1 = jnp.where(first, pltpu.roll(a, RWKV_HEAD, axis=1), b)
        for q in range(nseq):
            write(q, 0, head0[q * seq_len:(q + 1) * seq_len])
            write(q, 1, head1[q * seq_len:(q + 1) * seq_len])

    def shifted(ref, mu_ref):
        x = ref[...]
        t_idx = lax.broadcasted_iota(jnp.int32, x.shape, 0) % seq_len
        return x + mu_ref[...] * (_centred_nb(x, t_idx, seq_len, grid_w) - x)

    r = shifted(zr_ref, mur_ref)
    k = shifted(zk_ref, muk_ref)
    v = shifted(zv_ref, muv_ref)
    lo = shifted(zl_ref, mul_ref)
    rd, ra, rg = ranks
    wl = lo[:, :rd]
    al = lo[:, rd:rd + ra]
    gl = lo[:, rd + ra:rd + ra + rg]
    sel = sel_ref[...]
    selt = selt_ref[...]

    def head_sum(x):
        return _dot_exact_r(_dot_exact_r(x, sel), selt)

    kk = k * kk_w_ref[...]
    kk = kk * lax.rsqrt(head_sum(kk * kk) + 1e-12)
    def write_rv(q, hh, val):
        rv_o[q, hh] = val

    emit(write_rv, r, v)
    tw = jnp.tanh(wl).astype(BF16)
    alb = al.astype(BF16)
    for d in range(N_DIR):
        w_log = -_softplus(-(w0_ref[d] + _dot(tw, w2_ref[d]))) - 0.5
        a_d = _sigmoid(a0_ref[d] + _dot(alb, a2_ref[d]))

        def write_wb(q, hh, val, d=d):
            wb_o[d, q, hh] = val

        def write_kk(q, hh, val, d=d):
            kk_o[d, q, hh] = val

        emit(write_wb, jnp.exp(-jnp.exp(w_log)), kk * a_d)
        emit(write_kk, k * (1.0 + (a_d - 1.0) * ka_w_ref[...]), kk)
    bonus_o[...] = head_sum(r * k * rk_w_ref[...]) * v
    g_o[...] = _dot(_sigmoid(gl).astype(BF16), g2_ref[...])


def _rwkv_prep_call(zr, zk, zv, zl, row0, nrows, seq_len, grid_w, pw):
    w = zr.shape[1]
    lw = sum(pw["ranks"])
    cb = LANES
    rb = _seq_rows(seq_len, nrows, row0)
    assert w % cb == 0
    r0 = row0 // rb
    main = pl.BlockSpec((rb, cb), lambda i, j: (r0 + i, j))
    lspec = pl.BlockSpec((rb, lw), lambda i, j: (r0 + i, 0))
    colp = pl.BlockSpec((1, cb), lambda i, j: (0, j))
    dirp = pl.BlockSpec((N_DIR, 1, cb), lambda i, j: (0, 0, j))
    ospec = pl.BlockSpec((rb, cb), lambda i, j: (i, j))
    rd, ra, rg = pw["ranks"]
    nsel = pw["sel"].shape[1]
    assert cb == 2 * RWKV_HEAD
    nseq = rb // seq_len
    bsz = nrows // seq_len
    heads = w // RWKV_HEAD
    pk = jax.ShapeDtypeStruct((bsz, heads, seq_len, cb), F32)
    pkd = jax.ShapeDtypeStruct((N_DIR, bsz, heads, seq_len, cb), F32)
    pspec = pl.BlockSpec((nseq, 2, seq_len, cb), lambda i, j: (i, j, 0, 0))
    pdspec = pl.BlockSpec((N_DIR, nseq, 2, seq_len, cb), lambda i, j: (0, i, j, 0, 0))
    return pl.pallas_call(
        functools.partial(_rwkv_prep_kernel, seq_len=seq_len, grid_w=grid_w, ranks=pw["ranks"]),
        out_shape=[pk, pkd, pkd, jax.ShapeDtypeStruct((nrows, w), F32), jax.ShapeDtypeStruct((nrows, w), F32)],
        grid=(nrows // rb, w // cb),
        in_specs=[main, main, main, lspec,
                  colp, colp, colp, pl.BlockSpec((1, lw), lambda i, j: (0, 0)),
                  colp, colp, colp, dirp, dirp,
                  pl.BlockSpec((N_DIR, rd, cb), lambda i, j: (0, 0, j)),
                  pl.BlockSpec((N_DIR, ra, cb), lambda i, j: (0, 0, j)),
                  pl.BlockSpec((rg, cb), lambda i, j: (0, j)),
                  pl.BlockSpec((cb, nsel), lambda i, j: (0, 0)),
                  pl.BlockSpec((nsel, cb), lambda i, j: (0, 0))],
        out_specs=[pspec, pdspec, pdspec, ospec, ospec],
        compiler_params=_cparams(("parallel", "parallel")),
        name="rwkv_prep",
    )(zr, zk, zv, zl, pw["mu_r"], pw["mu_k"], pw["mu_v"], pw["mu_l"],
      pw["k_k"], pw["k_a"], pw["r_k"], pw["w0"], pw["a0"],
      pw["w2"], pw["a2"], pw["g2"], pw["sel"], pw["selt"])


def _load_time_chunk(ref, t0):
    idx = (0,) * (len(ref.shape) - 4) + (slice(None), slice(None), pl.ds(t0, SUBLANES), slice(None))
    x = ref[idx]
    return x.reshape(x.shape[0] * x.shape[1], SUBLANES, x.shape[-1])


def _store_time_rows(ref, s, val):
    steps, width = ref.shape[-2], ref.shape[-1]
    chains = math.prod(ref.shape[:-2])
    ref.reshape(chains * steps, width)[pl.ds(s, chains, stride=steps), :] = val


def _rwkv_scan_kernel(rva_ref, rvb_ref, wba_ref, wbb_ref, kka_ref, kkb_ref, s0_ref,
                      oa_ref, ob_ref, sf_ref, s_ref, out_ref, *ops_refs):
    i = pl.program_id(1)
    n = s_ref.shape[0]
    nj = n // SUBLANES
    steps = out_ref.shape[0]
    half = LANES // 2
    r_off, v_off, w_off, b_off, kd_off, kk_off = (q * n for q in range(6))

    @pl.when(i == 0)
    def _():
        s_ref[...] = s0_ref[...]

    srcs = ((rva_ref, rvb_ref), (wba_ref, wbb_ref), (kka_ref, kkb_ref))

    nparts = len(ops_refs)
    plen = steps // nparts

    def relayout(part, c):
        t0 = pl.multiple_of(part * plen + c * SUBLANES, SUBLANES)
        tb0 = pl.multiple_of(steps - SUBLANES - (part * plen + c * SUBLANES), SUBLANES)
        for p, (a_ref, b_ref) in enumerate(srcs):
            xa = jnp.swapaxes(_load_time_chunk(a_ref, t0), 0, 1)
            xb = jnp.swapaxes(_load_time_chunk(b_ref, tb0), 0, 1)
            for q in range(SUBLANES):
                x = jnp.concatenate([xa[q], xb[SUBLANES - 1 - q]], axis=0)
                ops_refs[part][c * SUBLANES + q, pl.ds(p * LANES, LANES), :] = x.T

    def unlay(t):
        tb_ = steps - 1 - t
        ot = out_ref[t].T
        _store_time_rows(oa_ref, t, ot[:half])
        _store_time_rows(ob_ref, tb_, ot[half:])

    out_ref[0] = jnp.zeros(out_ref.shape[1:], F32)

    def relayout_chunk(c, carry):
        relayout(0, c)
        return carry

    lax.fori_loop(0, plen // SUBLANES, relayout_chunk, 0)

    def state(k, j):
        return s_ref[k, pl.ds(j * SUBLANES, SUBLANES), :]

    for part in range(nparts):
        ops_ref = ops_refs[part]

        def row(s, r, ops_ref=ops_ref):
            return jnp.broadcast_to(ops_ref[s, pl.ds(r, 1), :], (SUBLANES, LANES))

        acc0 = [None] * nj
        for k in range(n):
            kkb = row(0, kk_off + k)
            for j in range(nj):
                p = state(k, j) * kkb
                acc0[j] = p if acc0[j] is None else acc0[j] + p

        def step(s, acc, part=part, ops_ref=ops_ref, row=row):
            t = part * plen + s
            unlay(jnp.maximum(t - 1, 0))
            s_next = jnp.minimum(s + 1, plen - 1)
            vv = [ops_ref[s, pl.ds(v_off + j * SUBLANES, SUBLANES), :] for j in range(nj)]
            out = [None] * nj
            acc_next = [None] * nj
            for k in range(n):
                wb = row(s, w_off + k)
                bb = row(s, b_off + k)
                kdb = row(s, kd_off + k)
                rb = row(s, r_off + k)
                kkn = row(s_next, kk_off + k)
                for j in range(nj):
                    s_new = state(k, j) * wb - acc[j] * bb + vv[j] * kdb
                    s_ref[k, pl.ds(j * SUBLANES, SUBLANES), :] = s_new
                    q = s_new * rb
                    out[j] = q if out[j] is None else out[j] + q
                    p = s_new * kkn
                    acc_next[j] = p if acc_next[j] is None else acc_next[j] + p
            for j in range(nj):
                out_ref[t, pl.ds(j * SUBLANES, SUBLANES), :] = out[j]
            return tuple(acc_next)

        lax.fori_loop(0, plen, step, tuple(acc0))
    unlay(steps - 1)

    @pl.when(i == pl.num_programs(1) - 1)
    def _():
        sf_ref[...] = s_ref[...]


def _rwkv_scan_call(rv, wb, kk, s0):
    bsz, heads, t, _ = rv.shape
    n = RWKV_HEAD
    bpb = (LANES // 2) // heads
    assert bpb * heads * 2 == LANES and bsz % bpb == 0
    nblk = bsz // bpb
    tb = min(RWKV_SCAN_TB, t)
    nparts = RWKV_SCAN_PARTS
    assert t % tb == 0 and tb % nparts == 0
    nt = t // tb
    fwd4 =pl.BlockSpec((bpb, heads, tb, LANES), lambda g, i: (g, 0, i, 0))
    bwd4 = pl.BlockSpec((bpb, heads, tb, LANES), lambda g, i: (g, 0, nt - 1 - i, 0))
    fwd5 = pl.BlockSpec((1, bpb, heads, tb, LANES), lambda g, i: (0, g, 0, i, 0))
    bwd5 = pl.BlockSpec((1, bpb, heads, tb, LANES), lambda g, i: (1, g, 0, nt - 1 - i, 0))
    st = pl.BlockSpec((n, n, LANES), lambda g, i: (0, 0, g))
    o_sds = jax.ShapeDtypeStruct((bsz, heads, t, n), F32)
    return pl.pallas_call(
        _rwkv_scan_kernel,
        out_shape=[o_sds, o_sds, jax.ShapeDtypeStruct((n, n, nblk * LANES), F32)],
        grid=(nblk, nt),
        in_specs=[fwd4, bwd4, fwd5, bwd5, fwd5, bwd5, st],
        out_specs=[pl.BlockSpec((bpb, heads, tb, n), lambda g, i: (g, 0, i, 0)),
                   pl.BlockSpec((bpb, heads, tb, n), lambda g, i: (g, 0, nt - 1 - i, 0)),
                   st],
        scratch_shapes=[pltpu.VMEM((n, n, LANES), F32),
                        pltpu.VMEM((tb, n, LANES), F32)]
        + [pltpu.VMEM((tb // nparts, 6 * n, LANES), F32)] * nparts,
        compiler_params=_cparams(("parallel", "arbitrary")),
        name="rwkv_scan",
    )(rv, rv, wb, wb, kk, kk, s0)


def _rwkv_post_kernel(of_ref, ob_ref, bonus_ref, g_ref, lng_ref, lnb_ref, sel_ref, selt_ref, acc_ref, o_ref,
                      *, head):
    del acc_ref
    sel = sel_ref[...]
    selt = selt_ref[...]

    def head_sum(x):
        return _dot_exact_r(_dot_exact_r(x, sel), selt)

    def rows(ref):
        parts = [jnp.concatenate([ref[q, 0], ref[q, 1]], axis=1) for q in range(ref.shape[0])]
        return parts[0] if len(parts) == 1 else jnp.concatenate(parts, axis=0)

    o = rows(of_ref) + rows(ob_ref)
    mu = head_sum(o) * (1.0 / head)
    dlt = o - mu
    var = head_sum(dlt * dlt) * (1.0 / head)
    on = dlt * lax.rsqrt(var + RWKV_LN_EPS) * lng_ref[...] + lnb_ref[...]
    o_ref[...] = ((on + bonus_ref[...]) * g_ref[...]).astype(o_ref.dtype)


def _rwkv_post_call(of, ob, bonus, g, pw, row0, acc):
    bsz, heads, t, n = of.shape
    m, w = bonus.shape
    cb = LANES
    nsel = pw["sel"].shape[1]
    tm = _seq_rows(t, m, row0)
    r0 = row0 // tm
    nseq = tm // t
    main = pl.BlockSpec((tm, cb), lambda i, j: (i, j))
    colp = pl.BlockSpec((1, cb), lambda i, j: (0, j))
    ospec = pl.BlockSpec((nseq, 2, t, n), lambda i, j: (i, j, 0, 0))
    return pl.pallas_call(
        functools.partial(_rwkv_post_kernel, head=RWKV_HEAD),
        out_shape=jax.ShapeDtypeStruct(acc.shape, acc.dtype),
        grid=(m // tm, w // cb),
        in_specs=[ospec, ospec, main, main, colp, colp,
                  pl.BlockSpec((cb, nsel), lambda i, j: (0, 0)),
                  pl.BlockSpec((nsel, cb), lambda i, j: (0, 0)),
                  pl.BlockSpec(memory_space=pl.ANY)],
        out_specs=pl.BlockSpec((tm, cb), lambda i, j: (r0 + i, j)),
        input_output_aliases={8: 0},
        compiler_params=_cparams(("parallel", "parallel")),
        name="rwkv_post",
    )(of, ob, bonus, g, pw["ln_g"], pw["ln_b"], pw["sel"], pw["selt"], acc)


def _conv_silu_kernel(x_ref, w_ref, b_ref, o_ref, *, seq_len):
    x = x_ref[...]
    rows = x.shape[0]
    kw = w_ref.shape[0]
    t_idx = lax.broadcasted_iota(jnp.int32, x.shape, 0) % seq_len
    y = b_ref[...] + jnp.zeros_like(x)
    for j in range(kw):
        off = j - kw // 2
        xs = x if off == 0 else _row_shift(x, off, t_idx, seq_len)
        y = y + w_ref[pl.ds(j, 1), :] * xs
    o_ref[...] = _silu(y)


def _conv_silu_call(x, row0, nrows, seq_len, w, b, col_blk=0):
    kw, c = w.shape
    cb = _pick(c, 256)
    rb = _seq_rows(seq_len, nrows, row0)
    r0 = row0 // rb
    j0 = col_blk * (c // cb)
    return pl.pallas_call(
        functools.partial(_conv_silu_kernel, seq_len=seq_len),
        out_shape=jax.ShapeDtypeStruct((nrows, c), F32),
        grid=(nrows // rb, c // cb),
        in_specs=[pl.BlockSpec((rb, cb), lambda i, j: (r0 + i, j0 + j)),
                  pl.BlockSpec((kw, cb), lambda i, j: (0, j)),
                  pl.BlockSpec((1, cb), lambda i, j: (0, j))],
        out_specs=pl.BlockSpec((rb, cb), lambda i, j: (i, j)),
        compiler_params=_cparams(("parallel", "parallel")),
        name="ssd_conv",
    )(x, w, b.reshape(1, c))


def _ssd_scan_kernel(x_ref, b_ref, c_ref, dt_ref, dtt_ref, bias_ref, alog_ref, biast_ref, alogt_ref,
                     tri_ref, trit_ref, e_ref, s0_ref, y_ref, sf_ref, st_ref,
                     *, heads, hdim, nstate, groups, has_init):
    d = pl.program_id(0)
    c = pl.program_id(2)
    nc = pl.num_programs(2)
    hpg = heads // groups
    gw = hpg * hdim

    @pl.when(c == 0)
    def _():
        if has_init:
            for g in range(groups):
                st_ref[pl.ds(g * nstate, nstate), :] = s0_ref[0, 0, pl.ds(g * gw, gw), :].T
        else:
            st_ref[...] = jnp.zeros_like(st_ref)

    tri = tri_ref[0]
    trit = trit_ref[0]
    lch = tri.shape[0]
    e = e_ref[...]
    dtp = _softplus(dt_ref[...] + bias_ref[0])
    a = -jnp.exp(alog_ref[0])
    da = dtp * a
    cum = _dot_exact_l(tri, da)
    dat = _softplus(dtt_ref[...] + biast_ref[0]) * (-jnp.exp(alogt_ref[0]))
    cumt = _dot_exact_r(dat, trit)
    total = jnp.sum(da, axis=0, keepdims=True)
    tot8 = jnp.broadcast_to(total, (SUBLANES, LANES))
    dt_full = _dot_exact_r(dtp, e)
    din_full = _dot_exact_r(jnp.exp(cum), e)
    dst_full = _dot_exact_r(jnp.exp(total - cum), e)
    tot_full = _dot_exact_r(jnp.exp(tot8), e)[0:1, :]
    x = x_ref[...]
    xdt = x * dt_full
    xdec = (xdt * dst_full).astype(BF16)
    xdt_b = xdt.astype(BF16)
    bm = b_ref[...]
    cm = c_ref[...]
    visible = tri > 0
    for g in range(groups):
        bg = bm[:, g * nstate:(g + 1) * nstate]
        cg = cm[:, g * nstate:(g + 1) * nstate].astype(BF16)
        bgb = bg.astype(BF16)
        cb = lax.dot_general(cg, bgb, (((1,), (1,)), ((), ())), preferred_element_type=F32)
        st_g = st_ref[pl.ds(g * nstate, nstate), :]
        y_off = _dot(cg, st_g.astype(BF16)) * din_full[:, g * gw:(g + 1) * gw]
        for hh in range(hpg):
            h = g * hpg + hh
            seg = cum[:, h:h + 1] - cumt[h:h + 1, :]
            lmat = jnp.exp(jnp.where(visible, seg, -jnp.inf))
            gmat = (cb * lmat).astype(BF16)
            yd = _dot(gmat, xdt_b[:, h * hdim:(h + 1) * hdim])
            y_ref[0, :, pl.ds(h * hdim, hdim)] = yd + y_off[:, hh * hdim:(hh + 1) * hdim]
        upd = _dot(bg.T.astype(BF16), xdec[:, g * gw:(g + 1) * gw])
        st_ref[pl.ds(g * nstate, nstate), :] = st_g * tot_full[:, g * gw:(g + 1) * gw] + upd

    @pl.when(c == nc - 1)
    def _():
        for g in range(groups):
            sf_ref[0, 0, pl.ds(g * gw, gw), :] = st_ref[pl.ds(g * nstate, nstate), :].T


def _ssd_scan_call(xs, bm, cm, zdt, dt_blk, dtt, row0, bsz, seq_len, pw, s0):
    hp = xs.shape[1]
    gn = bm.shape[1]
    heads = pw["heads"]
    hdim = hp // heads
    nstate = gn // SSD_GROUPS
    lch = min(SSD_CHUNK, seq_len)
    nc = seq_len // lch
    assert seq_len % lch == 0 and row0 % lch == 0
    c0 = row0 // lch
    has_init = s0 is not None
    if s0 is None:
        s0 = jnp.zeros((1, 1, hp, nstate), F32)

    def cidx(d, b, c):
        return b * nc + c + d * (nc - 1 - 2 * c)

    row = lambda d, b, c: (cidx(d, b, c), 0)
    s0_map = (lambda d, b, c: (b, d, 0, 0)) if has_init else (lambda d, b, c: (0, 0, 0, 0))
    dirp = pl.BlockSpec((1, 1, LANES), lambda d, b, c: (d, 0, 0))
    dirt = pl.BlockSpec((1, heads, 1), lambda d, b, c: (d, 0, 0))
    return pl.pallas_call(
        functools.partial(_ssd_scan_kernel, heads=heads, hdim=hdim, nstate=nstate,
                          groups=SSD_GROUPS, has_init=has_init),
        out_shape=[jax.ShapeDtypeStruct((N_DIR, bsz * seq_len, hp), F32),
                   jax.ShapeDtypeStruct((bsz, N_DIR, hp, nstate), F32)],
        grid=(N_DIR, bsz, nc),
        in_specs=[pl.BlockSpec((lch, hp), row),
                  pl.BlockSpec((lch, gn), row),
                  pl.BlockSpec((lch, gn), row),
                  pl.BlockSpec((lch, LANES), lambda d, b, c: (c0 + cidx(d, b, c), dt_blk)),
                  pl.BlockSpec((heads, lch), lambda d, b, c: (0, c0 + cidx(d, b, c))),
                  dirp, dirp, dirt, dirt,
                  pl.BlockSpec((1, lch, lch), lambda d, b, c: (d, 0, 0)),
                  pl.BlockSpec((1, lch, lch), lambda d, b, c: (d, 0, 0)),
                  pl.BlockSpec((LANES, hp), lambda d, b, c: (0, 0)),
                  pl.BlockSpec((1, 1, hp, nstate), s0_map)],
        out_specs=[pl.BlockSpec((1, lch, hp), lambda d, b, c: (d, cidx(d, b, c), 0)),
                   pl.BlockSpec((1, 1, hp, nstate), lambda d, b, c: (b, d, 0, 0))],
        scratch_shapes=[pltpu.VMEM((gn, hp // SSD_GROUPS), F32)],
        compiler_params=_cparams(("arbitrary", "arbitrary", "arbitrary")),
        name="ssd_scan",
    )(xs, bm, cm, zdt, dtt, pw["bias"], pw["alog"], pw["bias_t"], pw["alog_t"],
      pw["tri"][:, :lch, :lch], pw["trit"][:, :lch, :lch], pw["expand"], s0)


def _ssd_post_kernel(x_ref, yf_ref, yb_ref, z_ref, d_ref, g_ref, acc_ref, o_ref):
    del acc_ref
    y = (d_ref[...] * x_ref[...] + yf_ref[0] + yb_ref[0]) * _silu(z_ref[...])
    y = y * lax.rsqrt(jnp.mean(y * y, axis=-1, keepdims=True) + EPS) * g_ref[...]
    o_ref[...] = y.astype(o_ref.dtype)


def _ssd_post_call(xs, ydir, zz, row0, d_full, g, tm, acc):
    n, hp = xs.shape
    r0 = row0 // tm
    return pl.pallas_call(
        _ssd_post_kernel,
        out_shape=jax.ShapeDtypeStruct(acc.shape, acc.dtype),
        grid=(n // tm,),
        in_specs=[pl.BlockSpec((tm, hp), lambda i: (i, 0)),
                  pl.BlockSpec((1, tm, hp), lambda i: (0, i, 0)),
                  pl.BlockSpec((1, tm, hp), lambda i: (1, i, 0)),
                  pl.BlockSpec((tm, hp), lambda i: (r0 + i, 0)),
                  pl.BlockSpec((1, hp), lambda i: (0, 0)),
                  pl.BlockSpec((1, hp), lambda i: (0, 0)),
                  pl.BlockSpec(memory_space=pl.ANY)],
        out_specs=pl.BlockSpec((tm, hp), lambda i: (r0 + i, 0)),
        input_output_aliases={6: 0},
        compiler_params=_cparams(("parallel",)),
        name="ssd_post",
    )(xs, ydir, ydir, zz, d_full, g, acc)


def _s5_kernel(u_ref, bmat_ref, cmat_ref, lam_ref, s0_ref, y_ref, sf_ref,
               st_ref, buf_ref, *, slab_in, slab_state):
    d = pl.program_id(0)
    tb = pl.program_id(2)
    ntb = pl.num_programs(2)
    nb, steps = u_ref.shape[0], u_ref.shape[1]
    nslab = u_ref.shape[2] // slab_in
    spl = buf_ref.shape[0]
    ncol = slab_state // LANES
    assert slab_in == LANES

    @pl.when(tb == 0)
    def _():
        st_ref[...] = s0_ref[0]

    for s_base in range(0, nslab, spl):
        lam = []
        init = []
        for q in range(spl):
            s = s_base + q
            u = jnp.swapaxes(u_ref[:, :, pl.ds(s * slab_in, slab_in)], 0, 1)
            bu = _dot(u.reshape(steps * nb, slab_in).astype(BF16), bmat_ref[0, s])
            for c in range(2 * ncol):
                buf_ref[q, c] = bu[:, c * LANES:(c + 1) * LANES]
            for c in range(ncol):
                lanes = pl.ds(s * slab_state + c * LANES, LANES)
                lam.append((jnp.broadcast_to(lam_ref[0, 0, :, lanes], (nb, LANES)),
                            jnp.broadcast_to(lam_ref[0, 1, :, lanes], (nb, LANES))))
                init.append(st_ref[0, :, lanes])
                init.append(st_ref[1, :, lanes])

        def step(i, carry):
            te = i + d * (steps - 1 - 2 * i)
            rows = pl.ds(pl.multiple_of(te * nb, nb), nb)
            new = []
            for q in range(spl):
                for c in range(ncol):
                    lr, li = lam[q * ncol + c]
                    s_re = carry[2 * (q * ncol + c)]
                    s_im = carry[2 * (q * ncol + c) + 1]
                    n_re = lr * s_re - li * s_im + buf_ref[q, c, rows, :]
                    n_im = lr * s_im + li * s_re + buf_ref[q, ncol + c, rows, :]
                    buf_ref[q, c, rows, :] = n_re
                    buf_ref[q, ncol + c, rows, :] = n_im
                    new += [n_re, n_im]
            return tuple(new)

        fin = lax.fori_loop(0, steps, step, tuple(init), unroll=4)
        for q in range(spl):
            s = s_base + q
            for c in range(ncol):
                lanes = pl.ds(s * slab_state + c * LANES, LANES)
                st_ref[0, :, lanes] = fin[2 * (q * ncol + c)]
                st_ref[1, :, lanes] = fin[2 * (q * ncol + c) + 1]
            states = jnp.concatenate([buf_ref[q, c] for c in range(2 * ncol)], axis=1)
            y = _dot(states.astype(BF16), cmat_ref[0, s])
            y_ref[0, :, :, pl.ds(s * slab_in, slab_in)] = jnp.swapaxes(y.reshape(steps, nb, slab_in), 0, 1)

    @pl.when(tb == ntb - 1)
    def _():
        sf_ref[0] = st_ref[...]


def _s5_call(u3, seq0, bsz, pw, s0):
    _, t, w = u3.shape
    nb = min(SUBLANES, bsz)
    tb = min(S5_TB, t)
    assert t % tb == 0 and bsz % nb == 0 and seq0 % nb == 0
    ntb = t // tb
    b0 = seq0 // nb
    gp = pw["lam"].shape[-1]
    slab_in = pw["slab_in"]
    slab_state = pw["slab_state"]
    nslab = w // slab_in
    spl = 2 if nslab % 2 == 0 else 1

    def tidx(d, i):
        return i + d * (ntb - 1 - 2 * i)

    return pl.pallas_call(
        functools.partial(_s5_kernel, slab_in=slab_in, slab_state=slab_state),
        out_shape=[jax.ShapeDtypeStruct((N_DIR, bsz, t, w), F32),
                   jax.ShapeDtypeStruct((N_DIR, 2, bsz, gp), F32)],
        grid=(N_DIR, bsz // nb, ntb),
        in_specs=[pl.BlockSpec((nb, tb, w), lambda d, b, i: (b0 + b, tidx(d, i), 0)),
                  pl.BlockSpec((1, nslab, slab_in, 2 * slab_state), lambda d, b, i: (d, 0, 0, 0)),
                  pl.BlockSpec((1, nslab, 2 * slab_state, slab_in), lambda d, b, i: (d, 0, 0, 0)),
                  pl.BlockSpec((1, 2, 1, gp), lambda d, b, i: (d, 0, 0, 0)),
                  pl.BlockSpec((1, 2, nb, gp), lambda d, b, i: (d, 0, b, 0))],
        out_specs=[pl.BlockSpec((1, nb, tb, w), lambda d, b, i: (d, b, tidx(d, i), 0)),
                   pl.BlockSpec((1, 2, nb, gp), lambda d, b, i: (d, 0, b, 0))],
        scratch_shapes=[pltpu.VMEM((2, nb, gp), F32),
                        pltpu.VMEM((spl, 2 * slab_state // LANES, nb * tb, LANES), F32)],
        compiler_params=_cparams(("arbitrary", "arbitrary", "arbitrary")),
        name="s5_scan",
    )(u3, pw["bmat"], pw["cmat"], pw["lam"], s0)


def _s5_post_kernel(u_ref, yf_ref, yb_ref, d_ref, acc_ref, o_ref):
    del acc_ref
    y = d_ref[...] * u_ref[...] + yf_ref[0] + yb_ref[0]
    o_ref[...] = jax.nn.gelu(y).astype(o_ref.dtype)


def _s5_post_call(u, row0, ydir, d_full, tm, acc):
    _, m, w = ydir.shape
    r0 = row0 // tm
    return pl.pallas_call(
        _s5_post_kernel,
        out_shape=jax.ShapeDtypeStruct(acc.shape, acc.dtype),
        grid=(m // tm,),
        in_specs=[pl.BlockSpec((tm, w), lambda i: (r0 + i, 0)),
                  pl.BlockSpec((1, tm, w), lambda i: (0, i, 0)),
                  pl.BlockSpec((1, tm, w), lambda i: (1, i, 0)),
                  pl.BlockSpec((1, w), lambda i: (0, 0)),
                  pl.BlockSpec(memory_space=pl.ANY)],
        out_specs=pl.BlockSpec((tm, w), lambda i: (r0 + i, 0)),
        input_output_aliases={4: 0},
        compiler_params=_cparams(("parallel",)),
        name="s5_post",
    )(u, ydir, ydir, d_full, acc)


def _head_selectors(cb, head):
    nsel = LANES
    col = np.arange(cb)[:, None] // head
    sel = (col == np.arange(nsel)[None, :]).astype(np.float32)
    return jnp.asarray(sel, BF16), jnp.asarray(sel.T, BF16)


def _s5_weights(p, l, slab_groups):
    g, pst = p["s5_lambda_re"].shape[2:]
    cg = p["s5_b_re"].shape[-1]
    nslab = g // slab_groups
    eye = jnp.eye(slab_groups, dtype=F32)
    bmats, cmats, lams = [], [], []
    for d in range(N_DIR):
        lam_re, lam_im = p["s5_lambda_re"][l, d], p["s5_lambda_im"][l, d]
        delta = jnp.exp(p["s5_log_dt"][l, d])[:, None]
        mag = jnp.exp(lam_re * delta)
        lb_re, lb_im = mag * jnp.cos(lam_im * delta), mag * jnp.sin(lam_im * delta)
        den = lam_re * lam_re + lam_im * lam_im
        q_re = ((lb_re - 1.0) * lam_re + lb_im * lam_im) / den
        q_im = (lb_im * lam_re - (lb_re - 1.0) * lam_im) / den
        b_re, b_im = p["s5_b_re"][l, d], p["s5_b_im"][l, d]
        bb_re = q_re[..., None] * b_re - q_im[..., None] * b_im
        bb_im = q_re[..., None] * b_im + q_im[..., None] * b_re

        def in_blocks(bb):
            x = bb.reshape(nslab, slab_groups, pst, cg)
            return jnp.einsum("sgpc,gh->sgchp", x, eye).reshape(nslab, slab_groups * cg, slab_groups * pst)

        def out_blocks(cc):
            x = cc.reshape(nslab, slab_groups, cg, pst)
            return jnp.einsum("sgcp,gh->sgphc", x, eye).reshape(nslab, slab_groups * pst, slab_groups * cg)

        bmats.append(jnp.concatenate([in_blocks(bb_re), in_blocks(bb_im)], axis=-1))
        cmats.append(jnp.concatenate([out_blocks(p["s5_c_re"][l, d]), -out_blocks(p["s5_c_im"][l, d])], axis=-2))
        lams.append(jnp.stack([lb_re.reshape(1, g * pst), lb_im.reshape(1, g * pst)]))
    return dict(bmat=jnp.stack(bmats).astype(BF16), cmat=jnp.stack(cmats).astype(BF16),
                lam=jnp.stack(lams), slab_in=slab_groups * cg, slab_state=slab_groups * pst)


def _pad_cols(w, n):
    return jnp.pad(w, ((0, 0), (0, n - w.shape[1])))


def kernel(x_prompt, x_sample, state_rwkv, state_ssd, state_s5_re, state_s5_im, c, c_ctx, w_mod, b_mod, norm_g, ffn_w_in, ffn_w_out, w_in, rwkv_mu, rwkv_w0, rwkv_w2, rwkv_a0, rwkv_a2, rwkv_g2, rwkv_k_k, rwkv_k_a, rwkv_r_k, rwkv_ln_g, rwkv_ln_b, w_proj_a, ssd_conv_w, ssd_conv_b, ssd_dt_bias, ssd_a_log, ssd_d, ssd_norm_g, w_proj_b, s5_lambda_re, s5_lambda_im, s5_log_dt, s5_b_re, s5_b_im, s5_c_re, s5_c_im, s5_d, w_proj_c, w_out, final_norm_g):
    p = dict(s5_lambda_re=s5_lambda_re, s5_lambda_im=s5_lambda_im, s5_log_dt=s5_log_dt,
             s5_b_re=s5_b_re, s5_b_im=s5_b_im, s5_c_re=s5_c_re, s5_c_im=s5_c_im)
    bp, tp, dm = x_prompt.shape
    bs, ts, _ = x_sample.shape
    n_p, n_s = bp * tp, bs * ts
    m = n_p + n_s
    depth = w_mod.shape[0]
    d_ff = ffn_w_out.shape[2]
    ffp = _round_up(d_ff, 512)
    rw = rwkv_k_k.shape[1]
    rh = rw // RWKV_HEAD
    rd, ra, rg = rwkv_w2.shape[2], rwkv_a2.shape[2], rwkv_g2.shape[1]
    lora = rd + ra + rg
    sh = ssd_d.shape[1]
    sw = ssd_norm_g.shape[1]
    xbc_w = ssd_conv_w.shape[2]
    gn = (xbc_w - sw) // 2
    cw = s5_d.shape[1]
    s5_g, s5_p = s5_lambda_re.shape[2:]
    s5_cg = cw // s5_g
    slab_groups = max(1, min(s5_g, LANES // s5_cg))

    tm = _pick(math.gcd(n_p, ts), 1024, SUBLANES)
    tm_wide = _pick(m, 2 * tm, tm)
    n_tiles = m // tm
    tile_cond = np.array([0 if i * tm < n_p else 1 + (i * tm - n_p) // ts for i in range(n_tiles)])

    x = jnp.concatenate([x_prompt.reshape(n_p, dm), x_sample.reshape(n_s, dm)], axis=0)
    ncond = 1 + bs
    cond = jnp.concatenate([c_ctx[None, :], c], axis=0)
    cond8 = jnp.pad(cond, ((0, _round_up(ncond, SUBLANES) - ncond), (0, 0)))

    groups = [dict(b=bp, t=tp, row0=0, n=n_p, grid_w=None),
              dict(b=bs, t=ts, row0=n_p, n=n_s, grid_w=GRID_W)]

    sel, selt = _head_selectors(LANES, RWKV_HEAD)
    heads_per_blk = LANES // RWKV_HEAD
    tri_f = np.tril(np.ones((SSD_CHUNK, SSD_CHUNK), np.float32))
    tri = jnp.asarray(np.stack([tri_f, tri_f.T]), BF16)
    trit = jnp.asarray(np.stack([tri_f.T, tri_f]), BF16)
    expand = jnp.asarray((np.arange(LANES)[:, None] == (np.arange(sw)[None, :] // (sw // sh))).astype(np.float32), BF16)

    new_a, new_b, new_re, new_im = [], [], [], []
    for l in range(depth):
        mod = _mod_call(cond8, w_mod, b_mod, l)
        mods = mod.reshape(-1, N_MOD, dm)[tile_cond]
        sh1, sc1, g1, sh2, sc2, g2, sh3, sc3, g3 = [mods[:, i:i + 1, :] for i in range(N_MOD)]

        def ffn(x, idx, shv, scv, gv):
            wi = ffn_w_in[l, idx].astype(BF16)
            wo = ffn_w_out[l, idx].astype(BF16)
            h = _norm_mod_call(x, norm_g[l, idx * 2], shv, scv, tm)
            a = _mm_swiglu_call(h, wi, wi[:, d_ff:], tm)
            return _mm_res_call(a, wo, x, gv, 0.5, tm)

        x = ffn(x, 0, sh1, sc1, g1)

        h = _norm_mod_call(x, norm_g[l, 1], sh2, sc2, tm)
        wl_in = w_in[l]
        offs = np.cumsum([0, rw, rw, rw, lora, sw, sw, gn, gn, sh, cw, 3 * dm])
        segs = [wl_in[:, offs[i]:offs[i + 1]].astype(BF16) for i in range(11)]
        w_small = jnp.concatenate([segs[3], segs[6], segs[7], _pad_cols(segs[8], LANES)], axis=1)
        assert lora % gn == 0 and (lora + 2 * gn) % LANES == 0
        bm_blk, cm_blk, dt_blk = lora // gn, lora // gn + 1, (lora + 2 * gn) // LANES
        zr, zk, zv, zz, zxs, zc, zg, zs = [_mm_call(h, wseg, tm_wide)
                                           for wseg in (segs[0], segs[1], segs[2], segs[4], segs[5],
                                                        segs[9], segs[10], w_small)]
        dtt = zs[:, lora + 2 * gn:lora + 2 * gn + sh].T

        mu = rwkv_mu[l]
        pw_r = dict(ranks=(rd, ra, rg), sel=sel, selt=selt,
                    mu_r=mu[None, :rw], mu_k=mu[None, rw:2 * rw], mu_v=mu[None, 2 * rw:3 * rw],
                    mu_l=mu[None, 3 * rw:],
                    k_k=rwkv_k_k[l][None], k_a=rwkv_k_a[l][None], r_k=rwkv_r_k[l].reshape(1, rw),
                    w0=rwkv_w0[l][:, None, :], a0=rwkv_a0[l][:, None, :],
                    w2=rwkv_w2[l].astype(BF16), a2=rwkv_a2[l].astype(BF16), g2=rwkv_g2[l].astype(BF16),
                    ln_g=rwkv_ln_g[l][None], ln_b=rwkv_ln_b[l][None])
        y_a, fin_a = jnp.zeros((m, rw), BF16), None
        bpb = (LANES // 2) // rh
        for gi, gr in enumerate(groups):
            b_, t_ = gr["b"], gr["t"]
            nblk = b_ // bpb
            rv_, wb_, kk_, bonus_, g_ = _rwkv_prep_call(
                zr, zk, zv, zs, gr["row0"], gr["n"], t_, gr["grid_w"], pw_r)
            if gi == 0:
                s0c = jnp.zeros((RWKV_HEAD, RWKV_HEAD, nblk * LANES), F32)
            else:
                s0c = state_rwkv[:, l].reshape(nblk, bpb, N_DIR, rh, RWKV_HEAD, RWKV_HEAD)
                s0c = s0c.transpose(5, 4, 0, 2, 1, 3).reshape(RWKV_HEAD, RWKV_HEAD, nblk * LANES)
            o_f, o_b, sf_c = _rwkv_scan_call(rv_, wb_, kk_, s0c)
            y_a = _rwkv_post_call(o_f, o_b, bonus_, g_, pw_r, gr["row0"], y_a)
            if gi == 0:
                fin_a = sf_c.reshape(RWKV_HEAD, RWKV_HEAD, nblk, N_DIR, bpb, rh)
                fin_a = fin_a.transpose(2, 4, 3, 5, 1, 0).reshape(b_, N_DIR, rh, RWKV_HEAD, RWKV_HEAD)
        new_a.append(fin_a)

        cw_l, cb_l = ssd_conv_w[l], ssd_conv_b[l]
        pw_s = dict(heads=sh, tri=tri, trit=trit, expand=expand,
                    bias=jnp.pad(ssd_dt_bias[l], ((0, 0), (0, LANES - sh)))[:, None, :],
                    alog=jnp.pad(ssd_a_log[l], ((0, 0), (0, LANES - sh)))[:, None, :],
                    bias_t=ssd_dt_bias[l][:, :, None], alog_t=ssd_a_log[l][:, :, None])
        y_b, fin_b = jnp.zeros((m, sw), BF16), None
        for gi, gr in enumerate(groups):
            xs_ = _conv_silu_call(zxs, gr["row0"], gr["n"], gr["t"], cw_l[:, :sw], cb_l[:sw])
            bm_ = _conv_silu_call(zs, gr["row0"], gr["n"], gr["t"], cw_l[:, sw:sw + gn], cb_l[sw:sw + gn], bm_blk)
            cm_ = _conv_silu_call(zs, gr["row0"], gr["n"], gr["t"], cw_l[:, sw + gn:], cb_l[sw + gn:], cm_blk)
            s0s = None if gi == 0 else state_ssd[:, l].reshape(gr["b"], N_DIR, sw, gn // SSD_GROUPS)
            ydir, sf_s = _ssd_scan_call(xs_, bm_, cm_, zs, dt_blk, dtt, gr["row0"], gr["b"], gr["t"], pw_s, s0s)
            y_b = _ssd_post_call(xs_, ydir, zz, gr["row0"],
                                 jnp.repeat(ssd_d[l], sw // sh)[None, :], ssd_norm_g[l][None, :], tm, y_b)
            if gi == 0:
                fin_b = sf_s.reshape(gr["b"], N_DIR, sh, sw // sh, gn // SSD_GROUPS)
        new_b.append(fin_b)

        pw_c = _s5_weights(p, l, slab_groups)
        y_c, fin_re, fin_im = jnp.zeros((m, cw), BF16), None, None
        for gi, gr in enumerate(groups):
            b_, t_ = gr["b"], gr["t"]
            assert gr["row0"] % t_ == 0
            if gi == 0:
                s0 = jnp.zeros((N_DIR, 2, b_, s5_g * s5_p), F32)
            else:
                s0 = jnp.stack([state_s5_re[:, l], state_s5_im[:, l]])
                s0 = s0.reshape(2, b_, N_DIR, s5_g * s5_p).transpose(2, 0, 1, 3)
            y_dir, sf = _s5_call(zc.reshape(m // t_, t_, cw), gr["row0"] // t_, b_, pw_c, s0)
            y_dir = y_dir.reshape(N_DIR, gr["n"], cw)
            y_c = _s5_post_call(zc, gr["row0"], y_dir, s5_d[l][None, :], tm, y_c)
            if gi == 0:
                fin = sf.reshape(N_DIR, 2, b_, s5_g, s5_p).transpose(1, 2, 0, 3, 4)
                fin_re, fin_im = fin[0], fin[1]
        new_re.append(fin_re)
        new_im.append(fin_im)

        wc = w_proj_c[l]
        merged = _merge_call(y_a, y_b, y_c, zg, w_proj_a[l].astype(BF16), w_proj_b[l].astype(BF16),
                             jnp.stack([wc[:, :dm], wc[:, dm:]]).astype(BF16), tm)
        x = _mm_res_call(merged, w_out[l].astype(BF16), x, g2, 1.0, tm)

        x = ffn(x, 1, sh3, sc3, g3)

    y_p = _final_norm_call(x, 0, n_p, final_norm_g, tm)
    y_s = _final_norm_call(x, n_p, n_s, final_norm_g, tm)
    return (y_p.reshape(bp, tp, dm), y_s.reshape(bs, ts, dm),
            jnp.stack(new_a, axis=1), jnp.stack(new_b, axis=1),
            jnp.stack(new_re, axis=1), jnp.stack(new_im, axis=1))
```

```python
import functools
import math

import jax
import jax.numpy as jnp
import numpy as np
from jax import lax
from jax.experimental import pallas as pl
from jax.experimental.pallas import tpu as pltpu

F32 = jnp.float32
BF16 = jnp.bfloat16

LANES = 128
SUBLANES = 8
VMEM_LIMIT = 56 * 1024 * 1024

GRID_W = 64
SSD_CHUNK = 128
SSD_GROUPS = 2
N_DIR = 2
N_MOD = 9
EPS = 1e-6
RWKV_LN_EPS = 64e-5
RWKV_HEAD = 64
RWKV_SCAN_TB = 32
RWKV_SCAN_PARTS = 1
S5_TB = 64


def _cparams(sem):
    return pltpu.CompilerParams(dimension_semantics=sem, vmem_limit_bytes=VMEM_LIMIT)


def _pick(n, target, mult=LANES):
    best = None
    d = mult
    while d <= min(n, target):
        if n % d == 0:
            best = d
        d += mult
    return n if best is None else best


def _seq_rows(seq_len, nrows, row0, cap=2048):
    span = math.gcd(nrows, row0) if row0 else nrows
    assert span % seq_len == 0
    return _pick(span, max(cap, seq_len), seq_len)


def _round_up(n, m):
    return -(-n // m) * m


def _dot(a, b):
    return jnp.dot(a, b, preferred_element_type=F32)


def _split3(x):
    x1 = x.astype(BF16)
    r1 = x - x1.astype(F32)
    x2 = r1.astype(BF16)
    x3 = (r1 - x2.astype(F32)).astype(BF16)
    return x1, x2, x3


def _dot_exact_r(x, sel):
    return sum(_dot(p, sel) for p in _split3(x))


def _dot_exact_l(sel, x):
    return sum(_dot(sel, p) for p in _split3(x))


def _softplus(x):
    return jnp.maximum(x, 0.0) + jnp.log1p(jnp.exp(-jnp.abs(x)))


def _sigmoid(x):
    return 0.5 * (jnp.tanh(0.5 * x) + 1.0)


def _silu(x):
    return x * _sigmoid(x)


def _mod_kernel(c_ref, w_ref, b_ref, o_ref):
    c = c_ref[...]
    a = _silu(c).astype(BF16)
    o_ref[...] = _dot(a, w_ref[0].astype(BF16)) + b_ref[0]


def _mod_call(cond8, w_all, b_all, layer):
    depth, d, n = w_all.shape
    tn = _pick(n, 1024)
    return pl.pallas_call(
        _mod_kernel,
        out_shape=jax.ShapeDtypeStruct((cond8.shape[0], n), F32),
        grid=(n // tn,),
        in_specs=[pl.BlockSpec((cond8.shape[0], d), lambda j: (0, 0)),
                  pl.BlockSpec((1, d, tn), lambda j: (layer, 0, j)),
                  pl.BlockSpec((1, 1, tn), lambda j: (layer, 0, j))],
        out_specs=pl.BlockSpec((cond8.shape[0], tn), lambda j: (0, j)),
        compiler_params=_cparams(("parallel",)),
        name="mod_proj",
    )(cond8, w_all, b_all.reshape(depth, 1, n))


def _norm_mod_kernel(x_ref, g_ref, sh_ref, sc_ref, o_ref):
    x = x_ref[...]
    y = x * lax.rsqrt(jnp.mean(x * x, axis=-1, keepdims=True) + EPS) * g_ref[...]
    o_ref[...] = (y * (1.0 + sc_ref[0]) + sh_ref[0]).astype(o_ref.dtype)


def _norm_mod_call(x, g, sh, sc, tm):
    m, d = x.shape
    return pl.pallas_call(
        _norm_mod_kernel,
        out_shape=jax.ShapeDtypeStruct((m, d), BF16),
        grid=(m // tm,),
        in_specs=[pl.BlockSpec((tm, d), lambda i: (i, 0)),
                  pl.BlockSpec((1, d), lambda i: (0, 0)),
                  pl.BlockSpec((1, 1, d), lambda i: (i, 0, 0)),
                  pl.BlockSpec((1, 1, d), lambda i: (i, 0, 0))],
        out_specs=pl.BlockSpec((tm, d), lambda i: (i, 0)),
        compiler_params=_cparams(("parallel",)),
        name="norm_mod",
    )(x, g.reshape(1, d), sh, sc)


def _final_norm_kernel(x_ref, g_ref, o_ref):
    x = x_ref[...]
    o_ref[...] = x * lax.rsqrt(jnp.mean(x * x, axis=-1, keepdims=True) + EPS) * g_ref[...]


def _final_norm_call(x, row0, nrows, g, tm):
    d = x.shape[1]
    r0 = row0 // tm
    return pl.pallas_call(
        _final_norm_kernel,
        out_shape=jax.ShapeDtypeStruct((nrows, d), F32),
        grid=(nrows // tm,),
        in_specs=[pl.BlockSpec((tm, d), lambda i: (r0 + i, 0)),
                  pl.BlockSpec((1, d), lambda i: (0, 0))],
        out_specs=pl.BlockSpec((tm, d), lambda i: (i, 0)),
        compiler_params=_cparams(("parallel",)),
        name="final_norm",
    )(x, g.reshape(1, d))


def _mm_kernel(a_ref, w_ref, o_ref):
    o_ref[...] = _dot(a_ref[...], w_ref[...]).astype(o_ref.dtype)


def _mm_call(a, w, tm, tn_target=1024, out_dtype=F32):
    m, k = a.shape
    n = w.shape[1]
    tn = _pick(n, tn_target)
    return pl.pallas_call(
        _mm_kernel,
        out_shape=jax.ShapeDtypeStruct((m, n), out_dtype),
        grid=(m // tm, n // tn),
        in_specs=[pl.BlockSpec((tm, k), lambda i, j: (i, 0)),
                  pl.BlockSpec((k, tn), lambda i, j: (0, j))],
        out_specs=pl.BlockSpec((tm, tn), lambda i, j: (i, j)),
        compiler_params=_cparams(("parallel", "parallel")),
        name="mm",
    )(a, w)


def _mm_swiglu_kernel(a_ref, wg_ref, wu_ref, o_ref):
    a = a_ref[...]
    gate = _dot(a, wg_ref[...])
    up = _dot(a, wu_ref[...])
    o_ref[...] = (_silu(gate) * up).astype(o_ref.dtype)


def _mm_swiglu_call(a, w_gate_up, w_up, tm, tn_target=512):
    m, k = a.shape
    n = w_up.shape[1]
    tn = tn_target if n >= tn_target else n
    return pl.pallas_call(
        _mm_swiglu_kernel,
        out_shape=jax.ShapeDtypeStruct((m, n), BF16),
        grid=(m // tm, pl.cdiv(n, tn)),
        in_specs=[pl.BlockSpec((tm, k), lambda i, j: (i, 0)),
                  pl.BlockSpec((k, tn), lambda i, j: (0, j)),
                  pl.BlockSpec((k, tn), lambda i, j: (0, j))],
        out_specs=pl.BlockSpec((tm, tn), lambda i, j: (i, j)),
        compiler_params=_cparams(("parallel", "parallel")),
        name="mm_swiglu",
    )(a, w_gate_up, w_up)


def _mm_res_kernel(a_ref, w_ref, x_ref, g_ref, o_ref, *, coef):
    o_ref[...] = x_ref[...] + (coef * g_ref[0]) * _dot(a_ref[...], w_ref[...])


def _mm_res_call(a, w, x, gate, coef, tm, tn_target=512, w_index=()):
    m, k = a.shape
    n = w.shape[-1]
    tn = _pick(n, tn_target)
    return pl.pallas_call(
        functools.partial(_mm_res_kernel, coef=coef),
        out_shape=jax.ShapeDtypeStruct((m, n), F32),
        grid=(m // tm, n // tn),
        in_specs=[pl.BlockSpec((tm, k), lambda i, j: (i, 0)),
                  pl.BlockSpec((None,) * len(w_index) + (k, tn), lambda i, j: tuple(w_index) + (0, j)),
                  pl.BlockSpec((tm, tn), lambda i, j: (i, j)),
                  pl.BlockSpec((1, 1, tn), lambda i, j: (i, 0, j))],
        out_specs=pl.BlockSpec((tm, tn), lambda i, j: (i, j)),
        compiler_params=_cparams(("parallel", "parallel")),
        name="mm_res",
    )(a, w, x, gate)


def _merge_kernel(ya_ref, yb_ref, yc_ref, ga_ref, gb_ref, gc_ref, wa_ref, wb_ref, wc_ref, o_ref):
    pa = _dot(ya_ref[...], wa_ref[...])
    pb = _dot(yb_ref[...], wb_ref[...])
    yc = yc_ref[...]
    val = _dot(yc, wc_ref[0])
    gate = _dot(yc, wc_ref[1])
    merged = (_sigmoid(ga_ref[...]) * pa + _sigmoid(gb_ref[...]) * pb
              + _sigmoid(gc_ref[...]) * (val * _sigmoid(gate)))
    o_ref[...] = merged.astype(o_ref.dtype)


def _merge_call(ya, yb, yc, zg, wa, wb, wc2, tm, tn_target=512):
    m, ka = ya.shape
    d = wa.shape[1]
    tn = _pick(d, tn_target)
    nj = d // tn
    return pl.pallas_call(
        _merge_kernel,
        out_shape=jax.ShapeDtypeStruct((m, d), BF16),
        grid=(m // tm, nj),
        in_specs=[pl.BlockSpec((tm, ka), lambda i, j: (i, 0)),
                  pl.BlockSpec((tm, yb.shape[1]), lambda i, j: (i, 0)),
                  pl.BlockSpec((tm, yc.shape[1]), lambda i, j: (i, 0)),
                  pl.BlockSpec((tm, tn), lambda i, j: (i, j)),
                  pl.BlockSpec((tm, tn), lambda i, j: (i, nj + j)),
                  pl.BlockSpec((tm, tn), lambda i, j: (i, 2 * nj + j)),
                  pl.BlockSpec((ka, tn), lambda i, j: (0, j)),
                  pl.BlockSpec((yb.shape[1], tn), lambda i, j: (0, j)),
                  pl.BlockSpec((2, yc.shape[1], tn), lambda i, j: (0, 0, j))],
        out_specs=pl.BlockSpec((tm, tn), lambda i, j: (i, j)),
        compiler_params=_cparams(("parallel", "parallel")),
        name="merge",
    )(ya, yb, yc, zg, zg, zg, wa, wb, wc2)


def _row_shift(x, off, t_idx, seq_len):
    rows = x.shape[0]
    rolled = pltpu.roll(x, (-off) % rows, axis=0)
    src = t_idx + off
    ok = jnp.logical_and(src >= 0, src < seq_len)
    return jnp.where(ok, rolled, 0.0)


def _centred_nb(x, t_idx, seq_len, grid_w):
    if grid_w is None:
        return 0.5 * (_row_shift(x, -1, t_idx, seq_len) + _row_shift(x, 1, t_idx, seq_len))
    col = t_idx % grid_w
    left = jnp.where(col >= 1, _row_shift(x, -1, t_idx, seq_len), 0.0)
    right = jnp.where(col < grid_w - 1, _row_shift(x, 1, t_idx, seq_len), 0.0)
    up = _row_shift(x, -grid_w, t_idx, seq_len)
    down = _row_shift(x, grid_w, t_idx, seq_len)
    return 0.25 * (up + down + left + right)


def _rwkv_prep_kernel(zr_ref, zk_ref, zv_ref, zl_ref,
                      mur_ref, muk_ref, muv_ref, mul_ref,
                      kk_w_ref, ka_w_ref, rk_w_ref, w0_ref, a0_ref,
                      w2_ref, a2_ref, g2_ref, sel_ref, selt_ref,
                      rv_o, wb_o, kk_o, bonus_o, g_o,
                      *, seq_len, grid_w, ranks):
    nseq = rv_o.shape[0]

    def emit(write, a, b):
        lane = lax.broadcasted_iota(jnp.int32, a.shape, 1)
        first = lane < RWKV_HEAD
        head0 = jnp.where(first, a, pltpu.roll(b, RWKV_HEAD, axis=1))
        head1 = jnp.where(first, pltpu.roll(a, RWKV_HEAD, axis=1), b)
        for q in range(nseq):
            write(q, 0, head0[q * seq_len:(q + 1) * seq_len])
            write(q, 1, head1[q * seq_len:(q + 1) * seq_len])

    def shifted(ref, mu_ref):
        x = ref[...]
        t_idx = lax.broadcasted_iota(jnp.int32, x.shape, 0) % seq_len
        return x + mu_ref[...] * (_centred_nb(x, t_idx, seq_len, grid_w) - x)

    r = shifted(zr_ref, mur_ref)
    k = shifted(zk_ref, muk_ref)
    v = shifted(zv_ref, muv_ref)
    lo = shifted(zl_ref, mul_ref)
    rd, ra, rg = ranks
    wl = lo[:, :rd]
    al = lo[:, rd:rd + ra]
    gl = lo[:, rd + ra:rd + ra + rg]
    sel = sel_ref[...]
    selt = selt_ref[...]

    def head_sum(x):
        return _dot_exact_r(_dot_exact_r(x, sel), selt)

    kk = k * kk_w_ref[...]
    kk = kk * lax.rsqrt(head_sum(kk * kk) + 1e-12)
    def write_rv(q, hh, val):
        rv_o[q, hh] = val

    emit(write_rv, r, v)
    tw = jnp.tanh(wl).astype(BF16)
    alb = al.astype(BF16)
    for d in range(N_DIR):
        w_log = -_softplus(-(w0_ref[d] + _dot(tw, w2_ref[d]))) - 0.5
        a_d = _sigmoid(a0_ref[d] + _dot(alb, a2_ref[d]))

        def write_wb(q, hh, val, d=d):
            wb_o[d, q, hh] = val

        def write_kk(q, hh, val, d=d):
            kk_o[d, q, hh] = val

        emit(write_wb, jnp.exp(-jnp.exp(w_log)), kk * a_d)
        emit(write_kk, k * (1.0 + (a_d - 1.0) * ka_w_ref[...]), kk)
    bonus_o[...] = head_sum(r * k * rk_w_ref[...]) * v
    g_o[...] = _dot(_sigmoid(gl).astype(BF16), g2_ref[...])


def _rwkv_prep_call(zr, zk, zv, zl, row0, nrows, seq_len, grid_w, pw):
    w = zr.shape[1]
    lw = sum(pw["ranks"])
    cb = LANES
    rb = _seq_rows(seq_len, nrows, row0)
    assert w % cb == 0
    r0 = row0 // rb
    main = pl.BlockSpec((rb, cb), lambda i, j: (r0 + i, j))
    lspec = pl.BlockSpec((rb, lw), lambda i, j: (r0 + i, 0))
    colp = pl.BlockSpec((1, cb), lambda i, j: (0, j))
    dirp = pl.BlockSpec((N_DIR, 1, cb), lambda i, j: (0, 0, j))
    ospec = pl.BlockSpec((rb, cb), lambda i, j: (i, j))
    rd, ra, rg = pw["ranks"]
    nsel = pw["sel"].shape[1]
    assert cb == 2 * RWKV_HEAD
    nseq = rb // seq_len
    bsz = nrows // seq_len
    heads = w // RWKV_HEAD
    pk = jax.ShapeDtypeStruct((bsz, heads, seq_len, cb), F32)
    pkd = jax.ShapeDtypeStruct((N_DIR, bsz, heads, seq_len, cb), F32)
    pspec = pl.BlockSpec((nseq, 2, seq_len, cb), lambda i, j: (i, j, 0, 0))
    pdspec = pl.BlockSpec((N_DIR, nseq, 2, seq_len, cb), lambda i, j: (0, i, j, 0, 0))
    return pl.pallas_call(
        functools.partial(_rwkv_prep_kernel, seq_len=seq_len, grid_w=grid_w, ranks=pw["ranks"]),
        out_shape=[pk, pkd, pkd, jax.ShapeDtypeStruct((nrows, w), F32), jax.ShapeDtypeStruct((nrows, w), F32)],
        grid=(nrows // rb, w // cb),
        in_specs=[main, main, main, lspec,
                  colp, colp, colp, pl.BlockSpec((1, lw), lambda i, j: (0, 0)),
                  colp, colp, colp, dirp, dirp,
                  pl.BlockSpec((N_DIR, rd, cb), lambda i, j: (0, 0, j)),
                  pl.BlockSpec((N_DIR, ra, cb), lambda i, j: (0, 0, j)),
                  pl.BlockSpec((rg, cb), lambda i, j: (0, j)),
                  pl.BlockSpec((cb, nsel), lambda i, j: (0, 0)),
                  pl.BlockSpec((nsel, cb), lambda i, j: (0, 0))],
        out_specs=[pspec, pdspec, pdspec, ospec, ospec],
        compiler_params=_cparams(("parallel", "parallel")),
        name="rwkv_prep",
    )(zr, zk, zv, zl, pw["mu_r"], pw["mu_k"], pw["mu_v"], pw["mu_l"],
      pw["k_k"], pw["k_a"], pw["r_k"], pw["w0"], pw["a0"],
      pw["w2"], pw["a2"], pw["g2"], pw["sel"], pw["selt"])


def _load_time_chunk(ref, t0):
    idx = (0,) * (len(ref.shape) - 4) + (slice(None), slice(None), pl.ds(t0, SUBLANES), slice(None))
    x = ref[idx]
    return x.reshape(x.shape[0] * x.shape[1], SUBLANES, x.shape[-1])


def _store_time_rows(ref, s, val):
    steps, width = ref.shape[-2], ref.shape[-1]
    chains = math.prod(ref.shape[:-2])
    ref.reshape(chains * steps, width)[pl.ds(s, chains, stride=steps), :] = val


def _rwkv_scan_kernel(rva_ref, rvb_ref, wba_ref, wbb_ref, kka_ref, kkb_ref, s0_ref,
                      oa_ref, ob_ref, sf_ref, s_ref, out_ref, *ops_refs):
    i = pl.program_id(1)
    n = s_ref.shape[0]
    nj = n // SUBLANES
    steps = out_ref.shape[0]
    half = LANES // 2
    r_off, v_off, w_off, b_off, kd_off, kk_off = (q * n for q in range(6))

    @pl.when(i == 0)
    def _():
        s_ref[...] = s0_ref[...]

    srcs = ((rva_ref, rvb_ref), (wba_ref, wbb_ref), (kka_ref, kkb_ref))

    nparts = len(ops_refs)
    plen = steps // nparts

    def relayout(part, c):
        t0 = pl.multiple_of(part * plen + c * SUBLANES, SUBLANES)
        tb0 = pl.multiple_of(steps - SUBLANES - (part * plen + c * SUBLANES), SUBLANES)
        for p, (a_ref, b_ref) in enumerate(srcs):
            xa = jnp.swapaxes(_load_time_chunk(a_ref, t0), 0, 1)
            xb = jnp.swapaxes(_load_time_chunk(b_ref, tb0), 0, 1)
            for q in range(SUBLANES):
                x = jnp.concatenate([xa[q], xb[SUBLANES - 1 - q]], axis=0)
                ops_refs[part][c * SUBLANES + q, pl.ds(p * LANES, LANES), :] = x.T

    def unlay(t):
        tb_ = steps - 1 - t
        ot = out_ref[t].T
        _store_time_rows(oa_ref, t, ot[:half])
        _store_time_rows(ob_ref, tb_, ot[half:])

    out_ref[0] = jnp.zeros(out_ref.shape[1:], F32)

    def relayout_chunk(c, carry):
        relayout(0, c)
        return carry

    lax.fori_loop(0, plen // SUBLANES, relayout_chunk, 0)

    def state(k, j):
        return s_ref[k, pl.ds(j * SUBLANES, SUBLANES), :]

    for part in range(nparts):
        ops_ref = ops_refs[part]

        def row(s, r, ops_ref=ops_ref):
            return jnp.broadcast_to(ops_ref[s, pl.ds(r, 1), :], (SUBLANES, LANES))

        acc0 = [None] * nj
        for k in range(n):
            kkb = row(0, kk_off + k)
            for j in range(nj):
                p = state(k, j) * kkb
                acc0[j] = p if acc0[j] is None else acc0[j] + p

        def step(s, acc, part=part, ops_ref=ops_ref, row=row):
            t = part * plen + s
            unlay(jnp.maximum(t - 1, 0))
            s_next = jnp.minimum(s + 1, plen - 1)
            vv = [ops_ref[s, pl.ds(v_off + j * SUBLANES, SUBLANES), :] for j in range(nj)]
            out = [None] * nj
            acc_next = [None] * nj
            for k in range(n):
                wb = row(s, w_off + k)
                bb = row(s, b_off + k)
                kdb = row(s, kd_off + k)
                rb = row(s, r_off + k)
                kkn = row(s_next, kk_off + k)
                for j in range(nj):
                    s_new = state(k, j) * wb - acc[j] * bb + vv[j] * kdb
                    s_ref[k, pl.ds(j * SUBLANES, SUBLANES), :] = s_new
                    q = s_new * rb
                    out[j] = q if out[j] is None else out[j] + q
                    p = s_new * kkn
                    acc_next[j] = p if acc_next[j] is None else acc_next[j] + p
            for j in range(nj):
                out_ref[t, pl.ds(j * SUBLANES, SUBLANES), :] = out[j]
            return tuple(acc_next)

        lax.fori_loop(0, plen, step, tuple(acc0))
    unlay(steps - 1)

    @pl.when(i == pl.num_programs(1) - 1)
    def _():
        sf_ref[...] = s_ref[...]


def _rwkv_scan_call(rv, wb, kk, s0):
    bsz, heads, t, _ = rv.shape
    n = RWKV_HEAD
    bpb = (LANES // 2) // heads
    assert bpb * heads * 2 == LANES and bsz % bpb == 0
    nblk = bsz // bpb
    tb = min(RWKV_SCAN_TB, t)
    nparts = RWKV_SCAN_PARTS
    assert t % tb == 0 and tb % nparts == 0
    nt = t // tb
    fwd4 =pl.BlockSpec((bpb, heads, tb, LANES), lambda g, i: (g, 0, i, 0))
    bwd4 = pl.BlockSpec((bpb, heads, tb, LANES), lambda g, i: (g, 0, nt - 1 - i, 0))
    fwd5 = pl.BlockSpec((1, bpb, heads, tb, LANES), lambda g, i: (0, g, 0, i, 0))
    bwd5 = pl.BlockSpec((1, bpb, heads, tb, LANES), lambda g, i: (1, g, 0, nt - 1 - i, 0))
    st = pl.BlockSpec((n, n, LANES), lambda g, i: (0, 0, g))
    o_sds = jax.ShapeDtypeStruct((bsz, heads, t, n), F32)
    return pl.pallas_call(
        _rwkv_scan_kernel,
        out_shape=[o_sds, o_sds, jax.ShapeDtypeStruct((n, n, nblk * LANES), F32)],
        grid=(nblk, nt),
        in_specs=[fwd4, bwd4, fwd5, bwd5, fwd5, bwd5, st],
        out_specs=[pl.BlockSpec((bpb, heads, tb, n), lambda g, i: (g, 0, i, 0)),
                   pl.BlockSpec((bpb, heads, tb, n), lambda g, i: (g, 0, nt - 1 - i, 0)),
                   st],
        scratch_shapes=[pltpu.VMEM((n, n, LANES), F32),
                        pltpu.VMEM((tb, n, LANES), F32)]
        + [pltpu.VMEM((tb // nparts, 6 * n, LANES), F32)] * nparts,
        compiler_params=_cparams(("parallel", "arbitrary")),
        name="rwkv_scan",
    )(rv, rv, wb, wb, kk, kk, s0)


def _rwkv_post_kernel(of_ref, ob_ref, bonus_ref, g_ref, lng_ref, lnb_ref, sel_ref, selt_ref, acc_ref, o_ref,
                      *, head):
    del acc_ref
    sel = sel_ref[...]
    selt = selt_ref[...]

    def head_sum(x):
        return _dot_exact_r(_dot_exact_r(x, sel), selt)

    def rows(ref):
        parts = [jnp.concatenate([ref[q, 0], ref[q, 1]], axis=1) for q in range(ref.shape[0])]
        return parts[0] if len(parts) == 1 else jnp.concatenate(parts, axis=0)

    o = rows(of_ref) + rows(ob_ref)
    mu = head_sum(o) * (1.0 / head)
    dlt = o - mu
    var = head_sum(dlt * dlt) * (1.0 / head)
    on = dlt * lax.rsqrt(var + RWKV_LN_EPS) * lng_ref[...] + lnb_ref[...]
    o_ref[...] = ((on + bonus_ref[...]) * g_ref[...]).astype(o_ref.dtype)


def _rwkv_post_call(of, ob, bonus, g, pw, row0, acc):
    bsz, heads, t, n = of.shape
    m, w = bonus.shape
    cb = LANES
    nsel = pw["sel"].shape[1]
    tm = _seq_rows(t, m, row0)
    r0 = row0 // tm
    nseq = tm // t
    main = pl.BlockSpec((tm, cb), lambda i, j: (i, j))
    colp = pl.BlockSpec((1, cb), lambda i, j: (0, j))
    ospec = pl.BlockSpec((nseq, 2, t, n), lambda i, j: (i, j, 0, 0))
    return pl.pallas_call(
        functools.partial(_rwkv_post_kernel, head=RWKV_HEAD),
        out_shape=jax.ShapeDtypeStruct(acc.shape, acc.dtype),
        grid=(m // tm, w // cb),
        in_specs=[ospec, ospec, main, main, colp, colp,
                  pl.BlockSpec((cb, nsel), lambda i, j: (0, 0)),
                  pl.BlockSpec((nsel, cb), lambda i, j: (0, 0)),
                  pl.BlockSpec(memory_space=pl.ANY)],
        out_specs=pl.BlockSpec((tm, cb), lambda i, j: (r0 + i, j)),
        input_output_aliases={8: 0},
        compiler_params=_cparams(("parallel", "parallel")),
        name="rwkv_post",
    )(of, ob, bonus, g, pw["ln_g"], pw["ln_b"], pw["sel"], pw["selt"], acc)


def _conv_silu_kernel(x_ref, w_ref, b_ref, o_ref, *, seq_len):
    x = x_ref[...]
    rows = x.shape[0]
    kw = w_ref.shape[0]
    t_idx = lax.broadcasted_iota(jnp.int32, x.shape, 0) % seq_len
    y = b_ref[...] + jnp.zeros_like(x)
    for j in range(kw):
        off = j - kw // 2
        xs = x if off == 0 else _row_shift(x, off, t_idx, seq_len)
        y = y + w_ref[pl.ds(j, 1), :] * xs
    o_ref[...] = _silu(y)


def _conv_silu_call(x, row0, nrows, seq_len, w, b, col_blk=0):
    kw, c = w.shape
    cb = _pick(c, 256)
    rb = _seq_rows(seq_len, nrows, row0)
    r0 = row0 // rb
    j0 = col_blk * (c // cb)
    return pl.pallas_call(
        functools.partial(_conv_silu_kernel, seq_len=seq_len),
        out_shape=jax.ShapeDtypeStruct((nrows, c), F32),
        grid=(nrows // rb, c // cb),
        in_specs=[pl.BlockSpec((rb, cb), lambda i, j: (r0 + i, j0 + j)),
                  pl.BlockSpec((kw, cb), lambda i, j: (0, j)),
                  pl.BlockSpec((1, cb), lambda i, j: (0, j))],
        out_specs=pl.BlockSpec((rb, cb), lambda i, j: (i, j)),
        compiler_params=_cparams(("parallel", "parallel")),
        name="ssd_conv",
    )(x, w, b.reshape(1, c))


def _ssd_scan_kernel(x_ref, b_ref, c_ref, dt_ref, dtt_ref, bias_ref, alog_ref, biast_ref, alogt_ref,
                     tri_ref, trit_ref, e_ref, s0_ref, y_ref, sf_ref, st_ref,
                     *, heads, hdim, nstate, groups, has_init):
    d = pl.program_id(0)
    c = pl.program_id(2)
    nc = pl.num_programs(2)
    hpg = heads // groups
    gw = hpg * hdim

    @pl.when(c == 0)
    def _():
        if has_init:
            for g in range(groups):
                st_ref[pl.ds(g * nstate, nstate), :] = s0_ref[0, 0, pl.ds(g * gw, gw), :].T
        else:
            st_ref[...] = jnp.zeros_like(st_ref)

    tri = tri_ref[0]
    trit = trit_ref[0]
    lch = tri.shape[0]
    e = e_ref[...]
    dtp = _softplus(dt_ref[...] + bias_ref[0])
    a = -jnp.exp(alog_ref[0])
    da = dtp * a
    cum = _dot_exact_l(tri, da)
    dat = _softplus(dtt_ref[...] + biast_ref[0]) * (-jnp.exp(alogt_ref[0]))
    cumt = _dot_exact_r(dat, trit)
    total = jnp.sum(da, axis=0, keepdims=True)
    tot8 = jnp.broadcast_to(total, (SUBLANES, LANES))
    dt_full = _dot_exact_r(dtp, e)
    din_full = _dot_exact_r(jnp.exp(cum), e)
    dst_full = _dot_exact_r(jnp.exp(total - cum), e)
    tot_full = _dot_exact_r(jnp.exp(tot8), e)[0:1, :]
    x = x_ref[...]
    xdt = x * dt_full
    xdec = (xdt * dst_full).astype(BF16)
    xdt_b = xdt.astype(BF16)
    bm = b_ref[...]
    cm = c_ref[...]
    visible = tri > 0
    for g in range(groups):
        bg = bm[:, g * nstate:(g + 1) * nstate]
        cg = cm[:, g * nstate:(g + 1) * nstate].astype(BF16)
        bgb = bg.astype(BF16)
        cb = lax.dot_general(cg, bgb, (((1,), (1,)), ((), ())), preferred_element_type=F32)
        st_g = st_ref[pl.ds(g * nstate, nstate), :]
        y_off = _dot(cg, st_g.astype(BF16)) * din_full[:, g * gw:(g + 1) * gw]
        for hh in range(hpg):
            h = g * hpg + hh
            seg = cum[:, h:h + 1] - cumt[h:h + 1, :]
            lmat = jnp.exp(jnp.where(visible, seg, -jnp.inf))
            gmat = (cb * lmat).astype(BF16)
            yd = _dot(gmat, xdt_b[:, h * hdim:(h + 1) * hdim])
            y_ref[0, :, pl.ds(h * hdim, hdim)] = yd + y_off[:, hh * hdim:(hh + 1) * hdim]
        upd = _dot(bg.T.astype(BF16), xdec[:, g * gw:(g + 1) * gw])
        st_ref[pl.ds(g * nstate, nstate), :] = st_g * tot_full[:, g * gw:(g + 1) * gw] + upd

    @pl.when(c == nc - 1)
    def _():
        for g in range(groups):
            sf_ref[0, 0, pl.ds(g * gw, gw), :] = st_ref[pl.ds(g * nstate, nstate), :].T


def _ssd_scan_call(xs, bm, cm, zdt, dt_blk, dtt, row0, bsz, seq_len, pw, s0):
    hp = xs.shape[1]
    gn = bm.shape[1]
    heads = pw["heads"]
    hdim = hp // heads
    nstate = gn // SSD_GROUPS
    lch = min(SSD_CHUNK, seq_len)
    nc = seq_len // lch
    assert seq_len % lch == 0 and row0 % lch == 0
    c0 = row0 // lch
    has_init = s0 is not None
    if s0 is None:
        s0 = jnp.zeros((1, 1, hp, nstate), F32)

    def cidx(d, b, c):
        return b * nc + c + d * (nc - 1 - 2 * c)

    row = lambda d, b, c: (cidx(d, b, c), 0)
    s0_map = (lambda d, b, c: (b, d, 0, 0)) if has_init else (lambda d, b, c: (0, 0, 0, 0))
    dirp = pl.BlockSpec((1, 1, LANES), lambda d, b, c: (d, 0, 0))
    dirt = pl.BlockSpec((1, heads, 1), lambda d, b, c: (d, 0, 0))
    return pl.pallas_call(
        functools.partial(_ssd_scan_kernel, heads=heads, hdim=hdim, nstate=nstate,
                          groups=SSD_GROUPS, has_init=has_init),
        out_shape=[jax.ShapeDtypeStruct((N_DIR, bsz * seq_len, hp), F32),
                   jax.ShapeDtypeStruct((bsz, N_DIR, hp, nstate), F32)],
        grid=(N_DIR, bsz, nc),
        in_specs=[pl.BlockSpec((lch, hp), row),
                  pl.BlockSpec((lch, gn), row),
                  pl.BlockSpec((lch, gn), row),
                  pl.BlockSpec((lch, LANES), lambda d, b, c: (c0 + cidx(d, b, c), dt_blk)),
                  pl.BlockSpec((heads, lch), lambda d, b, c: (0, c0 + cidx(d, b, c))),
                  dirp, dirp, dirt, dirt,
                  pl.BlockSpec((1, lch, lch), lambda d, b, c: (d, 0, 0)),
                  pl.BlockSpec((1, lch, lch), lambda d, b, c: (d, 0, 0)),
                  pl.BlockSpec((LANES, hp), lambda d, b, c: (0, 0)),
                  pl.BlockSpec((1, 1, hp, nstate), s0_map)],
        out_specs=[pl.BlockSpec((1, lch, hp), lambda d, b, c: (d, cidx(d, b, c), 0)),
                   pl.BlockSpec((1, 1, hp, nstate), lambda d, b, c: (b, d, 0, 0))],
        scratch_shapes=[pltpu.VMEM((gn, hp // SSD_GROUPS), F32)],
        compiler_params=_cparams(("arbitrary", "arbitrary", "arbitrary")),
        name="ssd_scan",
    )(xs, bm, cm, zdt, dtt, pw["bias"], pw["alog"], pw["bias_t"], pw["alog_t"],
      pw["tri"][:, :lch, :lch], pw["trit"][:, :lch, :lch], pw["expand"], s0)


def _ssd_post_kernel(x_ref, yf_ref, yb_ref, z_ref, d_ref, g_ref, acc_ref, o_ref):
    del acc_ref
    y = (d_ref[...] * x_ref[...] + yf_ref[0] + yb_ref[0]) * _silu(z_ref[...])
    y = y * lax.rsqrt(jnp.mean(y * y, axis=-1, keepdims=True) + EPS) * g_ref[...]
    o_ref[...] = y.astype(o_ref.dtype)


def _ssd_post_call(xs, ydir, zz, row0, d_full, g, tm, acc):
    n, hp = xs.shape
    r0 = row0 // tm
    return pl.pallas_call(
        _ssd_post_kernel,
        out_shape=jax.ShapeDtypeStruct(acc.shape, acc.dtype),
        grid=(n // tm,),
        in_specs=[pl.BlockSpec((tm, hp), lambda i: (i, 0)),
                  pl.BlockSpec((1, tm, hp), lambda i: (0, i, 0)),
                  pl.BlockSpec((1, tm, hp), lambda i: (1, i, 0)),
                  pl.BlockSpec((tm, hp), lambda i: (r0 + i, 0)),
                  pl.BlockSpec((1, hp), lambda i: (0, 0)),
                  pl.BlockSpec((1, hp), lambda i: (0, 0)),
                  pl.BlockSpec(memory_space=pl.ANY)],
        out_specs=pl.BlockSpec((tm, hp), lambda i: (r0 + i, 0)),
        input_output_aliases={6: 0},
        compiler_params=_cparams(("parallel",)),
        name="ssd_post",
    )(xs, ydir, ydir, zz, d_full, g, acc)


def _s5_kernel(u_ref, bmat_ref, cmat_ref, lam_ref, s0_ref, y_ref, sf_ref,
               st_ref, buf_ref, *, slab_in, slab_state):
    d = pl.program_id(0)
    tb = pl.program_id(2)
    ntb = pl.num_programs(2)
    nb, steps = u_ref.shape[0], u_ref.shape[1]
    nslab = u_ref.shape[2] // slab_in
    spl = buf_ref.shape[0]
    ncol = slab_state // LANES
    assert slab_in == LANES

    @pl.when(tb == 0)
    def _():
        st_ref[...] = s0_ref[0]

    for s_base in range(0, nslab, spl):
        lam = []
        init = []
        for q in range(spl):
            s = s_base + q
            u = jnp.swapaxes(u_ref[:, :, pl.ds(s * slab_in, slab_in)], 0, 1)
            bu = _dot(u.reshape(steps * nb, slab_in).astype(BF16), bmat_ref[0, s])
            for c in range(2 * ncol):
                buf_ref[q, c] = bu[:, c * LANES:(c + 1) * LANES]
            for c in range(ncol):
                lanes = pl.ds(s * slab_state + c * LANES, LANES)
                lam.append((jnp.broadcast_to(lam_ref[0, 0, :, lanes], (nb, LANES)),
                            jnp.broadcast_to(lam_ref[0, 1, :, lanes], (nb, LANES))))
                init.append(st_ref[0, :, lanes])
                init.append(st_ref[1, :, lanes])

        def step(i, carry):
            te = i + d * (steps - 1 - 2 * i)
            rows = pl.ds(pl.multiple_of(te * nb, nb), nb)
            new = []
            for q in range(spl):
                for c in range(ncol):
                    lr, li = lam[q * ncol + c]
                    s_re = carry[2 * (q * ncol + c)]
                    s_im = carry[2 * (q * ncol + c) + 1]
                    n_re = lr * s_re - li * s_im + buf_ref[q, c, rows, :]
                    n_im = lr * s_im + li * s_re + buf_ref[q, ncol + c, rows, :]
                    buf_ref[q, c, rows, :] = n_re
                    buf_ref[q, ncol + c, rows, :] = n_im
                    new += [n_re, n_im]
            return tuple(new)

        fin = lax.fori_loop(0, steps, step, tuple(init), unroll=4)
        for q in range(spl):
            s = s_base + q
            for c in range(ncol):
                lanes = pl.ds(s * slab_state + c * LANES, LANES)
                st_ref[0, :, lanes] = fin[2 * (q * ncol + c)]
                st_ref[1, :, lanes] = fin[2 * (q * ncol + c) + 1]
            states = jnp.concatenate([buf_ref[q, c] for c in range(2 * ncol)], axis=1)
            y = _dot(states.astype(BF16), cmat_ref[0, s])
            y_ref[0, :, :, pl.ds(s * slab_in, slab_in)] = jnp.swapaxes(y.reshape(steps, nb, slab_in), 0, 1)

    @pl.when(tb == ntb - 1)
    def _():
        sf_ref[0] = st_ref[...]


def _s5_call(u3, seq0, bsz, pw, s0):
    _, t, w = u3.shape
    nb = min(SUBLANES, bsz)
    tb = min(S5_TB, t)
    assert t % tb == 0 and bsz % nb == 0 and seq0 % nb == 0
    ntb = t // tb
    b0 = seq0 // nb
    gp = pw["lam"].shape[-1]
    slab_in = pw["slab_in"]
    slab_state = pw["slab_state"]
    nslab = w // slab_in
    spl = 2 if nslab % 2 == 0 else 1

    def tidx(d, i):
        return i + d * (ntb - 1 - 2 * i)

    return pl.pallas_call(
        functools.partial(_s5_kernel, slab_in=slab_in, slab_state=slab_state),
        out_shape=[jax.ShapeDtypeStruct((N_DIR, bsz, t, w), F32),
                   jax.ShapeDtypeStruct((N_DIR, 2, bsz, gp), F32)],
        grid=(N_DIR, bsz // nb, ntb),
        in_specs=[pl.BlockSpec((nb, tb, w), lambda d, b, i: (b0 + b, tidx(d, i), 0)),
                  pl.BlockSpec((1, nslab, slab_in, 2 * slab_state), lambda d, b, i: (d, 0, 0, 0)),
                  pl.BlockSpec((1, nslab, 2 * slab_state, slab_in), lambda d, b, i: (d, 0, 0, 0)),
                  pl.BlockSpec((1, 2, 1, gp), lambda d, b, i: (d, 0, 0, 0)),
                  pl.BlockSpec((1, 2, nb, gp), lambda d, b, i: (d, 0, b, 0))],
        out_specs=[pl.BlockSpec((1, nb, tb, w), lambda d, b, i: (d, b, tidx(d, i), 0)),
                   pl.BlockSpec((1, 2, nb, gp), lambda d, b, i: (d, 0, b, 0))],
        scratch_shapes=[pltpu.VMEM((2, nb, gp), F32),
                        pltpu.VMEM((spl, 2 * slab_state // LANES, nb * tb, LANES), F32)],
        compiler_params=_cparams(("arbitrary", "arbitrary", "arbitrary")),
        name="s5_scan",
    )(u3, pw["bmat"], pw["cmat"], pw["lam"], s0)


def _s5_post_kernel(u_ref, yf_ref, yb_ref, d_ref, acc_ref, o_ref):
    del acc_ref
    y = d_ref[...] * u_ref[...] + yf_ref[0] + yb_ref[0]
    o_ref[...] = jax.nn.gelu(y).astype(o_ref.dtype)


def _s5_post_call(u, row0, ydir, d_full, tm, acc):
    _, m, w = ydir.shape
    r0 = row0 // tm
    return pl.pallas_call(
        _s5_post_kernel,
        out_shape=jax.ShapeDtypeStruct(acc.shape, acc.dtype),
        grid=(m // tm,),
        in_specs=[pl.BlockSpec((tm, w), lambda i: (r0 + i, 0)),
                  pl.BlockSpec((1, tm, w), lambda i: (0, i, 0)),
                  pl.BlockSpec((1, tm, w), lambda i: (1, i, 0)),
                  pl.BlockSpec((1, w), lambda i: (0, 0)),
                  pl.BlockSpec(memory_space=pl.ANY)],
        out_specs=pl.BlockSpec((tm, w), lambda i: (r0 + i, 0)),
        input_output_aliases={4: 0},
        compiler_params=_cparams(("parallel",)),
        name="s5_post",
    )(u, ydir, ydir, d_full, acc)


def _head_selectors(cb, head):
    nsel = LANES
    col = np.arange(cb)[:, None] // head
    sel = (col == np.arange(nsel)[None, :]).astype(np.float32)
    return jnp.asarray(sel, BF16), jnp.asarray(sel.T, BF16)


def _s5_weights(p, l, slab_groups):
    g, pst = p["s5_lambda_re"].shape[2:]
    cg = p["s5_b_re"].shape[-1]
    nslab = g // slab_groups
    eye = jnp.eye(slab_groups, dtype=F32)
    bmats, cmats, lams = [], [], []
    for d in range(N_DIR):
        lam_re, lam_im = p["s5_lambda_re"][l, d], p["s5_lambda_im"][l, d]
        delta = jnp.exp(p["s5_log_dt"][l, d])[:, None]
        mag = jnp.exp(lam_re * delta)
        lb_re, lb_im = mag * jnp.cos(lam_im * delta), mag * jnp.sin(lam_im * delta)
        den = lam_re * lam_re + lam_im * lam_im
        q_re = ((lb_re - 1.0) * lam_re + lb_im * lam_im) / den
        q_im = (lb_im * lam_re - (lb_re - 1.0) * lam_im) / den
        b_re, b_im = p["s5_b_re"][l, d], p["s5_b_im"][l, d]
        bb_re = q_re[..., None] * b_re - q_im[..., None] * b_im
        bb_im = q_re[..., None] * b_im + q_im[..., None] * b_re

        def in_blocks(bb):
            x = bb.reshape(nslab, slab_groups, pst, cg)
            return jnp.einsum("sgpc,gh->sgchp", x, eye).reshape(nslab, slab_groups * cg, slab_groups * pst)

        def out_blocks(cc):
            x = cc.reshape(nslab, slab_groups, cg, pst)
            return jnp.einsum("sgcp,gh->sgphc", x, eye).reshape(nslab, slab_groups * pst, slab_groups * cg)

        bmats.append(jnp.concatenate([in_blocks(bb_re), in_blocks(bb_im)], axis=-1))
        cmats.append(jnp.concatenate([out_blocks(p["s5_c_re"][l, d]), -out_blocks(p["s5_c_im"][l, d])], axis=-2))
        lams.append(jnp.stack([lb_re.reshape(1, g * pst), lb_im.reshape(1, g * pst)]))
    return dict(bmat=jnp.stack(bmats).astype(BF16), cmat=jnp.stack(cmats).astype(BF16),
                lam=jnp.stack(lams), slab_in=slab_groups * cg, slab_state=slab_groups * pst)


def _pad_cols(w, n):
    return jnp.pad(w, ((0, 0), (0, n - w.shape[1])))


def kernel(x_prompt, x_sample, state_rwkv, state_ssd, state_s5_re, state_s5_im, c, c_ctx, w_mod, b_mod, norm_g, ffn_w_in, ffn_w_out, w_in, rwkv_mu, rwkv_w0, rwkv_w2, rwkv_a0, rwkv_a2, rwkv_g2, rwkv_k_k, rwkv_k_a, rwkv_r_k, rwkv_ln_g, rwkv_ln_b, w_proj_a, ssd_conv_w, ssd_conv_b, ssd_dt_bias, ssd_a_log, ssd_d, ssd_norm_g, w_proj_b, s5_lambda_re, s5_lambda_im, s5_log_dt, s5_b_re, s5_b_im, s5_c_re, s5_c_im, s5_d, w_proj_c, w_out, final_norm_g):
    p = dict(s5_lambda_re=s5_lambda_re, s5_lambda_im=s5_lambda_im, s5_log_dt=s5_log_dt,
             s5_b_re=s5_b_re, s5_b_im=s5_b_im, s5_c_re=s5_c_re, s5_c_im=s5_c_im)
    bp, tp, dm = x_prompt.shape
    bs, ts, _ = x_sample.shape
    n_p, n_s = bp * tp, bs * ts
    m = n_p + n_s
    depth = w_mod.shape[0]
    d_ff = ffn_w_out.shape[2]
    ffn_w_out_bf = ffn_w_out.astype(BF16)
    rw = rwkv_k_k.shape[1]
    rh = rw // RWKV_HEAD
    rd, ra, rg = rwkv_w2.shape[2], rwkv_a2.shape[2], rwkv_g2.shape[1]
    lora = rd + ra + rg
    sh = ssd_d.shape[1]
    sw = ssd_norm_g.shape[1]
    xbc_w = ssd_conv_w.shape[2]
    gn = (xbc_w - sw) // 2
    cw = s5_d.shape[1]
    s5_g, s5_p = s5_lambda_re.shape[2:]
    s5_cg = cw // s5_g
    slab_groups = max(1, min(s5_g, LANES // s5_cg))

    tm = _pick(math.gcd(n_p, ts), 1024, SUBLANES)
    tm_wide = _pick(m, 2 * tm, tm)
    n_tiles = m // tm
    tile_cond = np.array([0 if i * tm < n_p else 1 + (i * tm - n_p) // ts for i in range(n_tiles)])

    x = jnp.concatenate([x_prompt.reshape(n_p, dm), x_sample.reshape(n_s, dm)], axis=0)
    ncond = 1 + bs
    cond = jnp.concatenate([c_ctx[None, :], c], axis=0)
    cond8 = jnp.pad(cond, ((0, _round_up(ncond, SUBLANES) - ncond), (0, 0)))

    groups = [dict(b=bp, t=tp, row0=0, n=n_p, grid_w=None),
              dict(b=bs, t=ts, row0=n_p, n=n_s, grid_w=GRID_W)]

    sel, selt = _head_selectors(LANES, RWKV_HEAD)
    tri_f = np.tril(np.ones((SSD_CHUNK, SSD_CHUNK), np.float32))
    tri = jnp.asarray(np.stack([tri_f, tri_f.T]), BF16)
    trit = jnp.asarray(np.stack([tri_f.T, tri_f]), BF16)
    expand = jnp.asarray((np.arange(LANES)[:, None] == (np.arange(sw)[None, :] // (sw // sh))).astype(np.float32), BF16)

    new_a, new_b, new_re, new_im = [], [], [], []
    for l in range(depth):
        mod = _mod_call(cond8, w_mod, b_mod, l)
        mods = mod.reshape(-1, N_MOD, dm)[tile_cond]
        sh1, sc1, g1, sh2, sc2, g2, sh3, sc3, g3 = [mods[:, i:i + 1, :] for i in range(N_MOD)]

        def ffn(x, idx, shv, scv, gv):
            wi = ffn_w_in[l, idx].astype(BF16)
            h = _norm_mod_call(x, norm_g[l, idx * 2], shv, scv, tm)
            a = _mm_swiglu_call(h, wi, wi[:, d_ff:], tm)
            return _mm_res_call(a, ffn_w_out_bf, x, gv, 0.5, tm, w_index=(l, idx))

        x = ffn(x, 0, sh1, sc1, g1)

        h = _norm_mod_call(x, norm_g[l, 1], sh2, sc2, tm)
        wl_in = w_in[l]
        offs = np.cumsum([0, rw, rw, rw, lora, sw, sw, gn, gn, sh, cw, 3 * dm])
        segs = [wl_in[:, offs[i]:offs[i + 1]].astype(BF16) for i in range(11)]
        w_small = jnp.concatenate([segs[3], segs[6], segs[7], _pad_cols(segs[8], LANES)], axis=1)
        assert lora % gn == 0 and (lora + 2 * gn) % LANES == 0
        bm_blk, cm_blk, dt_blk = lora // gn, lora // gn + 1, (lora + 2 * gn) // LANES
        zr, zk, zv, zz, zxs, zc, zg, zs = [_mm_call(h, wseg, tm_wide)
                                           for wseg in (segs[0], segs[1], segs[2], segs[4], segs[5],
                                                        segs[9], segs[10], w_small)]
        dtt = zs[:, lora + 2 * gn:lora + 2 * gn + sh].T

        mu = rwkv_mu[l]
        pw_r = dict(ranks=(rd, ra, rg), sel=sel, selt=selt,
                    mu_r=mu[None, :rw], mu_k=mu[None, rw:2 * rw], mu_v=mu[None, 2 * rw:3 * rw],
                    mu_l=mu[None, 3 * rw:],
                    k_k=rwkv_k_k[l][None], k_a=rwkv_k_a[l][None], r_k=rwkv_r_k[l].reshape(1, rw),
                    w0=rwkv_w0[l][:, None, :], a0=rwkv_a0[l][:, None, :],
                    w2=rwkv_w2[l].astype(BF16), a2=rwkv_a2[l].astype(BF16), g2=rwkv_g2[l].astype(BF16),
                    ln_g=rwkv_ln_g[l][None], ln_b=rwkv_ln_b[l][None])
        y_a, fin_a = jnp.zeros((m, rw), BF16), None
        bpb = (LANES // 2) // rh
        for gi, gr in enumerate(groups):
            b_, t_ = gr["b"], gr["t"]
            nblk = b_ // bpb
            rv_, wb_, kk_, bonus_, g_ = _rwkv_prep_call(
                zr, zk, zv, zs, gr["row0"], gr["n"], t_, gr["grid_w"], pw_r)
            if gi == 0:
                s0c = jnp.zeros((RWKV_HEAD, RWKV_HEAD, nblk * LANES), F32)
            else:
                s0c = state_rwkv[:, l].reshape(nblk, bpb, N_DIR, rh, RWKV_HEAD, RWKV_HEAD)
                s0c = s0c.transpose(5, 4, 0, 2, 1, 3).reshape(RWKV_HEAD, RWKV_HEAD, nblk * LANES)
            o_f, o_b, sf_c = _rwkv_scan_call(rv_, wb_, kk_, s0c)
            y_a = _rwkv_post_call(o_f, o_b, bonus_, g_, pw_r, gr["row0"], y_a)
            if gi == 0:
                fin_a = sf_c.reshape(RWKV_HEAD, RWKV_HEAD, nblk, N_DIR, bpb, rh)
                fin_a = fin_a.transpose(2, 4, 3, 5, 1, 0).reshape(b_, N_DIR, rh, RWKV_HEAD, RWKV_HEAD)
        new_a.append(fin_a)

        cw_l, cb_l = ssd_conv_w[l], ssd_conv_b[l]
        pw_s = dict(heads=sh, tri=tri, trit=trit, expand=expand,
                    bias=jnp.pad(ssd_dt_bias[l], ((0, 0), (0, LANES - sh)))[:, None, :],
                    alog=jnp.pad(ssd_a_log[l], ((0, 0), (0, LANES - sh)))[:, None, :],
                    bias_t=ssd_dt_bias[l][:, :, None], alog_t=ssd_a_log[l][:, :, None])
        y_b, fin_b = jnp.zeros((m, sw), BF16), None
        for gi, gr in enumerate(groups):
            xs_ = _conv_silu_call(zxs, gr["row0"], gr["n"], gr["t"], cw_l[:, :sw], cb_l[:sw])
            bm_ = _conv_silu_call(zs, gr["row0"], gr["n"], gr["t"], cw_l[:, sw:sw + gn], cb_l[sw:sw + gn], bm_blk)
            cm_ = _conv_silu_call(zs, gr["row0"], gr["n"], gr["t"], cw_l[:, sw + gn:], cb_l[sw + gn:], cm_blk)
            s0s = None if gi == 0 else state_ssd[:, l].reshape(gr["b"], N_DIR, sw, gn // SSD_GROUPS)
            ydir, sf_s = _ssd_scan_call(xs_, bm_, cm_, zs, dt_blk, dtt, gr["row0"], gr["b"], gr["t"], pw_s, s0s)
            y_b = _ssd_post_call(xs_, ydir, zz, gr["row0"],
                                 jnp.repeat(ssd_d[l], sw // sh)[None, :], ssd_norm_g[l][None, :], tm, y_b)
            if gi == 0:
                fin_b = sf_s.reshape(gr["b"], N_DIR, sh, sw // sh, gn // SSD_GROUPS)
        new_b.append(fin_b)

        pw_c = _s5_weights(p, l, slab_groups)
        y_c, fin_re, fin_im = jnp.zeros((m, cw), BF16), None, None
        for gi, gr in enumerate(groups):
            b_, t_ = gr["b"], gr["t"]
            assert gr["row0"] % t_ == 0
            if gi == 0:
                s0 = jnp.zeros((N_DIR, 2, b_, s5_g * s5_p), F32)
            else:
                s0 = jnp.stack([state_s5_re[:, l], state_s5_im[:, l]])
                s0 = s0.reshape(2, b_, N_DIR, s5_g * s5_p).transpose(2, 0, 1, 3)
            y_dir, sf = _s5_call(zc.reshape(m // t_, t_, cw), gr["row0"] // t_, b_, pw_c, s0)
            y_dir = y_dir.reshape(N_DIR, gr["n"], cw)
            y_c = _s5_post_call(zc, gr["row0"], y_dir, s5_d[l][None, :], tm, y_c)
            if gi == 0:
                fin = sf.reshape(N_DIR, 2, b_, s5_g, s5_p).transpose(1, 2, 0, 3, 4)
                fin_re, fin_im = fin[0], fin[1]
        new_re.append(fin_re)
        new_im.append(fin_im)

        wc = w_proj_c[l]
        merged = _merge_call(y_a, y_b, y_c, zg, w_proj_a[l].astype(BF16), w_proj_b[l].astype(BF16),
                             jnp.stack([wc[:, :dm], wc[:, dm:]]).astype(BF16), tm)
        x = _mm_res_call(merged, w_out[l].astype(BF16), x, g2, 1.0, tm)

        x = ffn(x, 1, sh3, sc3, g3)

    y_p = _final_norm_call(x, 0, n_p, final_norm_g, tm)
    y_s = _final_norm_call(x, n_p, n_s, final_norm_g, tm)
    return (y_p.reshape(bp, tp, dm), y_s.reshape(bs, ts, dm),
            jnp.stack(new_a, axis=1), jnp.stack(new_b, axis=1),
            jnp.stack(new_re, axis=1), jnp.stack(new_im, axis=1))
```

```python
import functools
import math

import jax
import jax.numpy as jnp
import numpy as np
from jax import lax
from jax.experimental import pallas as pl
from jax.experimental.pallas import tpu as pltpu

F32 = jnp.float32
BF16 = jnp.bfloat16

LANES = 128
SUBLANES = 8
VMEM_LIMIT = 56 * 1024 * 1024

GRID_W = 64
SSD_CHUNK = 128
SSD_GROUPS = 2
N_DIR = 2
N_MOD = 9
EPS = 1e-6
RWKV_LN_EPS = 64e-5
RWKV_HEAD = 64
RWKV_SCAN_TB = 32
RWKV_SCAN_PARTS = 1
S5_TB = 128


def _cparams(sem):
    return pltpu.CompilerParams(dimension_semantics=sem, vmem_limit_bytes=VMEM_LIMIT)


def _pick(n, target, mult=LANES):
    best = None
    d = mult
    while d <= min(n, target):
        if n % d == 0:
            best = d
        d += mult
    return n if best is None else best


def _seq_rows(seq_len, nrows, row0, cap=2048):
    span = math.gcd(nrows, row0) if row0 else nrows
    assert span % seq_len == 0
    return _pick(span, max(cap, seq_len), seq_len)


def _round_up(n, m):
    return -(-n // m) * m


def _dot(a, b):
    return jnp.dot(a, b, preferred_element_type=F32)


def _split3(x):
    x1 = x.astype(BF16)
    r1 = x - x1.astype(F32)
    x2 = r1.astype(BF16)
    x3 = (r1 - x2.astype(F32)).astype(BF16)
    return x1, x2, x3


def _dot_exact_r(x, sel):
    return sum(_dot(p, sel) for p in _split3(x))


def _dot_exact_l(sel, x):
    return sum(_dot(sel, p) for p in _split3(x))


def _softplus(x):
    return jnp.maximum(x, 0.0) + jnp.log1p(jnp.exp(-jnp.abs(x)))


def _sigmoid(x):
    return 0.5 * (jnp.tanh(0.5 * x) + 1.0)


def _silu(x):
    return x * _sigmoid(x)


def _mod_kernel(c_ref, w_ref, b_ref, o_ref):
    c = c_ref[...]
    a = _silu(c).astype(BF16)
    o_ref[...] = _dot(a, w_ref[0].astype(BF16)) + b_ref[0]


def _mod_call(cond8, w_all, b_all, layer):
    depth, d, n = w_all.shape
    tn = _pick(n, 1024)
    return pl.pallas_call(
        _mod_kernel,
        out_shape=jax.ShapeDtypeStruct((cond8.shape[0], n), F32),
        grid=(n // tn,),
        in_specs=[pl.BlockSpec((cond8.shape[0], d), lambda j: (0, 0)),
                  pl.BlockSpec((1, d, tn), lambda j: (layer, 0, j)),
                  pl.BlockSpec((1, 1, tn), lambda j: (layer, 0, j))],
        out_specs=pl.BlockSpec((cond8.shape[0], tn), lambda j: (0, j)),
        compiler_params=_cparams(("parallel",)),
        name="mod_proj",
    )(cond8, w_all, b_all.reshape(depth, 1, n))


def _norm_mod_kernel(x_ref, g_ref, sh_ref, sc_ref, o_ref):
    x = x_ref[...]
    y = x * lax.rsqrt(jnp.mean(x * x, axis=-1, keepdims=True) + EPS) * g_ref[...]
    o_ref[...] = (y * (1.0 + sc_ref[0]) + sh_ref[0]).astype(o_ref.dtype)


def _norm_mod_call(x, g, sh, sc, tm):
    m, d = x.shape
    return pl.pallas_call(
        _norm_mod_kernel,
        out_shape=jax.ShapeDtypeStruct((m, d), BF16),
        grid=(m // tm,),
        in_specs=[pl.BlockSpec((tm, d), lambda i: (i, 0)),
                  pl.BlockSpec((1, d), lambda i: (0, 0)),
                  pl.BlockSpec((1, 1, d), lambda i: (i, 0, 0)),
                  pl.BlockSpec((1, 1, d), lambda i: (i, 0, 0))],
        out_specs=pl.BlockSpec((tm, d), lambda i: (i, 0)),
        compiler_params=_cparams(("parallel",)),
        name="norm_mod",
    )(x, g.reshape(1, d), sh, sc)


def _final_norm_kernel(x_ref, g_ref, o_ref):
    x = x_ref[...]
    o_ref[...] = x * lax.rsqrt(jnp.mean(x * x, axis=-1, keepdims=True) + EPS) * g_ref[...]


def _final_norm_call(x, row0, nrows, g, tm):
    d = x.shape[1]
    r0 = row0 // tm
    return pl.pallas_call(
        _final_norm_kernel,
        out_shape=jax.ShapeDtypeStruct((nrows, d), F32),
        grid=(nrows // tm,),
        in_specs=[pl.BlockSpec((tm, d), lambda i: (r0 + i, 0)),
                  pl.BlockSpec((1, d), lambda i: (0, 0))],
        out_specs=pl.BlockSpec((tm, d), lambda i: (i, 0)),
        compiler_params=_cparams(("parallel",)),
        name="final_norm",
    )(x, g.reshape(1, d))


def _mm_kernel(a_ref, w_ref, o_ref):
    o_ref[...] = _dot(a_ref[...], w_ref[...]).astype(o_ref.dtype)


def _mm_call(a, w, tm, tn_target=1024, out_dtype=F32):
    m, k = a.shape
    n = w.shape[1]
    tn = _pick(n, tn_target)
    return pl.pallas_call(
        _mm_kernel,
        out_shape=jax.ShapeDtypeStruct((m, n), out_dtype),
        grid=(m // tm, n // tn),
        in_specs=[pl.BlockSpec((tm, k), lambda i, j: (i, 0)),
                  pl.BlockSpec((k, tn), lambda i, j: (0, j))],
        out_specs=pl.BlockSpec((tm, tn), lambda i, j: (i, j)),
        compiler_params=_cparams(("parallel", "parallel")),
        name="mm",
    )(a, w)


def _mm_swiglu_kernel(a_ref, wg_ref, wu_ref, o_ref):
    a = a_ref[...]
    gate = _dot(a, wg_ref[...])
    up = _dot(a, wu_ref[...])
    o_ref[...] = (_silu(gate) * up).astype(o_ref.dtype)


def _mm_swiglu_call(a, w_gate_up, w_up, tm, tn_target=512):
    m, k = a.shape
    n = w_up.shape[1]
    tn = tn_target if n >= tn_target else n
    return pl.pallas_call(
        _mm_swiglu_kernel,
        out_shape=jax.ShapeDtypeStruct((m, n), BF16),
        grid=(m // tm, pl.cdiv(n, tn)),
        in_specs=[pl.BlockSpec((tm, k), lambda i, j: (i, 0)),
                  pl.BlockSpec((k, tn), lambda i, j: (0, j)),
                  pl.BlockSpec((k, tn), lambda i, j: (0, j))],
        out_specs=pl.BlockSpec((tm, tn), lambda i, j: (i, j)),
        compiler_params=_cparams(("parallel", "parallel")),
        name="mm_swiglu",
    )(a, w_gate_up, w_up)


def _mm_res_kernel(a_ref, w_ref, x_ref, g_ref, o_ref, *, coef):
    o_ref[...] = x_ref[...] + (coef * g_ref[0]) * _dot(a_ref[...], w_ref[...])


def _mm_res_call(a, w, x, gate, coef, tm, tn_target=512, w_index=()):
    m, k = a.shape
    n = w.shape[-1]
    tn = _pick(n, tn_target)
    return pl.pallas_call(
        functools.partial(_mm_res_kernel, coef=coef),
        out_shape=jax.ShapeDtypeStruct((m, n), F32),
        grid=(m // tm, n // tn),
        in_specs=[pl.BlockSpec((tm, k), lambda i, j: (i, 0)),
                  pl.BlockSpec((None,) * len(w_index) + (k, tn), lambda i, j: tuple(w_index) + (0, j)),
                  pl.BlockSpec((tm, tn), lambda i, j: (i, j)),
                  pl.BlockSpec((1, 1, tn), lambda i, j: (i, 0, j))],
        out_specs=pl.BlockSpec((tm, tn), lambda i, j: (i, j)),
        compiler_params=_cparams(("parallel", "parallel")),
        name="mm_res",
    )(a, w, x, gate)


def _merge_kernel(ya_ref, yb_ref, yc_ref, ga_ref, gb_ref, gc_ref, wa_ref, wb_ref, wc_ref, o_ref):
    pa = _dot(ya_ref[...], wa_ref[...])
    pb = _dot(yb_ref[...], wb_ref[...])
    yc = yc_ref[...]
    val = _dot(yc, wc_ref[0])
    gate = _dot(yc, wc_ref[1])
    merged = (_sigmoid(ga_ref[...]) * pa + _sigmoid(gb_ref[...]) * pb
              + _sigmoid(gc_ref[...]) * (val * _sigmoid(gate)))
    o_ref[...] = merged.astype(o_ref.dtype)


def _merge_call(ya, yb, yc, zg, wa, wb, wc2, tm, tn_target=512):
    m, ka = ya.shape
    d = wa.shape[1]
    tn = _pick(d, tn_target)
    nj = d // tn
    return pl.pallas_call(
        _merge_kernel,
        out_shape=jax.ShapeDtypeStruct((m, d), BF16),
        grid=(m // tm, nj),
        in_specs=[pl.BlockSpec((tm, ka), lambda i, j: (i, 0)),
                  pl.BlockSpec((tm, yb.shape[1]), lambda i, j: (i, 0)),
                  pl.BlockSpec((tm, yc.shape[1]), lambda i, j: (i, 0)),
                  pl.BlockSpec((tm, tn), lambda i, j: (i, j)),
                  pl.BlockSpec((tm, tn), lambda i, j: (i, nj + j)),
                  pl.BlockSpec((tm, tn), lambda i, j: (i, 2 * nj + j)),
                  pl.BlockSpec((ka, tn), lambda i, j: (0, j)),
                  pl.BlockSpec((yb.shape[1], tn), lambda i, j: (0, j)),
                  pl.BlockSpec((2, yc.shape[1], tn), lambda i, j: (0, 0, j))],
        out_specs=pl.BlockSpec((tm, tn), lambda i, j: (i, j)),
        compiler_params=_cparams(("parallel", "parallel")),
        name="merge",
    )(ya, yb, yc, zg, zg, zg, wa, wb, wc2)


def _row_shift(x, off, t_idx, seq_len):
    rows = x.shape[0]
    rolled = pltpu.roll(x, (-off) % rows, axis=0)
    src = t_idx + off
    ok = jnp.logical_and(src >= 0, src < seq_len)
    return jnp.where(ok, rolled, 0.0)


def _centred_nb(x, t_idx, seq_len, grid_w):
    if grid_w is None:
        return 0.5 * (_row_shift(x, -1, t_idx, seq_len) + _row_shift(x, 1, t_idx, seq_len))
    col = t_idx % grid_w
    left = jnp.where(col >= 1, _row_shift(x, -1, t_idx, seq_len), 0.0)
    right = jnp.where(col < grid_w - 1, _row_shift(x, 1, t_idx, seq_len), 0.0)
    up = _row_shift(x, -grid_w, t_idx, seq_len)
    down = _row_shift(x, grid_w, t_idx, seq_len)
    return 0.25 * (up + down + left + right)


def _rwkv_prep_kernel(zr_ref, zk_ref, zv_ref, zl_ref,
                      mur_ref, muk_ref, muv_ref, mul_ref,
                      kk_w_ref, ka_w_ref, rk_w_ref, w0_ref, a0_ref,
                      w2_ref, a2_ref, g2_ref, sel_ref, selt_ref,
                      rv_o, wb_o, kk_o, bonus_o, g_o,
                      *, seq_len, grid_w, ranks):
    nseq = rv_o.shape[0]

    def emit(write, a, b):
        lane = lax.broadcasted_iota(jnp.int32, a.shape, 1)
        first = lane < RWKV_HEAD
        head0 = jnp.where(first, a, pltpu.roll(b, RWKV_HEAD, axis=1))
        head1 = jnp.where(first, pltpu.roll(a, RWKV_HEAD, axis=1), b)
        for q in range(nseq):
            write(q, 0, head0[q * seq_len:(q + 1) * seq_len])
            write(q, 1, head1[q * seq_len:(q + 1) * seq_len])

    def shifted(ref, mu_ref):
        x = ref[...]
        t_idx = lax.broadcasted_iota(jnp.int32, x.shape, 0) % seq_len
        return x + mu_ref[...] * (_centred_nb(x, t_idx, seq_len, grid_w) - x)

    r = shifted(zr_ref, mur_ref)
    k = shifted(zk_ref, muk_ref)
    v = shifted(zv_ref, muv_ref)
    lo = shifted(zl_ref, mul_ref)
    rd, ra, rg = ranks
    wl = lo[:, :rd]
    al = lo[:, rd:rd + ra]
    gl = lo[:, rd + ra:rd + ra + rg]
    sel = sel_ref[...]
    selt = selt_ref[...]

    def head_sum(x):
        return _dot_exact_r(_dot_exact_r(x, sel), selt)

    kk = k * kk_w_ref[...]
    kk = kk * lax.rsqrt(head_sum(kk * kk) + 1e-12)
    def write_rv(q, hh, val):
        rv_o[q, hh] = val

    emit(write_rv, r, v)
    tw = jnp.tanh(wl).astype(BF16)
    alb = al.astype(BF16)
    for d in range(N_DIR):
        w_log = -_softplus(-(w0_ref[d] + _dot(tw, w2_ref[d]))) - 0.5
        a_d = _sigmoid(a0_ref[d] + _dot(alb, a2_ref[d]))

        def write_wb(q, hh, val, d=d):
            wb_o[d, q, hh] = val

        def write_kk(q, hh, val, d=d):
            kk_o[d, q, hh] = val

        emit(write_wb, jnp.exp(-jnp.exp(w_log)), kk * a_d)
        emit(write_kk, k * (1.0 + (a_d - 1.0) * ka_w_ref[...]), kk)
    bonus_o[...] = head_sum(r * k * rk_w_ref[...]) * v
    g_o[...] = _dot(_sigmoid(gl).astype(BF16), g2_ref[...])


def _rwkv_prep_call(zr, zk, zv, zl, row0, nrows, seq_len, grid_w, pw):
    w = zr.shape[1]
    lw = sum(pw["ranks"])
    cb = LANES
    rb = _seq_rows(seq_len, nrows, row0)
    assert w % cb == 0
    r0 = row0 // rb
    main = pl.BlockSpec((rb, cb), lambda i, j: (r0 + i, j))
    lspec = pl.BlockSpec((rb, lw), lambda i, j: (r0 + i, 0))
    colp = pl.BlockSpec((1, cb), lambda i, j: (0, j))
    dirp = pl.BlockSpec((N_DIR, 1, cb), lambda i, j: (0, 0, j))
    ospec = pl.BlockSpec((rb, cb), lambda i, j: (i, j))
    rd, ra, rg = pw["ranks"]
    nsel = pw["sel"].shape[1]
    assert cb == 2 * RWKV_HEAD
    nseq = rb // seq_len
    bsz = nrows // seq_len
    heads = w // RWKV_HEAD
    pk = jax.ShapeDtypeStruct((bsz, heads, seq_len, cb), F32)
    pkd = jax.ShapeDtypeStruct((N_DIR, bsz, heads, seq_len, cb), F32)
    pspec = pl.BlockSpec((nseq, 2, seq_len, cb), lambda i, j: (i, j, 0, 0))
    pdspec = pl.BlockSpec((N_DIR, nseq, 2, seq_len, cb), lambda i, j: (0, i, j, 0, 0))
    return pl.pallas_call(
        functools.partial(_rwkv_prep_kernel, seq_len=seq_len, grid_w=grid_w, ranks=pw["ranks"]),
        out_shape=[pk, pkd, pkd, jax.ShapeDtypeStruct((nrows, w), F32), jax.ShapeDtypeStruct((nrows, w), F32)],
        grid=(nrows // rb, w // cb),
        in_specs=[main, main, main, lspec,
                  colp, colp, colp, pl.BlockSpec((1, lw), lambda i, j: (0, 0)),
                  colp, colp, colp, dirp, dirp,
                  pl.BlockSpec((N_DIR, rd, cb), lambda i, j: (0, 0, j)),
                  pl.BlockSpec((N_DIR, ra, cb), lambda i, j: (0, 0, j)),
                  pl.BlockSpec((rg, cb), lambda i, j: (0, j)),
                  pl.BlockSpec((cb, nsel), lambda i, j: (0, 0)),
                  pl.BlockSpec((nsel, cb), lambda i, j: (0, 0))],
        out_specs=[pspec, pdspec, pdspec, ospec, ospec],
        compiler_params=_cparams(("parallel", "parallel")),
        name="rwkv_prep",
    )(zr, zk, zv, zl, pw["mu_r"], pw["mu_k"], pw["mu_v"], pw["mu_l"],
      pw["k_k"], pw["k_a"], pw["r_k"], pw["w0"], pw["a0"],
      pw["w2"], pw["a2"], pw["g2"], pw["sel"], pw["selt"])


def _load_time_chunk(ref, t0):
    idx = (0,) * (len(ref.shape) - 4) + (slice(None), slice(None), pl.ds(t0, SUBLANES), slice(None))
    x = ref[idx]
    return x.reshape(x.shape[0] * x.shape[1], SUBLANES, x.shape[-1])


def _store_time_rows(ref, s, val):
    steps, width = ref.shape[-2], ref.shape[-1]
    chains = math.prod(ref.shape[:-2])
    ref.reshape(chains * steps, width)[pl.ds(s, chains, stride=steps), :] = val


def _rwkv_scan_kernel(rva_ref, rvb_ref, wba_ref, wbb_ref, kka_ref, kkb_ref, s0_ref,
                      oa_ref, ob_ref, sf_ref, s_ref, out_ref, *ops_refs):
    i = pl.program_id(1)
    n = s_ref.shape[0]
    nj = n // SUBLANES
    steps = out_ref.shape[0]
    half = LANES // 2
    r_off, v_off, w_off, b_off, kd_off, kk_off = (q * n for q in range(6))

    @pl.when(i == 0)
    def _():
        s_ref[...] = s0_ref[...]

    srcs = ((rva_ref, rvb_ref), (wba_ref, wbb_ref), (kka_ref, kkb_ref))

    nparts = len(ops_refs)
    plen = steps // nparts

    def relayout(part, c):
        t0 = pl.multiple_of(part * plen + c * SUBLANES, SUBLANES)
        tb0 = pl.multiple_of(steps - SUBLANES - (part * plen + c * SUBLANES), SUBLANES)
        for p, (a_ref, b_ref) in enumerate(srcs):
            xa = jnp.swapaxes(_load_time_chunk(a_ref, t0), 0, 1)
            xb = jnp.swapaxes(_load_time_chunk(b_ref, tb0), 0, 1)
            for q in range(SUBLANES):
                x = jnp.concatenate([xa[q], xb[SUBLANES - 1 - q]], axis=0)
                ops_refs[part][c * SUBLANES + q, pl.ds(p * LANES, LANES), :] = x.T

    def unlay(t):
        tb_ = steps - 1 - t
        ot = out_ref[t].T
        _store_time_rows(oa_ref, t, ot[:half])
        _store_time_rows(ob_ref, tb_, ot[half:])

    out_ref[0] = jnp.zeros(out_ref.shape[1:], F32)

    def relayout_chunk(c, carry):
        relayout(0, c)
        return carry

    lax.fori_loop(0, plen // SUBLANES, relayout_chunk, 0)

    def state(k, j):
        return s_ref[k, pl.ds(j * SUBLANES, SUBLANES), :]

    for part in range(nparts):
        ops_ref = ops_refs[part]

        def row(s, r, ops_ref=ops_ref):
            return jnp.broadcast_to(ops_ref[s, pl.ds(r, 1), :], (SUBLANES, LANES))

        acc0 = [None] * nj
        for k in range(n):
            kkb = row(0, kk_off + k)
            for j in range(nj):
                p = state(k, j) * kkb
                acc0[j] = p if acc0[j] is None else acc0[j] + p

        def step(s, acc, part=part, ops_ref=ops_ref, row=row):
            t = part * plen + s
            unlay(jnp.maximum(t - 1, 0))
            s_next = jnp.minimum(s + 1, plen - 1)
            vv = [ops_ref[s, pl.ds(v_off + j * SUBLANES, SUBLANES), :] for j in range(nj)]
            out = [None] * nj
            acc_next = [None] * nj
            for k in range(n):
                wb = row(s, w_off + k)
                bb = row(s, b_off + k)
                kdb = row(s, kd_off + k)
                rb = row(s, r_off + k)
                kkn = row(s_next, kk_off + k)
                for j in range(nj):
                    s_new = state(k, j) * wb - acc[j] * bb + vv[j] * kdb
                    s_ref[k, pl.ds(j * SUBLANES, SUBLANES), :] = s_new
                    q = s_new * rb
                    out[j] = q if out[j] is None else out[j] + q
                    p = s_new * kkn
                    acc_next[j] = p if acc_next[j] is None else acc_next[j] + p
            for j in range(nj):
                out_ref[t, pl.ds(j * SUBLANES, SUBLANES), :] = out[j]
            return tuple(acc_next)

        lax.fori_loop(0, plen, step, tuple(acc0))
    unlay(steps - 1)

    @pl.when(i == pl.num_programs(1) - 1)
    def _():
        sf_ref[...] = s_ref[...]


def _rwkv_scan_call(rv, wb, kk, s0):
    bsz, heads, t, _ = rv.shape
    n = RWKV_HEAD
    bpb = (LANES // 2) // heads
    assert bpb * heads * 2 == LANES and bsz % bpb == 0
    nblk = bsz // bpb
    tb = min(RWKV_SCAN_TB, t)
    nparts = RWKV_SCAN_PARTS
    assert t % tb == 0 and tb % nparts == 0
    nt = t // tb
    fwd4 =pl.BlockSpec((bpb, heads, tb, LANES), lambda g, i: (g, 0, i, 0))
    bwd4 = pl.BlockSpec((bpb, heads, tb, LANES), lambda g, i: (g, 0, nt - 1 - i, 0))
    fwd5 = pl.BlockSpec((1, bpb, heads, tb, LANES), lambda g, i: (0, g, 0, i, 0))
    bwd5 = pl.BlockSpec((1, bpb, heads, tb, LANES), lambda g, i: (1, g, 0, nt - 1 - i, 0))
    st = pl.BlockSpec((n, n, LANES), lambda g, i: (0, 0, g))
    o_sds = jax.ShapeDtypeStruct((bsz, heads, t, n), F32)
    return pl.pallas_call(
        _rwkv_scan_kernel,
        out_shape=[o_sds, o_sds, jax.ShapeDtypeStruct((n, n, nblk * LANES), F32)],
        grid=(nblk, nt),
        in_specs=[fwd4, bwd4, fwd5, bwd5, fwd5, bwd5, st],
        out_specs=[pl.BlockSpec((bpb, heads, tb, n), lambda g, i: (g, 0, i, 0)),
                   pl.BlockSpec((bpb, heads, tb, n), lambda g, i: (g, 0, nt - 1 - i, 0)),
                   st],
        scratch_shapes=[pltpu.VMEM((n, n, LANES), F32),
                        pltpu.VMEM((tb, n, LANES), F32)]
        + [pltpu.VMEM((tb // nparts, 6 * n, LANES), F32)] * nparts,
        compiler_params=_cparams(("parallel", "arbitrary")),
        name="rwkv_scan",
    )(rv, rv, wb, wb, kk, kk, s0)


def _rwkv_post_kernel(of_ref, ob_ref, bonus_ref, g_ref, lng_ref, lnb_ref, sel_ref, selt_ref, acc_ref, o_ref,
                      *, head):
    del acc_ref
    sel = sel_ref[...]
    selt = selt_ref[...]

    def head_sum(x):
        return _dot_exact_r(_dot_exact_r(x, sel), selt)

    def rows(ref):
        parts = [jnp.concatenate([ref[q, 0], ref[q, 1]], axis=1) for q in range(ref.shape[0])]
        return parts[0] if len(parts) == 1 else jnp.concatenate(parts, axis=0)

    o = rows(of_ref) + rows(ob_ref)
    mu = head_sum(o) * (1.0 / head)
    dlt = o - mu
    var = head_sum(dlt * dlt) * (1.0 / head)
    on = dlt * lax.rsqrt(var + RWKV_LN_EPS) * lng_ref[...] + lnb_ref[...]
    o_ref[...] = ((on + bonus_ref[...]) * g_ref[...]).astype(o_ref.dtype)


def _rwkv_post_call(of, ob, bonus, g, pw, row0, acc):
    bsz, heads, t, n = of.shape
    m, w = bonus.shape
    cb = LANES
    nsel = pw["sel"].shape[1]
    tm = _seq_rows(t, m, row0)
    r0 = row0 // tm
    nseq = tm // t
    main = pl.BlockSpec((tm, cb), lambda i, j: (i, j))
    colp = pl.BlockSpec((1, cb), lambda i, j: (0, j))
    ospec = pl.BlockSpec((nseq, 2, t, n), lambda i, j: (i, j, 0, 0))
    return pl.pallas_call(
        functools.partial(_rwkv_post_kernel, head=RWKV_HEAD),
        out_shape=jax.ShapeDtypeStruct(acc.shape, acc.dtype),
        grid=(m // tm, w // cb),
        in_specs=[ospec, ospec, main, main, colp, colp,
                  pl.BlockSpec((cb, nsel), lambda i, j: (0, 0)),
                  pl.BlockSpec((nsel, cb), lambda i, j: (0, 0)),
                  pl.BlockSpec(memory_space=pl.ANY)],
        out_specs=pl.BlockSpec((tm, cb), lambda i, j: (r0 + i, j)),
        input_output_aliases={8: 0},
        compiler_params=_cparams(("parallel", "parallel")),
        name="rwkv_post",
    )(of, ob, bonus, g, pw["ln_g"], pw["ln_b"], pw["sel"], pw["selt"], acc)


def _conv_silu_kernel(x_ref, w_ref, b_ref, o_ref, *, seq_len):
    x = x_ref[...]
    rows = x.shape[0]
    kw = w_ref.shape[0]
    t_idx = lax.broadcasted_iota(jnp.int32, x.shape, 0) % seq_len
    y = b_ref[...] + jnp.zeros_like(x)
    for j in range(kw):
        off = j - kw // 2
        xs = x if off == 0 else _row_shift(x, off, t_idx, seq_len)
        y = y + w_ref[pl.ds(j, 1), :] * xs
    o_ref[...] = _silu(y)


def _conv_silu_call(x, row0, nrows, seq_len, w, b, col_blk=0):
    kw, c = w.shape
    cb = _pick(c, 256)
    rb = _seq_rows(seq_len, nrows, row0)
    r0 = row0 // rb
    j0 = col_blk * (c // cb)
    return pl.pallas_call(
        functools.partial(_conv_silu_kernel, seq_len=seq_len),
        out_shape=jax.ShapeDtypeStruct((nrows, c), F32),
        grid=(nrows // rb, c // cb),
        in_specs=[pl.BlockSpec((rb, cb), lambda i, j: (r0 + i, j0 + j)),
                  pl.BlockSpec((kw, cb), lambda i, j: (0, j)),
                  pl.BlockSpec((1, cb), lambda i, j: (0, j))],
        out_specs=pl.BlockSpec((rb, cb), lambda i, j: (i, j)),
        compiler_params=_cparams(("parallel", "parallel")),
        name="ssd_conv",
    )(x, w, b.reshape(1, c))


def _ssd_scan_kernel(x_ref, b_ref, c_ref, dt_ref, dtt_ref, bias_ref, alog_ref, biast_ref, alogt_ref,
                     tri_ref, trit_ref, e_ref, s0_ref, y_ref, sf_ref, st_ref,
                     *, heads, hdim, nstate, groups, has_init):
    d = pl.program_id(0)
    c = pl.program_id(2)
    nc = pl.num_programs(2)
    hpg = heads // groups
    gw = hpg * hdim

    @pl.when(c == 0)
    def _():
        if has_init:
            for g in range(groups):
                st_ref[pl.ds(g * nstate, nstate), :] = s0_ref[0, 0, pl.ds(g * gw, gw), :].T
        else:
            st_ref[...] = jnp.zeros_like(st_ref)

    tri = tri_ref[0]
    trit = trit_ref[0]
    lch = tri.shape[0]
    e = e_ref[...]
    dtp = _softplus(dt_ref[...] + bias_ref[0])
    a = -jnp.exp(alog_ref[0])
    da = dtp * a
    cum = _dot_exact_l(tri, da)
    dat = _softplus(dtt_ref[...] + biast_ref[0]) * (-jnp.exp(alogt_ref[0]))
    cumt = _dot_exact_r(dat, trit)
    total = jnp.sum(da, axis=0, keepdims=True)
    tot8 = jnp.broadcast_to(total, (SUBLANES, LANES))
    dt_full = _dot_exact_r(dtp, e)
    din_full = _dot_exact_r(jnp.exp(cum), e)
    dst_full = _dot_exact_r(jnp.exp(total - cum), e)
    tot_full = _dot_exact_r(jnp.exp(tot8), e)[0:1, :]
    x = x_ref[...]
    xdt = x * dt_full
    xdec = (xdt * dst_full).astype(BF16)
    xdt_b = xdt.astype(BF16)
    bm = b_ref[...]
    cm = c_ref[...]
    visible = tri > 0
    for g in range(groups):
        bg = bm[:, g * nstate:(g + 1) * nstate]
        cg = cm[:, g * nstate:(g + 1) * nstate].astype(BF16)
        bgb = bg.astype(BF16)
        cb = lax.dot_general(cg, bgb, (((1,), (1,)), ((), ())), preferred_element_type=F32)
        st_g = st_ref[pl.ds(g * nstate, nstate), :]
        y_off = _dot(cg, st_g.astype(BF16)) * din_full[:, g * gw:(g + 1) * gw]
        for hh in range(hpg):
            h = g * hpg + hh
            seg = cum[:, h:h + 1] - cumt[h:h + 1, :]
            lmat = jnp.exp(jnp.where(visible, seg, -jnp.inf))
            gmat = (cb * lmat).astype(BF16)
            yd = _dot(gmat, xdt_b[:, h * hdim:(h + 1) * hdim])
            y_ref[0, :, pl.ds(h * hdim, hdim)] = yd + y_off[:, hh * hdim:(hh + 1) * hdim]
        upd = _dot(bg.T.astype(BF16), xdec[:, g * gw:(g + 1) * gw])
        st_ref[pl.ds(g * nstate, nstate), :] = st_g * tot_full[:, g * gw:(g + 1) * gw] + upd

    @pl.when(c == nc - 1)
    def _():
        for g in range(groups):
            sf_ref[0, 0, pl.ds(g * gw, gw), :] = st_ref[pl.ds(g * nstate, nstate), :].T


def _ssd_scan_call(xs, bm, cm, zdt, dt_blk, dtt, row0, bsz, seq_len, pw, s0):
    hp = xs.shape[1]
    gn = bm.shape[1]
    heads = pw["heads"]
    hdim = hp // heads
    nstate = gn // SSD_GROUPS
    lch = min(SSD_CHUNK, seq_len)
    nc = seq_len // lch
    assert seq_len % lch == 0 and row0 % lch == 0
    c0 = row0 // lch
    has_init = s0 is not None
    if s0 is None:
        s0 = jnp.zeros((1, 1, hp, nstate), F32)

    def cidx(d, b, c):
        return b * nc + c + d * (nc - 1 - 2 * c)

    row = lambda d, b, c: (cidx(d, b, c), 0)
    s0_map = (lambda d, b, c: (b, d, 0, 0)) if has_init else (lambda d, b, c: (0, 0, 0, 0))
    dirp = pl.BlockSpec((1, 1, LANES), lambda d, b, c: (d, 0, 0))
    dirt = pl.BlockSpec((1, heads, 1), lambda d, b, c: (d, 0, 0))
    return pl.pallas_call(
        functools.partial(_ssd_scan_kernel, heads=heads, hdim=hdim, nstate=nstate,
                          groups=SSD_GROUPS, has_init=has_init),
        out_shape=[jax.ShapeDtypeStruct((N_DIR, bsz * seq_len, hp), F32),
                   jax.ShapeDtypeStruct((bsz, N_DIR, hp, nstate), F32)],
        grid=(N_DIR, bsz, nc),
        in_specs=[pl.BlockSpec((lch, hp), row),
                  pl.BlockSpec((lch, gn), row),
                  pl.BlockSpec((lch, gn), row),
                  pl.BlockSpec((lch, LANES), lambda d, b, c: (c0 + cidx(d, b, c), dt_blk)),
                  pl.BlockSpec((heads, lch), lambda d, b, c: (0, c0 + cidx(d, b, c))),
                  dirp, dirp, dirt, dirt,
                  pl.BlockSpec((1, lch, lch), lambda d, b, c: (d, 0, 0)),
                  pl.BlockSpec((1, lch, lch), lambda d, b, c: (d, 0, 0)),
                  pl.BlockSpec((LANES, hp), lambda d, b, c: (0, 0)),
                  pl.BlockSpec((1, 1, hp, nstate), s0_map)],
        out_specs=[pl.BlockSpec((1, lch, hp), lambda d, b, c: (d, cidx(d, b, c), 0)),
                   pl.BlockSpec((1, 1, hp, nstate), lambda d, b, c: (b, d, 0, 0))],
        scratch_shapes=[pltpu.VMEM((gn, hp // SSD_GROUPS), F32)],
        compiler_params=_cparams(("arbitrary", "arbitrary", "arbitrary")),
        name="ssd_scan",
    )(xs, bm, cm, zdt, dtt, pw["bias"], pw["alog"], pw["bias_t"], pw["alog_t"],
      pw["tri"][:, :lch, :lch], pw["trit"][:, :lch, :lch], pw["expand"], s0)


def _ssd_post_kernel(x_ref, yf_ref, yb_ref, z_ref, d_ref, g_ref, acc_ref, o_ref):
    del acc_ref
    y = (d_ref[...] * x_ref[...] + yf_ref[0] + yb_ref[0]) * _silu(z_ref[...])
    y = y * lax.rsqrt(jnp.mean(y * y, axis=-1, keepdims=True) + EPS) * g_ref[...]
    o_ref[...] = y.astype(o_ref.dtype)


def _ssd_post_call(xs, ydir, zz, row0, d_full, g, tm, acc):
    n, hp = xs.shape
    r0 = row0 // tm
    return pl.pallas_call(
        _ssd_post_kernel,
        out_shape=jax.ShapeDtypeStruct(acc.shape, acc.dtype),
        grid=(n // tm,),
        in_specs=[pl.BlockSpec((tm, hp), lambda i: (i, 0)),
                  pl.BlockSpec((1, tm, hp), lambda i: (0, i, 0)),
                  pl.BlockSpec((1, tm, hp), lambda i: (1, i, 0)),
                  pl.BlockSpec((tm, hp), lambda i: (r0 + i, 0)),
                  pl.BlockSpec((1, hp), lambda i: (0, 0)),
                  pl.BlockSpec((1, hp), lambda i: (0, 0)),
                  pl.BlockSpec(memory_space=pl.ANY)],
        out_specs=pl.BlockSpec((tm, hp), lambda i: (r0 + i, 0)),
        input_output_aliases={6: 0},
        compiler_params=_cparams(("parallel",)),
        name="ssd_post",
    )(xs, ydir, ydir, zz, d_full, g, acc)


def _s5_kernel(u_ref, bmat_ref, cmat_ref, lam_ref, s0_ref, y_ref, sf_ref,
               st_ref, buf_ref, *, slab_in, slab_state):
    d = pl.program_id(0)
    tb = pl.program_id(2)
    ntb = pl.num_programs(2)
    nb, steps = u_ref.shape[0], u_ref.shape[1]
    nslab = u_ref.shape[2] // slab_in
    spl = buf_ref.shape[0]
    ncol = slab_state // LANES
    assert slab_in == LANES

    @pl.when(tb == 0)
    def _():
        st_ref[...] = s0_ref[0]

    for s_base in range(0, nslab, spl):
        lam = []
        init = []
        for q in range(spl):
            s = s_base + q
            u = jnp.swapaxes(u_ref[:, :, pl.ds(s * slab_in, slab_in)], 0, 1)
            bu = _dot(u.reshape(steps * nb, slab_in).astype(BF16), bmat_ref[0, s])
            for c in range(2 * ncol):
                buf_ref[q, c] = bu[:, c * LANES:(c + 1) * LANES]
            for c in range(ncol):
                lanes = pl.ds(s * slab_state + c * LANES, LANES)
                lam.append((jnp.broadcast_to(lam_ref[0, 0, :, lanes], (nb, LANES)),
                            jnp.broadcast_to(lam_ref[0, 1, :, lanes], (nb, LANES))))
                init.append(st_ref[0, :, lanes])
                init.append(st_ref[1, :, lanes])

        def step(i, carry):
            te = i + d * (steps - 1 - 2 * i)
            rows = pl.ds(pl.multiple_of(te * nb, nb), nb)
            new = []
            for q in range(spl):
                for c in range(ncol):
                    lr, li = lam[q * ncol + c]
                    s_re = carry[2 * (q * ncol + c)]
                    s_im = carry[2 * (q * ncol + c) + 1]
                    n_re = lr * s_re - li * s_im + buf_ref[q, c, rows, :]
                    n_im = lr * s_im + li * s_re + buf_ref[q, ncol + c, rows, :]
                    buf_ref[q, c, rows, :] = n_re
                    buf_ref[q, ncol + c, rows, :] = n_im
                    new += [n_re, n_im]
            return tuple(new)

        fin = lax.fori_loop(0, steps, step, tuple(init), unroll=4)
        for q in range(spl):
            s = s_base + q
            for c in range(ncol):
                lanes = pl.ds(s * slab_state + c * LANES, LANES)
                st_ref[0, :, lanes] = fin[2 * (q * ncol + c)]
                st_ref[1, :, lanes] = fin[2 * (q * ncol + c) + 1]
            states = jnp.concatenate([buf_ref[q, c] for c in range(2 * ncol)], axis=1)
            y = _dot(states.astype(BF16), cmat_ref[0, s])
            y_ref[0, :, :, pl.ds(s * slab_in, slab_in)] = jnp.swapaxes(y.reshape(steps, nb, slab_in), 0, 1)

    @pl.when(tb == ntb - 1)
    def _():
        sf_ref[0] = st_ref[...]


def _s5_call(u3, seq0, bsz, pw, s0):
    _, t, w = u3.shape
    nb = min(SUBLANES, bsz)
    tb = min(S5_TB, t)
    assert t % tb == 0 and bsz % nb == 0 and seq0 % nb == 0
    ntb = t // tb
    b0 = seq0 // nb
    gp = pw["lam"].shape[-1]
    slab_in = pw["slab_in"]
    slab_state = pw["slab_state"]
    nslab = w // slab_in
    spl = 2 if nslab % 2 == 0 else 1

    def tidx(d, i):
        return i + d * (ntb - 1 - 2 * i)

    return pl.pallas_call(
        functools.partial(_s5_kernel, slab_in=slab_in, slab_state=slab_state),
        out_shape=[jax.ShapeDtypeStruct((N_DIR, bsz, t, w), F32),
                   jax.ShapeDtypeStruct((N_DIR, 2, bsz, gp), F32)],
        grid=(N_DIR, bsz // nb, ntb),
        in_specs=[pl.BlockSpec((nb, tb, w), lambda d, b, i: (b0 + b, tidx(d, i), 0)),
                  pl.BlockSpec((1, nslab, slab_in, 2 * slab_state), lambda d, b, i: (d, 0, 0, 0)),
                  pl.BlockSpec((1, nslab, 2 * slab_state, slab_in), lambda d, b, i: (d, 0, 0, 0)),
                  pl.BlockSpec((1, 2, 1, gp), lambda d, b, i: (d, 0, 0, 0)),
                  pl.BlockSpec((1, 2, nb, gp), lambda d, b, i: (d, 0, b, 0))],
        out_specs=[pl.BlockSpec((1, nb, tb, w), lambda d, b, i: (d, b, tidx(d, i), 0)),
                   pl.BlockSpec((1, 2, nb, gp), lambda d, b, i: (d, 0, b, 0))],
        scratch_shapes=[pltpu.VMEM((2, nb, gp), F32),
                        pltpu.VMEM((spl, 2 * slab_state // LANES, nb * tb, LANES), F32)],
        compiler_params=_cparams(("arbitrary", "arbitrary", "arbitrary")),
        name="s5_scan",
    )(u3, pw["bmat"], pw["cmat"], pw["lam"], s0)


def _s5_post_kernel(u_ref, yf_ref, yb_ref, d_ref, acc_ref, o_ref):
    del acc_ref
    y = d_ref[...] * u_ref[...] + yf_ref[0] + yb_ref[0]
    o_ref[...] = jax.nn.gelu(y).astype(o_ref.dtype)


def _s5_post_call(u, row0, ydir, d_full, tm, acc):
    _, m, w = ydir.shape
    r0 = row0 // tm
    return pl.pallas_call(
        _s5_post_kernel,
        out_shape=jax.ShapeDtypeStruct(acc.shape, acc.dtype),
        grid=(m // tm,),
        in_specs=[pl.BlockSpec((tm, w), lambda i: (r0 + i, 0)),
                  pl.BlockSpec((1, tm, w), lambda i: (0, i, 0)),
                  pl.BlockSpec((1, tm, w), lambda i: (1, i, 0)),
                  pl.BlockSpec((1, w), lambda i: (0, 0)),
                  pl.BlockSpec(memory_space=pl.ANY)],
        out_specs=pl.BlockSpec((tm, w), lambda i: (r0 + i, 0)),
        input_output_aliases={4: 0},
        compiler_params=_cparams(("parallel",)),
        name="s5_post",
    )(u, ydir, ydir, d_full, acc)


def _head_selectors(cb, head):
    nsel = LANES
    col = np.arange(cb)[:, None] // head
    sel = (col == np.arange(nsel)[None, :]).astype(np.float32)
    return jnp.asarray(sel, BF16), jnp.asarray(sel.T, BF16)


def _s5_weights(p, l, slab_groups):
    g, pst = p["s5_lambda_re"].shape[2:]
    cg = p["s5_b_re"].shape[-1]
    nslab = g // slab_groups
    eye = jnp.eye(slab_groups, dtype=F32)
    bmats, cmats, lams = [], [], []
    for d in range(N_DIR):
        lam_re, lam_im = p["s5_lambda_re"][l, d], p["s5_lambda_im"][l, d]
        delta = jnp.exp(p["s5_log_dt"][l, d])[:, None]
        mag = jnp.exp(lam_re * delta)
        lb_re, lb_im = mag * jnp.cos(lam_im * delta), mag * jnp.sin(lam_im * delta)
        den = lam_re * lam_re + lam_im * lam_im
        q_re = ((lb_re - 1.0) * lam_re + lb_im * lam_im) / den
        q_im = (lb_im * lam_re - (lb_re - 1.0) * lam_im) / den
        b_re, b_im = p["s5_b_re"][l, d], p["s5_b_im"][l, d]
        bb_re = q_re[..., None] * b_re - q_im[..., None] * b_im
        bb_im = q_re[..., None] * b_im + q_im[..., None] * b_re

        def in_blocks(bb):
            x = bb.reshape(nslab, slab_groups, pst, cg)
            return jnp.einsum("sgpc,gh->sgchp", x, eye).reshape(nslab, slab_groups * cg, slab_groups * pst)

        def out_blocks(cc):
            x = cc.reshape(nslab, slab_groups, cg, pst)
            return jnp.einsum("sgcp,gh->sgphc", x, eye).reshape(nslab, slab_groups * pst, slab_groups * cg)

        bmats.append(jnp.concatenate([in_blocks(bb_re), in_blocks(bb_im)], axis=-1))
        cmats.append(jnp.concatenate([out_blocks(p["s5_c_re"][l, d]), -out_blocks(p["s5_c_im"][l, d])], axis=-2))
        lams.append(jnp.stack([lb_re.reshape(1, g * pst), lb_im.reshape(1, g * pst)]))
    return dict(bmat=jnp.stack(bmats).astype(BF16), cmat=jnp.stack(cmats).astype(BF16),
                lam=jnp.stack(lams), slab_in=slab_groups * cg, slab_state=slab_groups * pst)


def _pad_cols(w, n):
    return jnp.pad(w, ((0, 0), (0, n - w.shape[1])))


def kernel(x_prompt, x_sample, state_rwkv, state_ssd, state_s5_re, state_s5_im, c, c_ctx, w_mod, b_mod, norm_g, ffn_w_in, ffn_w_out, w_in, rwkv_mu, rwkv_w0, rwkv_w2, rwkv_a0, rwkv_a2, rwkv_g2, rwkv_k_k, rwkv_k_a, rwkv_r_k, rwkv_ln_g, rwkv_ln_b, w_proj_a, ssd_conv_w, ssd_conv_b, ssd_dt_bias, ssd_a_log, ssd_d, ssd_norm_g, w_proj_b, s5_lambda_re, s5_lambda_im, s5_log_dt, s5_b_re, s5_b_im, s5_c_re, s5_c_im, s5_d, w_proj_c, w_out, final_norm_g):
    p = dict(s5_lambda_re=s5_lambda_re, s5_lambda_im=s5_lambda_im, s5_log_dt=s5_log_dt,
             s5_b_re=s5_b_re, s5_b_im=s5_b_im, s5_c_re=s5_c_re, s5_c_im=s5_c_im)
    bp, tp, dm = x_prompt.shape
    bs, ts, _ = x_sample.shape
    n_p, n_s = bp * tp, bs * ts
    m = n_p + n_s
    depth = w_mod.shape[0]
    d_ff = ffn_w_out.shape[2]
    ffn_w_out_bf = ffn_w_out.astype(BF16)
    rw = rwkv_k_k.shape[1]
    rh = rw // RWKV_HEAD
    rd, ra, rg = rwkv_w2.shape[2], rwkv_a2.shape[2], rwkv_g2.shape[1]
    lora = rd + ra + rg
    sh = ssd_d.shape[1]
    sw = ssd_norm_g.shape[1]
    xbc_w = ssd_conv_w.shape[2]
    gn = (xbc_w - sw) // 2
    cw = s5_d.shape[1]
    s5_g, s5_p = s5_lambda_re.shape[2:]
    s5_cg = cw // s5_g
    slab_groups = max(1, min(s5_g, LANES // s5_cg))

    tm = _pick(math.gcd(n_p, ts), 1024, SUBLANES)
    tm_wide = _pick(m, 2 * tm, tm)
    n_tiles = m // tm
    tile_cond = np.array([0 if i * tm < n_p else 1 + (i * tm - n_p) // ts for i in range(n_tiles)])

    x = jnp.concatenate([x_prompt.reshape(n_p, dm), x_sample.reshape(n_s, dm)], axis=0)
    ncond = 1 + bs
    cond = jnp.concatenate([c_ctx[None, :], c], axis=0)
    cond8 = jnp.pad(cond, ((0, _round_up(ncond, SUBLANES) - ncond), (0, 0)))

    groups = [dict(b=bp, t=tp, row0=0, n=n_p, grid_w=None),
              dict(b=bs, t=ts, row0=n_p, n=n_s, grid_w=GRID_W)]

    sel, selt = _head_selectors(LANES, RWKV_HEAD)
    tri_f = np.tril(np.ones((SSD_CHUNK, SSD_CHUNK), np.float32))
    tri = jnp.asarray(np.stack([tri_f, tri_f.T]), BF16)
    trit = jnp.asarray(np.stack([tri_f.T, tri_f]), BF16)
    expand = jnp.asarray((np.arange(LANES)[:, None] == (np.arange(sw)[None, :] // (sw // sh))).astype(np.float32), BF16)

    new_a, new_b, new_re, new_im = [], [], [], []
    for l in range(depth):
        mod = _mod_call(cond8, w_mod, b_mod, l)
        mods = mod.reshape(-1, N_MOD, dm)[tile_cond]
        sh1, sc1, g1, sh2, sc2, g2, sh3, sc3, g3 = [mods[:, i:i + 1, :] for i in range(N_MOD)]

        def ffn(x, idx, shv, scv, gv):
            wi = ffn_w_in[l, idx].astype(BF16)
            h = _norm_mod_call(x, norm_g[l, idx * 2], shv, scv, tm)
            a = _mm_swiglu_call(h, wi, wi[:, d_ff:], tm)
            return _mm_res_call(a, ffn_w_out_bf, x, gv, 0.5, tm, w_index=(l, idx))

        x = ffn(x, 0, sh1, sc1, g1)

        h = _norm_mod_call(x, norm_g[l, 1], sh2, sc2, tm)
        wl_in = w_in[l]
        offs = np.cumsum([0, rw, rw, rw, lora, sw, sw, gn, gn, sh, cw, 3 * dm])
        segs = [wl_in[:, offs[i]:offs[i + 1]].astype(BF16) for i in range(11)]
        w_small = jnp.concatenate([segs[3], segs[6], segs[7], _pad_cols(segs[8], LANES)], axis=1)
        assert lora % gn == 0 and (lora + 2 * gn) % LANES == 0
        bm_blk, cm_blk, dt_blk = lora // gn, lora // gn + 1, (lora + 2 * gn) // LANES
        zr, zk, zv, zz, zxs, zc, zg, zs = [_mm_call(h, wseg, tm_wide)
                                           for wseg in (segs[0], segs[1], segs[2], segs[4], segs[5],
                                                        segs[9], segs[10], w_small)]
        dtt = zs[:, lora + 2 * gn:lora + 2 * gn + sh].T

        mu = rwkv_mu[l]
        pw_r = dict(ranks=(rd, ra, rg), sel=sel, selt=selt,
                    mu_r=mu[None, :rw], mu_k=mu[None, rw:2 * rw], mu_v=mu[None, 2 * rw:3 * rw],
                    mu_l=mu[None, 3 * rw:],
                    k_k=rwkv_k_k[l][None], k_a=rwkv_k_a[l][None], r_k=rwkv_r_k[l].reshape(1, rw),
                    w0=rwkv_w0[l][:, None, :], a0=rwkv_a0[l][:, None, :],
                    w2=rwkv_w2[l].astype(BF16), a2=rwkv_a2[l].astype(BF16), g2=rwkv_g2[l].astype(BF16),
                    ln_g=rwkv_ln_g[l][None], ln_b=rwkv_ln_b[l][None])
        y_a, fin_a = jnp.zeros((m, rw), BF16), None
        bpb = (LANES // 2) // rh
        for gi, gr in enumerate(groups):
            b_, t_ = gr["b"], gr["t"]
            nblk = b_ // bpb
            rv_, wb_, kk_, bonus_, g_ = _rwkv_prep_call(
                zr, zk, zv, zs, gr["row0"], gr["n"], t_, gr["grid_w"], pw_r)
            if gi == 0:
                s0c = jnp.zeros((RWKV_HEAD, RWKV_HEAD, nblk * LANES), F32)
            else:
                s0c = state_rwkv[:, l].reshape(nblk, bpb, N_DIR, rh, RWKV_HEAD, RWKV_HEAD)
                s0c = s0c.transpose(5, 4, 0, 2, 1, 3).reshape(RWKV_HEAD, RWKV_HEAD, nblk * LANES)
            o_f, o_b, sf_c = _rwkv_scan_call(rv_, wb_, kk_, s0c)
            y_a = _rwkv_post_call(o_f, o_b, bonus_, g_, pw_r, gr["row0"], y_a)
            if gi == 0:
                fin_a = sf_c.reshape(RWKV_HEAD, RWKV_HEAD, nblk, N_DIR, bpb, rh)
                fin_a = fin_a.transpose(2, 4, 3, 5, 1, 0).reshape(b_, N_DIR, rh, RWKV_HEAD, RWKV_HEAD)
        new_a.append(fin_a)

        cw_l, cb_l = ssd_conv_w[l], ssd_conv_b[l]
        pw_s = dict(heads=sh, tri=tri, trit=trit, expand=expand,
                    bias=jnp.pad(ssd_dt_bias[l], ((0, 0), (0, LANES - sh)))[:, None, :],
                    alog=jnp.pad(ssd_a_log[l], ((0, 0), (0, LANES - sh)))[:, None, :],
                    bias_t=ssd_dt_bias[l][:, :, None], alog_t=ssd_a_log[l][:, :, None])
        y_b, fin_b = jnp.zeros((m, sw), BF16), None
        for gi, gr in enumerate(groups):
            xs_ = _conv_silu_call(zxs, gr["row0"], gr["n"], gr["t"], cw_l[:, :sw], cb_l[:sw])
            bm_ = _conv_silu_call(zs, gr["row0"], gr["n"], gr["t"], cw_l[:, sw:sw + gn], cb_l[sw:sw + gn], bm_blk)
            cm_ = _conv_silu_call(zs, gr["row0"], gr["n"], gr["t"], cw_l[:, sw + gn:], cb_l[sw + gn:], cm_blk)
            s0s = None if gi == 0 else state_ssd[:, l].reshape(gr["b"], N_DIR, sw, gn // SSD_GROUPS)
            ydir, sf_s = _ssd_scan_call(xs_, bm_, cm_, zs, dt_blk, dtt, gr["row0"], gr["b"], gr["t"], pw_s, s0s)
            y_b = _ssd_post_call(xs_, ydir, zz, gr["row0"],
                                 jnp.repeat(ssd_d[l], sw // sh)[None, :], ssd_norm_g[l][None, :], tm, y_b)
            if gi == 0:
                fin_b = sf_s.reshape(gr["b"], N_DIR, sh, sw // sh, gn // SSD_GROUPS)
        new_b.append(fin_b)

        pw_c = _s5_weights(p, l, slab_groups)
        y_c, fin_re, fin_im = jnp.zeros((m, cw), BF16), None, None
        for gi, gr in enumerate(groups):
            b_, t_ = gr["b"], gr["t"]
            assert gr["row0"] % t_ == 0
            if gi == 0:
                s0 = jnp.zeros((N_DIR, 2, b_, s5_g * s5_p), F32)
            else:
                s0 = jnp.stack([state_s5_re[:, l], state_s5_im[:, l]])
                s0 = s0.reshape(2, b_, N_DIR, s5_g * s5_p).transpose(2, 0, 1, 3)
            y_dir, sf = _s5_call(zc.reshape(m // t_, t_, cw), gr["row0"] // t_, b_, pw_c, s0)
            y_dir = y_dir.reshape(N_DIR, gr["n"], cw)
            y_c = _s5_post_call(zc, gr["row0"], y_dir, s5_d[l][None, :], tm, y_c)
            if gi == 0:
                fin = sf.reshape(N_DIR, 2, b_, s5_g, s5_p).transpose(1, 2, 0, 3, 4)
                fin_re, fin_im = fin[0], fin[1]
        new_re.append(fin_re)
        new_im.append(fin_im)

        wc = w_proj_c[l]
        merged = _merge_call(y_a, y_b, y_c, zg, w_proj_a[l].astype(BF16), w_proj_b[l].astype(BF16),
                             jnp.stack([wc[:, :dm], wc[:, dm:]]).astype(BF16), tm)
        x = _mm_res_call(merged, w_out[l].astype(BF16), x, g2, 1.0, tm)

        x = ffn(x, 1, sh3, sc3, g3)

    y_p = _final_norm_call(x, 0, n_p, final_norm_g, tm)
    y_s = _final_norm_call(x, n_p, n_s, final_norm_g, tm)
    return (y_p.reshape(bp, tp, dm), y_s.reshape(bs, ts, dm),
            jnp.stack(new_a, axis=1), jnp.stack(new_b, axis=1),
            jnp.stack(new_re, axis=1), jnp.stack(new_im, axis=1))
```
